```python
import math
import jax, jax.numpy as jnp
from jax import lax
import numpy as np

D_MODEL = 1024
BATCH = 4
SEQ = 8192
DEPTH = 2

N_META = 16
D_FF = 2816
NORM_EPS = 1e-6

RET_HEADS = 4
RET_DK = 128
RET_DV = 128
RET_CHUNK = 128
GDN_HEADS = 4
GDN_DK = 128
GDN_DV = 128
GDN_CONV = 4
GDN_CHUNK = 64
GDN_QKV = 2 * GDN_HEADS * GDN_DK + GDN_HEADS * GDN_DV
SWA_HEADS = 8
SWA_KV_HEADS = 2
SWA_DH = 64
SWA_WINDOW = 128
SB_HEADS = 8
SB_DH = 64
ATT_BLOCK = 128

AB_WIDTHS = (RET_HEADS * RET_DK, RET_HEADS * RET_DK, RET_HEADS * RET_DV, RET_HEADS * RET_DV,
             GDN_QKV, GDN_HEADS * GDN_DV, GDN_HEADS, GDN_HEADS)
AB_IN = 2 * RET_HEADS * RET_DK + 2 * RET_HEADS * RET_DV + GDN_QKV + GDN_HEADS * GDN_DV + 2 * GDN_HEADS
AB_OUT = RET_HEADS * RET_DV + GDN_HEADS * GDN_DV
CD_WIDTHS = (SWA_HEADS * SWA_DH, SWA_KV_HEADS * SWA_DH, SWA_KV_HEADS * SWA_DH,
             SB_HEADS * SB_DH, SB_HEADS * SB_DH, SB_HEADS * SB_DH)
CD_IN = SWA_HEADS * SWA_DH + 2 * SWA_KV_HEADS * SWA_DH + 3 * SB_HEADS * SB_DH
CD_OUT = SWA_HEADS * SWA_DH + SB_HEADS * SB_DH

kernel_name = "hybrid_retention_gdn_swa_stickbreak_macaron"


def rms_norm(x, g):
    xf = x.astype(jnp.float32)
    y = xf * lax.rsqrt(jnp.mean(xf * xf, axis=-1, keepdims=True) + NORM_EPS)
    return (y * g.astype(jnp.float32)).astype(x.dtype)


def l2_normalize(x):
    return x * lax.rsqrt(jnp.sum(x * x, axis=-1, keepdims=True) + NORM_EPS)


def swiglu(x, w_gate, w_up, w_down):
    return (jax.nn.silu(x @ w_gate) * (x @ w_up)) @ w_down


def split_cols(t, widths):
    return jnp.split(t, np.cumsum(widths)[:-1].tolist(), axis=-1)


def pad_front(t, n):
    return jnp.pad(t, [(0, 0), (n, 0)] + [(0, 0)] * (t.ndim - 2))


def rotate_pairs(x, pos):
    d = x.shape[-1]
    inv_freq = 1.0 / (10000.0 ** jnp.linspace(0.0, 1.0, d // 2, dtype=jnp.float32))
    ang = pos[:, None] * inv_freq[None, :]
    cos = jnp.cos(ang)[None, :, None, :]
    sin = jnp.sin(ang)[None, :, None, :]
    x1, x2 = x[..., 0::2], x[..., 1::2]
    return jnp.stack([x1 * cos - x2 * sin, x2 * cos + x1 * sin], axis=-1).reshape(x.shape)


def causal_depthwise_conv(x, w):
    K, ch = w.shape
    return lax.conv_general_dilated(x, w[:, None, :], window_strides=(1,), padding=[(K - 1, 0)],
                                    dimension_numbers=('NWC', 'WIO', 'NWC'), feature_group_count=ch)


def retention_chunked(q, k, v):
    B, Lp, H, dk = q.shape
    dv = v.shape[-1]
    C = RET_CHUNK
    N = Lp // C
    q = q.reshape(B, N, C, H, dk)
    k = k.reshape(B, N, C, H, dk)
    v = v.reshape(B, N, C, H, dv)
    log_gamma = jnp.log1p(-jnp.exp2(-5.0 - jnp.arange(H, dtype=jnp.float32)))
    idx = jnp.arange(C, dtype=jnp.float32)
    diff = idx[:, None] - idx[None, :]
    intra_decay = jnp.where(diff >= 0, jnp.exp(jnp.maximum(diff, 0.0) * log_gamma[:, None, None]), 0.0)
    scores = jnp.einsum('bnihd,bnjhd->bnhij', q, k) * intra_decay
    intra = jnp.einsum('bnhij,bnjhe->bnihe', scores, v)
    zeta = jnp.exp((C - 1.0 - idx)[:, None] * log_gamma[None, :])
    xi = jnp.exp((idx + 1.0)[:, None] * log_gamma[None, :])
    kv = jnp.einsum('bnjhd,bnjhe->bnhde', k * zeta[..., None], v)
    chunk_decay = jnp.exp(C * log_gamma)[:, None, None]

    def step(S, kv_n):
        return S * chunk_decay + kv_n, S

    _, s_prev = lax.scan(step, jnp.zeros_like(kv[:, 0]), jnp.moveaxis(kv, 1, 0))
    s_prev = jnp.moveaxis(s_prev, 0, 1)
    cross = jnp.einsum('bnihd,bnhde->bnihe', q * xi[..., None], s_prev)
    return (intra + cross).reshape(B, Lp, H, dv)


def gated_delta_chunked(q, k, v, g, beta):
    B, Lp, H, dk = q.shape
    dv = v.shape[-1]
    C = GDN_CHUNK
    N = Lp // C

    def chunks(t):
        return jnp.moveaxis(t.reshape((B, N, C, H) + t.shape[3:]), 2, 3)

    q = chunks(q) * dk ** -0.5
    k = chunks(k)
    v = chunks(v)
    g = chunks(g)
    beta = chunks(beta)
    g_cum = jnp.cumsum(g, axis=-1)
    idx = jnp.arange(C)
    incl = idx[:, None] >= idx[None, :]
    strict = idx[:, None] > idx[None, :]
    gdiff = g_cum[..., :, None] - g_cum[..., None, :]
    decay = jnp.where(incl, jnp.exp(jnp.where(incl, gdiff, 0.0)), 0.0)
    k_beta = k * beta[..., None]
    a_mat = jnp.where(strict, jnp.einsum('bnhid,bnhjd->bnhij', k_beta, k) * decay, 0.0)
    system = a_mat + jnp.eye(C, dtype=a_mat.dtype)
    rhs = jnp.concatenate([v * beta[..., None], k_beta * jnp.exp(g_cum)[..., None]], axis=-1)
    sol = lax.linalg.triangular_solve(system, rhs, left_side=True, lower=True, unit_diagonal=True)
    u_vec, w_vec = sol[..., :dv], sol[..., dv:]
    qk = jnp.where(incl, jnp.einsum('bnhid,bnhjd->bnhij', q, k) * decay, 0.0)
    q_dec = q * jnp.exp(g_cum)[..., None]
    g_last = g_cum[..., -1]
    k_tail = k * jnp.exp(g_last[..., None] - g_cum)[..., None]

    def step(S, xs):
        u_n, w_n, qk_n, qd_n, kt_n, gl_n = xs
        v_new = u_n - jnp.einsum('bhcd,bhde->bhce', w_n, S)
        o_n = jnp.einsum('bhcd,bhde->bhce', qd_n, S) + jnp.einsum('bhij,bhje->bhie', qk_n, v_new)
        S = S * jnp.exp(gl_n)[..., None, None] + jnp.einsum('bhcd,bhce->bhde', kt_n, v_new)
        return S, o_n

    xs = tuple(jnp.moveaxis(t, 1, 0) for t in (u_vec, w_vec, qk, q_dec, k_tail, g_last))
    _, o = lax.scan(step, jnp.zeros((B, H, dk, dv), q.dtype), xs)
    return jnp.moveaxis(o, 0, 1).transpose(0, 1, 3, 2, 4).reshape(B, Lp, H, dv)


def swa_sink_attention(q, k, v, sinks):
    B, Lp, HQ, dh = q.shape
    HKV = k.shape[2]
    G = HQ // HKV
    BLK = ATT_BLOCK
    NB = Lp // BLK
    pad = BLK - N_META
    qb = q.reshape(B, NB, BLK, HKV, G, dh) * dh ** -0.5
    kb = k.reshape(B, NB, BLK, HKV, dh)
    vb = v.reshape(B, NB, BLK, HKV, dh)
    shift = ((0, 0), (1, 0), (0, 0), (0, 0), (0, 0))
    k_prev = jnp.pad(kb, shift)[:, :-1]
    v_prev = jnp.pad(vb, shift)[:, :-1]
    meta_k = jnp.broadcast_to(k[:, None, pad:BLK], (B, NB, N_META, HKV, dh))
    meta_v = jnp.broadcast_to(v[:, None, pad:BLK], (B, NB, N_META, HKV, dh))
    keys = jnp.concatenate([meta_k, k_prev, kb], axis=2)
    vals = jnp.concatenate([meta_v, v_prev, vb], axis=2)
    blocks = jnp.arange(NB)
    qpos = blocks[:, None] * BLK + jnp.arange(BLK)[None, :]
    kpos = (blocks[:, None] - 1) * BLK + jnp.arange(2 * BLK)[None, :]
    dpos = qpos[:, :, None] - kpos[:, None, :]
    band = (dpos >= 0) & (dpos < SWA_WINDOW) & (kpos[:, None, :] >= BLK)
    meta_vis = (pad + jnp.arange(N_META))[None, None, :] <= qpos[:, :, None]
    mask = jnp.concatenate([jnp.broadcast_to(meta_vis, (NB, BLK, N_META)), band], axis=-1)
    s = jnp.einsum('bnqhgd,bnkhd->bnhgqk', qb, keys)
    s = jnp.where(mask[None, :, None, None], s, -jnp.inf)
    sink = jnp.broadcast_to(sinks.astype(s.dtype).reshape(1, 1, HKV, G, 1, 1), s.shape[:-1] + (1,))
    p = jax.nn.softmax(jnp.concatenate([s, sink], axis=-1), axis=-1)[..., :-1]
    o = jnp.einsum('bnhgqk,bnkhd->bnqhgd', p, vals)
    return o.reshape(B, Lp, HQ, dh)


def stick_breaking_attention(q, k, v):
    B, Lp, H, dh = q.shape
    BLK = ATT_BLOCK
    NB = Lp // BLK
    pad = BLK - N_META
    kpos = jnp.arange(Lp)
    qb = jnp.moveaxis(q.reshape(B, NB, BLK, H, dh), 1, 0)
    scale = dh ** -0.5

    def block(args):
        q_blk, n = args
        qpos = n * BLK + jnp.arange(BLK)
        valid = (kpos[None, :] < qpos[:, None]) & (kpos[None, :] >= pad)
        z = jnp.einsum('bqhd,bkhd->bhqk', q_blk, k) * scale
        log_beta = jax.nn.log_sigmoid(z)
        log_1m_beta = jnp.where(valid, jax.nn.log_sigmoid(-z), 0.0)
        log_stick = lax.cumsum(log_1m_beta, axis=3, reverse=True) - log_1m_beta
        a = jnp.where(valid, jnp.exp(log_beta + log_stick), 0.0)
        return jnp.einsum('bhqk,bkhd->bqhd', a, v)

    o = lax.map(block, (qb, jnp.arange(NB)))
    return jnp.moveaxis(o, 0, 1).reshape(B, Lp, H, dh)


def mixer_ab(u, w_in, conv_w, a_log, dt_bias, out_norm, w_out):
    B, L, _ = u.shape
    f32 = jnp.float32
    rq, rk, rv, rg, gqkv, gz, gb, ga = split_cols(u @ w_in, AB_WIDTHS)
    pos = jnp.arange(L, dtype=f32)
    rq = rotate_pairs(rq.astype(f32).reshape(B, L, RET_HEADS, RET_DK), pos)
    rk = rotate_pairs(rk.astype(f32).reshape(B, L, RET_HEADS, RET_DK), pos) * RET_DK ** -0.5
    rv = rv.astype(f32).reshape(B, L, RET_HEADS, RET_DV)
    rpad = RET_CHUNK - N_META
    ret = retention_chunked(pad_front(rq, rpad), pad_front(rk, rpad), pad_front(rv, rpad))[:, rpad:]
    mu = jnp.mean(ret, axis=-1, keepdims=True)
    var = jnp.mean(jnp.square(ret - mu), axis=-1, keepdims=True)
    ret = ((ret - mu) * lax.rsqrt(var + NORM_EPS)).reshape(B, L, -1) * jax.nn.silu(rg.astype(f32))
    qkv = jax.nn.silu(causal_depthwise_conv(gqkv.astype(f32), conv_w.astype(f32)))
    gq, gk, gv = split_cols(qkv, (GDN_HEADS * GDN_DK, GDN_HEADS * GDN_DK, GDN_HEADS * GDN_DV))
    gq = l2_normalize(gq.reshape(B, L, GDN_HEADS, GDN_DK))
    gk = l2_normalize(gk.reshape(B, L, GDN_HEADS, GDN_DK))
    gv = gv.reshape(B, L, GDN_HEADS, GDN_DV)
    beta = jax.nn.sigmoid(gb.astype(f32))
    g = -jnp.exp(a_log.astype(f32)) * jax.nn.softplus(ga.astype(f32) + dt_bias.astype(f32))
    gpad = GDN_CHUNK - N_META
    o = gated_delta_chunked(*(pad_front(t, gpad) for t in (gq, gk, gv, g, beta)))[:, gpad:]
    o = rms_norm(o, out_norm) * jax.nn.silu(gz.astype(f32).reshape(B, L, GDN_HEADS, GDN_DV))
    mixed = jnp.concatenate([ret, o.reshape(B, L, -1)], axis=-1).astype(u.dtype)
    return mixed @ w_out


def mixer_cd(u, w_in, sinks, w_out):
    B, L, _ = u.shape
    pad = ATT_BLOCK - N_META
    Lp = L + pad
    cq, ck, cv, sq, sk, sv = [pad_front(t.astype(jnp.float32), pad) for t in split_cols(u @ w_in, CD_WIDTHS)]
    swa = swa_sink_attention(cq.reshape(B, Lp, SWA_HEADS, SWA_DH), ck.reshape(B, Lp, SWA_KV_HEADS, SWA_DH),
                             cv.reshape(B, Lp, SWA_KV_HEADS, SWA_DH), sinks)
    sb = stick_breaking_attention(sq.reshape(B, Lp, SB_HEADS, SB_DH), sk.reshape(B, Lp, SB_HEADS, SB_DH),
                                  sv.reshape(B, Lp, SB_HEADS, SB_DH))
    mixed = jnp.concatenate([swa.reshape(B, Lp, -1), sb.reshape(B, Lp, -1)], axis=-1)[:, pad:]
    return mixed.astype(u.dtype) @ w_out


def setup_inputs(seed: int = 0) -> dict:
    key = jax.random.key(seed)
    ks = jax.random.split(key, 16)
    f32 = jnp.float32
    n_even = (DEPTH + 1) // 2
    n_odd = DEPTH // 2

    def dense(k, shape, fan_in):
        return jax.random.normal(k, shape, f32) * fan_in ** -0.5

    x = jax.random.normal(ks[0], (BATCH, SEQ, D_MODEL), f32)
    meta_tokens = jax.random.normal(ks[1], (N_META, D_MODEL), f32)
    norm_gains = 1.0 + 0.01 * jax.random.normal(ks[2], (DEPTH, 6, D_MODEL), f32)
    ffn_w_gate = dense(ks[3], (DEPTH, 2, D_MODEL, D_FF), D_MODEL)
    ffn_w_up = dense(ks[4], (DEPTH, 2, D_MODEL, D_FF), D_MODEL)
    ffn_w_down = dense(ks[5], (DEPTH, 2, D_FF, D_MODEL), D_FF)
    ab_w_in = dense(ks[6], (n_even, D_MODEL, AB_IN), D_MODEL)
    ab_conv_w = dense(ks[7], (n_even, GDN_CONV, GDN_QKV), GDN_CONV)
    ab_a_log = jnp.log(jax.random.uniform(ks[8], (n_even, GDN_HEADS), f32, 1.0, 16.0))
    dt = jnp.exp(jax.random.uniform(ks[9], (n_even, GDN_HEADS), f32, math.log(1e-3), math.log(1e-1)))
    ab_dt_bias = dt + jnp.log(-jnp.expm1(-dt))
    ab_out_norm = 1.0 + 0.01 * jax.random.normal(ks[10], (n_even, GDN_DV), f32)
    ab_w_out = dense(ks[11], (n_even, AB_OUT, D_MODEL), AB_OUT)
    cd_w_in = dense(ks[12], (n_odd, D_MODEL, CD_IN), D_MODEL)
    cd_sinks = jax.random.normal(ks[13], (n_odd, SWA_HEADS), f32)
    cd_w_out = dense(ks[14], (n_odd, CD_OUT, D_MODEL), CD_OUT)
    return {"x": x, "meta_tokens": meta_tokens, "norm_gains": norm_gains,
            "ffn_w_gate": ffn_w_gate, "ffn_w_up": ffn_w_up, "ffn_w_down": ffn_w_down,
            "ab_w_in": ab_w_in, "ab_conv_w": ab_conv_w, "ab_a_log": ab_a_log, "ab_dt_bias": ab_dt_bias,
            "ab_out_norm": ab_out_norm, "ab_w_out": ab_w_out,
            "cd_w_in": cd_w_in, "cd_sinks": cd_sinks, "cd_w_out": cd_w_out}


def reference(x, meta_tokens, norm_gains, ffn_w_gate, ffn_w_up, ffn_w_down,
              ab_w_in, ab_conv_w, ab_a_log, ab_dt_bias, ab_out_norm, ab_w_out,
              cd_w_in, cd_sinks, cd_w_out):
    B = x.shape[0]
    meta = jnp.broadcast_to(meta_tokens[None].astype(x.dtype), (B, N_META, D_MODEL))
    h = jnp.concatenate([meta, x], axis=1)
    for i in range(DEPTH):
        g = norm_gains[i]
        j = i // 2
        y = swiglu(rms_norm(h, g[0]), ffn_w_gate[i, 0], ffn_w_up[i, 0], ffn_w_down[i, 0])
        h = h + 0.5 * rms_norm(y, g[1])
        u = rms_norm(h, g[2])
        if i % 2 == 0:
            y = mixer_ab(u, ab_w_in[j], ab_conv_w[j], ab_a_log[j], ab_dt_bias[j], ab_out_norm[j], ab_w_out[j])
        else:
            y = mixer_cd(u, cd_w_in[j], cd_sinks[j], cd_w_out[j])
        h = h + rms_norm(y, g[3])
        y = swiglu(rms_norm(h, g[4]), ffn_w_gate[i, 1], ffn_w_up[i, 1], ffn_w_down[i, 1])
        h = h + 0.5 * rms_norm(y, g[5])
    return h[:, N_META:]
```

```python
import functools
import math

import numpy as np
import jax
import jax.numpy as jnp
from jax import lax
from jax.experimental import pallas as pl
from jax.experimental.pallas import tpu as pltpu

F32 = jnp.float32
BF16 = jnp.bfloat16

N_META = 16
NORM_EPS = 1e-6
BLK = 128
PAD = BLK - N_META

RET_HEADS, RET_DK, RET_DV = 4, 128, 128
GDN_HEADS, GDN_DK, GDN_DV, GDN_CONV, GDN_CHUNK = 4, 128, 128, 4, 64
SWA_HEADS, SWA_KV_HEADS, SWA_DH = 8, 2, 64
SB_HEADS, SB_DH = 8, 64
SOLVE_BLK = 16

RET_W = RET_HEADS * RET_DK
AB_GATE_COL = 4096
AB_COLS = 4352
CD_COLS = 2304

VMEM_LIMIT = 56 * 1024 * 1024
NEG_BIG = -1e30
SB_SKIP = -104.0


def _rms(x, g):
    return x * lax.rsqrt(jnp.mean(x * x, axis=-1, keepdims=True) + NORM_EPS) * g


def _silu(x):
    return x * jax.nn.sigmoid(x)


def _dot(a, b):
    return jnp.dot(a, b, preferred_element_type=F32)


def _dot_nt(a, b):
    return lax.dot_general(a, b, (((1,), (1,)), ((), ())), preferred_element_type=F32)


def _dot_tn(a, b):
    return lax.dot_general(a, b, (((0,), (0,)), ((), ())), preferred_element_type=F32)


def _split(a):
    hi = a.astype(BF16)
    return hi, (a - hi.astype(F32)).astype(BF16)


def _dot3(a, b, dot=_dot):
    ah, al = _split(a)
    bh, bl = _split(b)
    return dot(ah, bh) + (dot(ah, bl) + dot(al, bh))


def _params(n_grid, parallel=True):
    sem = ("parallel",) + ("arbitrary",) * (n_grid - 1) if parallel else ("arbitrary",) * n_grid
    return pltpu.CompilerParams(dimension_semantics=sem, vmem_limit_bytes=VMEM_LIMIT)


def _const_spec(shape):
    nd = len(shape)
    return pl.BlockSpec(shape, lambda *_: (0,) * nd, pipeline_mode=pl.Buffered(1))


def _ffn_kernel(h_ref, gpre_ref, gpost_ref, wg_ref, wu_ref, wd_ref, o_ref, xn_ref, acc_ref):
    xn_ref[...] = _rms(h_ref[...], gpre_ref[...]).astype(BF16)
    acc_ref[...] = jnp.zeros_like(acc_ref)

    def body(c, carry):
        xn = xn_ref[...]
        g = _dot(xn, wg_ref[c])
        u = _dot(xn, wu_ref[c])
        a = (_silu(g) * u).astype(BF16)
        acc_ref[...] += _dot(a, wd_ref[c])
        return carry

    lax.fori_loop(0, wg_ref.shape[0], body, 0)
    o_ref[...] = h_ref[...] + 0.5 * _rms(acc_ref[...], gpost_ref[...])


def _ffn(h, g_pre, g_post, w_gate, w_up, w_down, *, tm=512, tf=256):
    T, D = h.shape
    F = w_gate.shape[1]
    nc = F // tf
    wg = w_gate.astype(BF16).reshape(D, nc, tf).transpose(1, 0, 2)
    wu = w_up.astype(BF16).reshape(D, nc, tf).transpose(1, 0, 2)
    wd = w_down.astype(BF16).reshape(nc, tf, D)
    row = pl.BlockSpec((tm, D), lambda i: (i, 0))
    return pl.pallas_call(
        _ffn_kernel,
        grid=(T // tm,),
        in_specs=[row, _const_spec((1, D)), _const_spec((1, D)),
                  _const_spec((nc, D, tf)), _const_spec((nc, D, tf)), _const_spec((nc, tf, D))],
        out_specs=row,
        out_shape=jax.ShapeDtypeStruct((T, D), F32),
        scratch_shapes=[pltpu.VMEM((tm, D), BF16), pltpu.VMEM((tm, D), F32)],
        compiler_params=_params(1),
        name="ffn",
    )(h, g_pre.reshape(1, D), g_post.reshape(1, D), wg, wu, wd)


def _proj_kernel(h_ref, g_ref, w_ref, o_ref, *, tn):
    xn = _rms(h_ref[...], g_ref[...]).astype(BF16)
    for c in range(0, w_ref.shape[1], tn):
        o_ref[:, c:c + tn] = _dot(xn, w_ref[:, c:c + tn]).astype(o_ref.dtype)


def _proj(h, g, w, out_dtype, *, tm=512, tn=256):
    T, D = h.shape
    N = w.shape[1]
    return pl.pallas_call(
        functools.partial(_proj_kernel, tn=tn),
        grid=(T // tm,),
        in_specs=[pl.BlockSpec((tm, D), lambda i: (i, 0)), _const_spec((1, D)), _const_spec((D, N))],
        out_specs=pl.BlockSpec((tm, N), lambda i: (i, 0)),
        out_shape=jax.ShapeDtypeStruct((T, N), out_dtype),
        compiler_params=_params(1),
        name="in_proj",
    )(h, g.reshape(1, D), w.astype(BF16))


def _outproj_kernel(a_ref, b_ref, h_ref, g_ref, wa_ref, wb_ref, o_ref):
    y = _dot(a_ref[...].astype(BF16), wa_ref[...]) + _dot(b_ref[...].astype(BF16), wb_ref[...])
    o_ref[...] = h_ref[...] + _rms(y, g_ref[...])


def _outproj(a, b, h, g, w_out, *, tm=512):
    T, D = h.shape
    Ka, Kb = a.shape[1], b.shape[1]
    w = w_out.astype(BF16)
    row = lambda n: pl.BlockSpec((tm, n), lambda i: (i, 0))
    return pl.pallas_call(
        _outproj_kernel,
        grid=(T // tm,),
        in_specs=[row(Ka), row(Kb), row(D), _const_spec((1, D)), _const_spec((Ka, D)), _const_spec((Kb, D))],
        out_specs=row(D),
        out_shape=jax.ShapeDtypeStruct((T, D), F32),
        compiler_params=_params(1),
        name="out_proj",
    )(a, b, h, g.reshape(1, D), w[:Ka], w[Ka:])


def _ret_kernel(q_ref, k_ref, v_ref, gate_ref, cos_ref, sin_ref, o_ref, s_ref, dec_ref, zx_ref):
    C = BLK
    n = pl.program_id(1)

    @pl.when(n == 0)
    def _init():
        s_ref[...] = jnp.zeros_like(s_ref)
        diff = (lax.broadcasted_iota(jnp.int32, (C, C), 0) - lax.broadcasted_iota(jnp.int32, (C, C), 1)).astype(F32)
        idx = lax.broadcasted_iota(jnp.int32, (C, RET_DK), 0).astype(F32)
        for hd in range(RET_HEADS):
            lg = math.log1p(-2.0 ** (-5.0 - hd))
            dec_ref[hd] = jnp.where(diff >= 0, jnp.exp(jnp.maximum(diff, 0.0) * lg), 0.0)
            zx_ref[0, hd] = jnp.exp((C - 1.0 - idx) * lg)
            zx_ref[1, hd] = jnp.exp((idx + 1.0) * lg)

    cos = cos_ref[...]
    sin = sin_ref[...]
    for hd in range(RET_HEADS):
        lg = math.log1p(-2.0 ** (-5.0 - hd))
        sl = slice(hd * RET_DK, (hd + 1) * RET_DK)
        q = q_ref[:, sl]
        k = k_ref[:, sl]
        v = v_ref[:, sl].astype(BF16)
        qr = q * cos + pltpu.roll(q, RET_DK // 2, 1) * sin
        kr = (k * cos + pltpu.roll(k, RET_DK // 2, 1) * sin) * RET_DK ** -0.5
        scores = _dot_nt(qr.astype(BF16), kr.astype(BF16)) * dec_ref[hd]
        intra = _dot(scores.astype(BF16), v)
        state = s_ref[hd]
        cross = _dot((qr * zx_ref[1, hd]).astype(BF16), state.astype(BF16))
        kv = _dot_tn((kr * zx_ref[0, hd]).astype(BF16), v)
        s_ref[hd] = state * math.exp(C * lg) + kv
        o = intra + cross
        mu = jnp.mean(o, axis=-1, keepdims=True)
        var = jnp.mean(jnp.square(o - mu), axis=-1, keepdims=True)
        o_ref[:, sl] = (o - mu) * lax.rsqrt(var + NORM_EPS) * _silu(gate_ref[:, sl])


def _retention(proj, cos, sin, B, Lp):
    T = proj.shape[0]
    nb = Lp // BLK
    col = lambda j: pl.BlockSpec((BLK, RET_W), lambda b, n: (b * nb + n, j))
    tab = pl.BlockSpec((BLK, RET_DK), lambda b, n: (n, 0))
    return pl.pallas_call(
        _ret_kernel,
        grid=(B, nb),
        in_specs=[col(0), col(1), col(2), col(3), tab, tab],
        out_specs=pl.BlockSpec((BLK, RET_W), lambda b, n: (b * nb + n, 0)),
        out_shape=jax.ShapeDtypeStruct((T, RET_W), F32),
        scratch_shapes=[pltpu.VMEM((RET_HEADS, RET_DK, RET_DV), F32),
                        pltpu.VMEM((RET_HEADS, BLK, BLK), F32),
                        pltpu.VMEM((2, RET_HEADS, BLK, RET_DK), F32)],
        compiler_params=_params(2),
        name="retention",
    )(proj, proj, proj, proj, cos, sin)


def _unit_lower_inverse(a_mat, eye, same_blk):
    d_mat = jnp.where(same_blk, a_mat, 0.0)
    e_mat = a_mat - d_mat
    x = -d_mat
    t_inv = eye + x
    for _ in range(3):
        x = _dot3(x, x)
        t_inv = t_inv + _dot3(t_inv, x)
    f = _dot3(t_inv, e_mat)
    f2 = _dot3(f, f)
    m = eye - f + f2 - _dot3(f, f2)
    return m, t_inv


def _gdn_kernel(q_ref, k_ref, v_ref, z_ref, gate_ref, cw_ref, alog_ref, dtb_ref, onorm_ref,
                o_ref, s_ref, xbuf_ref):
    C = GDN_CHUNK
    n = pl.program_id(1)
    HALO = 8

    @pl.when(n == 0)
    def _init():
        s_ref[...] = jnp.zeros_like(s_ref)
        xbuf_ref[:, 0:HALO, :] = jnp.zeros((3, HALO, RET_W), F32)

    qkv = []
    for j, ref in enumerate((q_ref, k_ref, v_ref)):
        xbuf_ref[j, HALO:HALO + C, :] = ref[...]
        y = jnp.zeros((C, RET_W), F32)
        for tap in range(GDN_CONV):
            off = HALO - (GDN_CONV - 1) + tap
            y = y + cw_ref[j, tap:tap + 1, :] * xbuf_ref[j, off:off + C, :]
        xbuf_ref[j, 0:HALO, :] = xbuf_ref[j, C:C + HALO, :]
        qkv.append(_silu(y))
    qc, kc, vc = qkv

    row = lax.broadcasted_iota(jnp.int32, (C, C), 0)
    col = lax.broadcasted_iota(jnp.int32, (C, C), 1)
    incl = row >= col
    strict = row > col
    same_blk = (row // SOLVE_BLK) == (col // SOLVE_BLK)
    eye = jnp.where(row == col, 1.0, 0.0).astype(F32)
    tri_incl = jnp.where(incl, 1.0, 0.0).astype(F32)
    tri_upper = jnp.where(row <= col, 1.0, 0.0).astype(F32)

    gates = gate_ref[...]
    beta_all = jax.nn.sigmoid(gates)
    g_all = -jnp.exp(alog_ref[...]) * jax.nn.softplus(gates + dtb_ref[...])
    gcum_all = _dot3(tri_incl, g_all)

    for hd in range(GDN_HEADS):
        sl = slice(hd * GDN_DK, (hd + 1) * GDN_DK)
        q = qc[:, sl]
        k = kc[:, sl]
        v = vc[:, sl]
        q = q * lax.rsqrt(jnp.sum(q * q, axis=-1, keepdims=True) + NORM_EPS) * GDN_DK ** -0.5
        k = k * lax.rsqrt(jnp.sum(k * k, axis=-1, keepdims=True) + NORM_EPS)
        beta = beta_all[:, hd:hd + 1]
        gcum = gcum_all[:, GDN_HEADS + hd:GDN_HEADS + hd + 1]
        g_col = g_all[:, GDN_HEADS + hd:GDN_HEADS + hd + 1]
        gcum_row = _dot3(jnp.broadcast_to(g_col, (C, C)), tri_upper, _dot_tn)
        gdiff = gcum - gcum_row
        decay = jnp.where(incl, jnp.exp(jnp.where(incl, gdiff, 0.0)), 0.0)
        k_beta = k * beta
        a_mat = jnp.where(strict, _dot_nt(k_beta.astype(BF16), k.astype(BF16)) * decay, 0.0)
        m, t_inv = _unit_lower_inverse(a_mat, eye, same_blk)
        e_gcum = jnp.exp(gcum)
        rhs = jnp.concatenate([v * beta, k_beta * e_gcum], axis=-1)
        sol = _dot3(m, _dot3(t_inv, rhs))
        u_vec = sol[:, :GDN_DV]
        w_vec = sol[:, GDN_DV:]
        qk = jnp.where(incl, _dot_nt(q.astype(BF16), k.astype(BF16)) * decay, 0.0)
        g_last = gcum[C - 1:C, :]
        k_tail = k * jnp.exp(g_last - gcum)
        state = s_ref[hd]
        state_b = state.astype(BF16)
        v_new = u_vec - _dot(w_vec.astype(BF16), state_b)
        o = _dot((q * e_gcum).astype(BF16), state_b) + _dot(qk.astype(BF16), v_new.astype(BF16))
        s_ref[hd] = state * jnp.exp(g_last) + _dot_tn(k_tail.astype(BF16), v_new.astype(BF16))
        o_ref[:, sl] = _rms(o, onorm_ref[...]) * _silu(z_ref[:, sl])


def _gdn(proj, conv_w, a_log, dt_bias, out_norm, B, Lp):
    T = proj.shape[0]
    C = GDN_CHUNK
    nc = Lp // C
    col = lambda j: pl.BlockSpec((C, RET_W), lambda b, n: (b * nc + n, j))
    gate_lanes = jnp.zeros((1, BLK), F32)
    alog = gate_lanes.at[0, GDN_HEADS:2 * GDN_HEADS].set(a_log.astype(F32))
    dtb = gate_lanes.at[0, GDN_HEADS:2 * GDN_HEADS].set(dt_bias.astype(F32))
    cw = conv_w.astype(F32).reshape(GDN_CONV, 3, RET_W).transpose(1, 0, 2)
    return pl.pallas_call(
        _gdn_kernel,
        grid=(B, nc),
        in_specs=[col(4), col(5), col(6), col(7),
                  pl.BlockSpec((C, BLK), lambda b, n: (b * nc + n, AB_GATE_COL // BLK)),
                  _const_spec((3, GDN_CONV, RET_W)), _const_spec((1, BLK)), _const_spec((1, BLK)),
                  _const_spec((1, GDN_DV))],
        out_specs=pl.BlockSpec((C, RET_W), lambda b, n: (b * nc + n, 0)),
        out_shape=jax.ShapeDtypeStruct((T, RET_W), F32),
        scratch_shapes=[pltpu.VMEM((GDN_HEADS, GDN_DK, GDN_DV), F32),
                        pltpu.VMEM((3, C + 8, RET_W), F32)],
        compiler_params=_params(2),
        name="gated_deltanet",
    )(proj, proj, proj, proj, proj, cw, alog, dtb, out_norm.astype(F32).reshape(1, GDN_DV))


def _swa_kernel(q_ref, kc_ref, vc_ref, kp_ref, vp_ref, km_ref, vm_ref, sink_ref, o_ref):
    n = pl.program_id(1)
    G = SWA_HEADS // SWA_KV_HEADS
    row = lax.broadcasted_iota(jnp.int32, (BLK, BLK), 0)
    col = lax.broadcasted_iota(jnp.int32, (BLK, BLK), 1)
    cur_ok = (col <= row) & (n >= 1)
    prev_ok = (col > row) & (n >= 2)
    mrow = lax.broadcasted_iota(jnp.int32, (BLK, N_META), 0)
    mcol = lax.broadcasted_iota(jnp.int32, (BLK, N_META), 1)
    meta_ok = (n >= 1) | (PAD + mcol <= mrow)
    for kvh in range(SWA_KV_HEADS):
        ks = slice(kvh * SWA_DH, (kvh + 1) * SWA_DH)
        k_cur, v_cur = kc_ref[:, ks], vc_ref[:, ks]
        k_prev, v_prev = kp_ref[:, ks], vp_ref[:, ks]
        k_meta, v_meta = km_ref[PAD:BLK, ks], vm_ref[PAD:BLK, ks]
        for g in range(G):
            hq = kvh * G + g
            qs = slice(hq * SWA_DH, (hq + 1) * SWA_DH)
            q = q_ref[:, qs] * SWA_DH ** -0.5
            s_cur = jnp.where(cur_ok, _dot_nt(q, k_cur), NEG_BIG)
            s_prev = jnp.where(prev_ok, _dot_nt(q, k_prev), NEG_BIG)
            s_meta = jnp.where(meta_ok, _dot_nt(q, k_meta), NEG_BIG)
            sink = sink_ref[hq]
            m = jnp.maximum(jnp.maximum(jnp.max(s_cur, axis=-1, keepdims=True),
                                        jnp.max(s_prev, axis=-1, keepdims=True)),
                            jnp.maximum(jnp.max(s_meta, axis=-1, keepdims=True), sink))
            p_cur = jnp.exp(s_cur - m)
            p_prev = jnp.exp(s_prev - m)
            p_meta = jnp.exp(s_meta - m)
            denom = (jnp.sum(p_cur, axis=-1, keepdims=True) + jnp.sum(p_prev, axis=-1, keepdims=True)
                     + jnp.sum(p_meta, axis=-1, keepdims=True) + jnp.exp(sink - m))
            o = (_dot(p_cur.astype(BF16), v_cur) + _dot(p_prev.astype(BF16), v_prev)
                 + _dot(p_meta.astype(BF16), v_meta))
            o_ref[:, qs] = o / denom


def _swa(proj, sinks, B, Lp):
    T = proj.shape[0]
    nb = Lp // BLK
    k_col, v_col = RET_W // BLK, RET_W // BLK + 1
    cur = lambda j: pl.BlockSpec((BLK, BLK), lambda b, n: (b * nb + n, j))
    prev = lambda j: pl.BlockSpec((BLK, BLK), lambda b, n: (b * nb + jnp.maximum(n - 1, 0), j))
    first = lambda j: pl.BlockSpec((BLK, BLK), lambda b, n: (b * nb, j))
    return pl.pallas_call(
        _swa_kernel,
        grid=(B, nb),
        in_specs=[pl.BlockSpec((BLK, RET_W), lambda b, n: (b * nb + n, 0)),
                  cur(k_col), cur(v_col), prev(k_col), prev(v_col), first(k_col), first(v_col),
                  pl.BlockSpec(memory_space=pltpu.SMEM)],
        out_specs=pl.BlockSpec((BLK, RET_W), lambda b, n: (b * nb + n, 0)),
        out_shape=jax.ShapeDtypeStruct((T, RET_W), F32),
        compiler_params=_params(2),
        name="swa_sink",
    )(proj, proj, proj, proj, proj, proj, proj, sinks.astype(F32))


def _sb_kernel(q_ref, k_ref, v_ref, o_ref, acc_ref, run_ref):
    n = pl.program_id(2)
    row = lax.broadcasted_iota(jnp.int32, (BLK, BLK), 0)
    col = lax.broadcasted_iota(jnp.int32, (BLK, BLK), 1)
    suffix = jnp.where(row > col, 1.0, 0.0).astype(BF16)
    ones = jnp.ones((BLK, BLK), BF16)
    sum_rhs = jnp.concatenate([jnp.concatenate([suffix, ones], axis=1)] * 2, axis=0)
    qpos = n * BLK + row
    heads_per_blk = BLK // SB_DH
    outs = []
    for hh in range(heads_per_blk):
        in_head = (col >= hh * SB_DH) & (col < (hh + 1) * SB_DH)
        q = jnp.where(in_head, q_ref[...], 0).astype(BF16)
        acc_ref[...] = jnp.zeros_like(acc_ref)
        run_ref[...] = jnp.zeros_like(run_ref)

        def cond(carry):
            j, go = carry
            return (j >= 0) & (go > 0)

        def body(carry):
            j, _ = carry
            start = pl.multiple_of(j * BLK, BLK)
            k = k_ref[pl.ds(start, BLK), :]
            v = v_ref[pl.ds(start, BLK), :]
            z = _dot_nt(q, k) * SB_DH ** -0.5
            kpos = j * BLK + col
            valid = (kpos < qpos) & (kpos >= PAD)
            softplus_neg = jnp.log1p(jnp.exp(-jnp.abs(z)))
            log_beta = jnp.minimum(z, 0.0) - softplus_neg
            log_1m = jnp.where(valid, -jnp.maximum(z, 0.0) - softplus_neg, 0.0)
            hi, lo = _split(log_1m)
            sums = _dot(jnp.concatenate([hi, lo], axis=1), sum_rhs)
            run = run_ref[...]
            a = jnp.where(valid, jnp.exp(log_beta + sums[:, :BLK] + run), 0.0)
            acc_ref[...] += _dot(a.astype(BF16), v)
            run = run + sums[:, BLK:]
            run_ref[...] = run
            return j - 1, (jnp.max(run) >= SB_SKIP).astype(jnp.int32)

        lax.while_loop(cond, body, (n, jnp.int32(1)))
        outs.append((in_head, acc_ref[...]))
    o = jnp.zeros((BLK, BLK), F32)
    for in_head, acc in outs:
        o = jnp.where(in_head, acc, o)
    o_ref[...] = o


def _stick_breaking(proj, B, Lp):
    T = proj.shape[0]
    nb = Lp // BLK
    pairs = SB_HEADS * SB_DH // BLK
    q0 = (SWA_HEADS + 2 * SWA_KV_HEADS) * SWA_DH // BLK
    k0, v0 = q0 + pairs, q0 + 2 * pairs
    proj3 = proj.reshape(B, Lp, proj.shape[1])
    seq = lambda c0: pl.BlockSpec((None, Lp, BLK), lambda b, p, n: (b, 0, c0 + p))
    return pl.pallas_call(
        _sb_kernel,
        grid=(B, pairs, nb),
        in_specs=[pl.BlockSpec((BLK, BLK), lambda b, p, n: (b * nb + n, q0 + p)), seq(k0), seq(v0)],
        out_specs=pl.BlockSpec((BLK, BLK), lambda b, p, n: (b * nb + n, p)),
        out_shape=jax.ShapeDtypeStruct((T, RET_W), F32),
        scratch_shapes=[pltpu.VMEM((BLK, BLK), F32), pltpu.VMEM((BLK, BLK), F32)],
        compiler_params=_params(3),
        name="stick_breaking",
    )(proj, proj3, proj3)


def _rotation_tables(Lp):
    half = RET_DK // 2
    inv_freq = 1.0 / (10000.0 ** jnp.linspace(0.0, 1.0, half, dtype=F32))
    pos = jnp.arange(Lp, dtype=F32) - float(PAD)
    ang = pos[:, None] * inv_freq[None, :]
    cos, sin = jnp.cos(ang), jnp.sin(ang)
    return jnp.concatenate([cos, cos], axis=1), jnp.concatenate([-sin, sin], axis=1)


def _ab_weight(w_in):
    D = w_in.shape[0]
    perm = np.concatenate([np.arange(0, RET_DK, 2), np.arange(1, RET_DK, 2)])
    qk_perm = np.concatenate([h * RET_DK + perm for h in range(RET_HEADS)])
    cols = np.concatenate([qk_perm, RET_W + qk_perm, np.arange(2 * RET_W, w_in.shape[1])])
    w = w_in[:, cols]
    gates = w[:, -2 * GDN_HEADS:]
    body = w[:, :-2 * GDN_HEADS]
    return jnp.concatenate([body, gates, jnp.zeros((D, AB_COLS - AB_GATE_COL - 2 * GDN_HEADS), w.dtype)], axis=1)


def kernel(x, meta_tokens, norm_gains, ffn_w_gate, ffn_w_up, ffn_w_down, ab_w_in, ab_conv_w, ab_a_log, ab_dt_bias, ab_out_norm, ab_w_out, cd_w_in, cd_sinks, cd_w_out):
    B, S, D = x.shape
    Lp = S + BLK
    meta = jnp.broadcast_to(meta_tokens[None].astype(x.dtype), (B, N_META, D))
    h = jnp.concatenate([jnp.zeros((B, PAD, D), x.dtype), meta, x], axis=1).reshape(B * Lp, D)
    cos, sin = _rotation_tables(Lp)
    for i in range(norm_gains.shape[0]):
        g = norm_gains[i]
        j = i // 2
        h = _ffn(h, g[0], g[1], ffn_w_gate[i, 0], ffn_w_up[i, 0], ffn_w_down[i, 0])
        if i % 2 == 0:
            proj = _proj(h, g[2], _ab_weight(ab_w_in[j]), F32)
            mix_a = _retention(proj, cos, sin, B, Lp)
            mix_b = _gdn(proj, ab_conv_w[j], ab_a_log[j], ab_dt_bias[j], ab_out_norm[j], B, Lp)
            h = _outproj(mix_a, mix_b, h, g[3], ab_w_out[j])
        else:
            proj = _proj(h, g[2], cd_w_in[j], BF16)
            mix_a = _swa(proj, cd_sinks[j], B, Lp)
            mix_b = _stick_breaking(proj, B, Lp)
            h = _outproj(mix_a, mix_b, h, g[3], cd_w_out[j])
        h = _ffn(h, g[4], g[5], ffn_w_gate[i, 1], ffn_w_up[i, 1], ffn_w_down[i, 1])
    return h.reshape(B, Lp, D)[:, BLK:]
```

```python
import functools
import math

import numpy as np
import jax
import jax.numpy as jnp
from jax import lax
from jax.experimental import pallas as pl
from jax.experimental.pallas import tpu as pltpu

F32 = jnp.float32
BF16 = jnp.bfloat16

N_META = 16
NORM_EPS = 1e-6
BLK = 128
PAD = BLK - N_META

RET_HEADS, RET_DK, RET_DV = 4, 128, 128
GDN_HEADS, GDN_DK, GDN_DV, GDN_CONV = 4, 128, 128, 4
SWA_HEADS, SWA_KV_HEADS, SWA_DH = 8, 2, 64
SB_HEADS, SB_DH = 8, 64
SOLVE_BLK = 16

RET_W = RET_HEADS * RET_DK
AB_GATE_COL = 4096
AB_COLS = 4352
CONV_HALO = 8

VMEM_LIMIT = 56 * 1024 * 1024
NEG_BIG = -1e30
SB_SKIP = -104.0


def _rms(x, g):
    return x * lax.rsqrt(jnp.mean(x * x, axis=-1, keepdims=True) + NORM_EPS) * g


def _silu(x):
    return x * jax.nn.sigmoid(x)


def _dot(a, b):
    return jnp.dot(a, b, preferred_element_type=F32)


def _dot_nt(a, b):
    return lax.dot_general(a, b, (((1,), (1,)), ((), ())), preferred_element_type=F32)


def _dot_tn(a, b):
    return lax.dot_general(a, b, (((0,), (0,)), ((), ())), preferred_element_type=F32)


def _split(a):
    hi = a.astype(BF16)
    return hi, (a - hi.astype(F32)).astype(BF16)


def _dot3(a, b, dot=_dot):
    ah, al = _split(a)
    bh, bl = _split(b)
    return dot(ah, bh) + (dot(ah, bl) + dot(al, bh))


def _params(n_grid, parallel=True):
    sem = ("parallel",) + ("arbitrary",) * (n_grid - 1) if parallel else ("arbitrary",) * n_grid
    return pltpu.CompilerParams(dimension_semantics=sem, vmem_limit_bytes=VMEM_LIMIT)


def _const_spec(shape):
    nd = len(shape)
    return pl.BlockSpec(shape, lambda *_: (0,) * nd, pipeline_mode=pl.Buffered(1))


def _row_tile(rows, target):
    tile = target
    while rows % tile:
        tile -= BLK
    return tile


def _blocks_per_step(nb, prefs):
    for s in prefs:
        if nb % s == 0:
            return s
    return 1


def _ffn_kernel(h_ref, gpre_ref, gpost_ref, wg_ref, wu_ref, wd_ref, o_ref, xn_ref, acc_ref):
    xn_ref[...] = _rms(h_ref[...], gpre_ref[...]).astype(BF16)
    acc_ref[...] = jnp.zeros_like(acc_ref)

    def body(c, carry):
        xn = xn_ref[...]
        g = _dot(xn, wg_ref[c])
        u = _dot(xn, wu_ref[c])
        a = (_silu(g) * u).astype(BF16)
        acc_ref[...] += _dot(a, wd_ref[c])
        return carry

    lax.fori_loop(0, wg_ref.shape[0], body, 0)
    o_ref[...] = h_ref[...] + 0.5 * _rms(acc_ref[...], gpost_ref[...])


def _ffn(h, g_pre, g_post, w_gate, w_up, w_down, *, tm=512, tf=256):
    T, D = h.shape
    F = w_gate.shape[1]
    tm = _row_tile(T, tm)
    nc = F // tf
    wg = w_gate.astype(BF16).reshape(D, nc, tf).transpose(1, 0, 2)
    wu = w_up.astype(BF16).reshape(D, nc, tf).transpose(1, 0, 2)
    wd = w_down.astype(BF16).reshape(nc, tf, D)
    row = pl.BlockSpec((tm, D), lambda i: (i, 0))
    return pl.pallas_call(
        _ffn_kernel,
        grid=(T // tm,),
        in_specs=[row, _const_spec((1, D)), _const_spec((1, D)),
                  _const_spec((nc, D, tf)), _const_spec((nc, D, tf)), _const_spec((nc, tf, D))],
        out_specs=row,
        out_shape=jax.ShapeDtypeStruct((T, D), F32),
        scratch_shapes=[pltpu.VMEM((tm, D), BF16), pltpu.VMEM((tm, D), F32)],
        compiler_params=_params(1),
        name="ffn",
    )(h, g_pre.reshape(1, D), g_post.reshape(1, D), wg, wu, wd)


def _proj_kernel(h_ref, g_ref, w_ref, o_ref, *, tn):
    xn = _rms(h_ref[...], g_ref[...]).astype(BF16)
    for c in range(0, w_ref.shape[1], tn):
        o_ref[:, c:c + tn] = _dot(xn, w_ref[:, c:c + tn]).astype(o_ref.dtype)


def _proj(h, g, w, out_dtype, *, tm=512, tn=256):
    T, D = h.shape
    N = w.shape[1]
    tm = _row_tile(T, tm)
    return pl.pallas_call(
        functools.partial(_proj_kernel, tn=tn),
        grid=(T // tm,),
        in_specs=[pl.BlockSpec((tm, D), lambda i: (i, 0)), _const_spec((1, D)), _const_spec((D, N))],
        out_specs=pl.BlockSpec((tm, N), lambda i: (i, 0)),
        out_shape=jax.ShapeDtypeStruct((T, N), out_dtype),
        compiler_params=_params(1),
        name="in_proj",
    )(h, g.reshape(1, D), w.astype(BF16))


def _outproj_kernel(a_ref, b_ref, h_ref, g_ref, wa_ref, wb_ref, o_ref):
    y = _dot(a_ref[...].astype(BF16), wa_ref[...]) + _dot(b_ref[...].astype(BF16), wb_ref[...])
    o_ref[...] = h_ref[...] + _rms(y, g_ref[...])


def _outproj(a, b, h, g, w_out, *, tm=512):
    T, D = h.shape
    Ka, Kb = a.shape[1], b.shape[1]
    tm = _row_tile(T, tm)
    w = w_out.astype(BF16)
    row = lambda n: pl.BlockSpec((tm, n), lambda i: (i, 0))
    return pl.pallas_call(
        _outproj_kernel,
        grid=(T // tm,),
        in_specs=[row(Ka), row(Kb), row(D), _const_spec((1, D)), _const_spec((Ka, D)), _const_spec((Kb, D))],
        out_specs=row(D),
        out_shape=jax.ShapeDtypeStruct((T, D), F32),
        compiler_params=_params(1),
        name="out_proj",
    )(a, b, h, g.reshape(1, D), w[:Ka], w[Ka:])


def _ret_kernel(q_ref, k_ref, v_ref, gate_ref, cos_ref, sin_ref, o_ref, s_ref, dec_ref, zx_ref):
    C = BLK
    n = pl.program_id(1)

    @pl.when(n == 0)
    def _init():
        s_ref[...] = jnp.zeros_like(s_ref)
        diff = (lax.broadcasted_iota(jnp.int32, (C, C), 0) - lax.broadcasted_iota(jnp.int32, (C, C), 1)).astype(F32)
        idx = lax.broadcasted_iota(jnp.int32, (C, RET_DK), 0).astype(F32)
        for hd in range(RET_HEADS):
            lg = math.log1p(-2.0 ** (-5.0 - hd))
            dec_ref[hd] = jnp.where(diff >= 0, jnp.exp(jnp.maximum(diff, 0.0) * lg), 0.0)
            zx_ref[0, hd] = jnp.exp((C - 1.0 - idx) * lg)
            zx_ref[1, hd] = jnp.exp((idx + 1.0) * lg)

    cos = cos_ref[...]
    sin = sin_ref[...]
    for hd in range(RET_HEADS):
        lg = math.log1p(-2.0 ** (-5.0 - hd))
        sl = slice(hd * RET_DK, (hd + 1) * RET_DK)
        q = q_ref[:, sl]
        k = k_ref[:, sl]
        v = v_ref[:, sl].astype(BF16)
        qr = q * cos + pltpu.roll(q, RET_DK // 2, 1) * sin
        kr = (k * cos + pltpu.roll(k, RET_DK // 2, 1) * sin) * RET_DK ** -0.5
        scores = _dot_nt(qr.astype(BF16), kr.astype(BF16)) * dec_ref[hd]
        intra = _dot(scores.astype(BF16), v)
        state = s_ref[hd]
        cross = _dot((qr * zx_ref[1, hd]).astype(BF16), state.astype(BF16))
        kv = _dot_tn((kr * zx_ref[0, hd]).astype(BF16), v)
        s_ref[hd] = state * math.exp(C * lg) + kv
        o = intra + cross
        mu = jnp.mean(o, axis=-1, keepdims=True)
        var = jnp.mean(jnp.square(o - mu), axis=-1, keepdims=True)
        o_ref[:, sl] = (o - mu) * lax.rsqrt(var + NORM_EPS) * _silu(gate_ref[:, sl])


def _retention(proj, cos, sin, B, Lp):
    T = proj.shape[0]
    nb = Lp // BLK
    col = lambda j: pl.BlockSpec((BLK, RET_W), lambda b, n: (b * nb + n, j))
    tab = pl.BlockSpec((BLK, RET_DK), lambda b, n: (n, 0))
    return pl.pallas_call(
        _ret_kernel,
        grid=(B, nb),
        in_specs=[col(0), col(1), col(2), col(3), tab, tab],
        out_specs=pl.BlockSpec((BLK, RET_W), lambda b, n: (b * nb + n, 0)),
        out_shape=jax.ShapeDtypeStruct((T, RET_W), F32),
        scratch_shapes=[pltpu.VMEM((RET_HEADS, RET_DK, RET_DV), F32),
                        pltpu.VMEM((RET_HEADS, BLK, BLK), F32),
                        pltpu.VMEM((2, RET_HEADS, BLK, RET_DK), F32)],
        compiler_params=_params(2),
        name="retention",
    )(proj, proj, proj, proj, cos, sin)


def _unit_lower_inverses(a_mats, eye, same_blk):
    d = [jnp.where(same_blk, a, 0.0) for a in a_mats]
    e = [(a - dd).astype(BF16) for a, dd in zip(a_mats, d)]
    x = [-dd for dd in d]
    t = [eye + xx for xx in x]
    for _ in range(3):
        xb = [xx.astype(BF16) for xx in x]
        x = [_dot(xx, xx) for xx in xb]
        t = [tt + _dot(tt.astype(BF16), xx.astype(BF16)) for tt, xx in zip(t, x)]
    tb = [tt.astype(BF16) for tt in t]
    f = [_dot(tt, ee) for tt, ee in zip(tb, e)]
    p = [eye - ff for ff in f]
    for _ in range(2):
        fb = [ff.astype(BF16) for ff in f]
        f = [_dot(ff, ff) for ff in fb]
        p = [pp + _dot(pp.astype(BF16), ff.astype(BF16)) for pp, ff in zip(p, f)]
    return [pp.astype(BF16) for pp in p], tb


def _gdn_kernel(q_ref, k_ref, v_ref, z_ref, gate_ref, cw_ref, alog_ref, dtb_ref, onorm_ref,
                o_ref, s_ref, xbuf_ref):
    nbatch, C = q_ref.shape[0], q_ref.shape[1]
    n = pl.program_id(0)

    @pl.when(n == 0)
    def _init():
        s_ref[...] = jnp.zeros_like(s_ref)
        xbuf_ref[:, :, 0:CONV_HALO, :] = jnp.zeros((3, nbatch, CONV_HALO, RET_W), F32)

    qkv = []
    for j, ref in enumerate((q_ref, k_ref, v_ref)):
        xbuf_ref[j, :, CONV_HALO:CONV_HALO + C, :] = ref[...]
        per_batch = []
        for b in range(nbatch):
            x_ext = xbuf_ref[j, b]
            y = cw_ref[j, GDN_CONV - 1:GDN_CONV, :] * x_ext[CONV_HALO:]
            for shift in range(1, GDN_CONV):
                tap = GDN_CONV - 1 - shift
                y = y + cw_ref[j, tap:tap + 1, :] * pltpu.roll(x_ext, shift, 0)[CONV_HALO:]
            per_batch.append(_silu(y))
        xbuf_ref[j, :, 0:CONV_HALO, :] = xbuf_ref[j, :, C:C + CONV_HALO, :]
        qkv.append(per_batch)
    qc, kc, vc = qkv

    row = lax.broadcasted_iota(jnp.int32, (C, C), 0)
    col = lax.broadcasted_iota(jnp.int32, (C, C), 1)
    incl = row >= col
    strict = row > col
    same_blk = (row // SOLVE_BLK) == (col // SOLVE_BLK)
    eye = jnp.where(row == col, 1.0, 0.0).astype(F32)
    tri_incl = jnp.where(incl, 1.0, 0.0).astype(F32)
    tri_upper = jnp.where(row <= col, 1.0, 0.0).astype(F32)

    gates = [gate_ref[b] for b in range(nbatch)]
    beta_all = [jax.nn.sigmoid(gt) for gt in gates]
    g_all = [-jnp.exp(alog_ref[...]) * jax.nn.softplus(gt + dtb_ref[...]) for gt in gates]
    gcum_all = [_dot3(tri_incl, g) for g in g_all]
    gcum_t = [_dot3(g, tri_upper, _dot_tn) for g in g_all]

    chains = [(b, hd) for b in range(nbatch) for hd in range(GDN_HEADS)]
    head = lambda t, b, hd: t[b][:, hd * GDN_DK:(hd + 1) * GDN_DK]
    l2n = lambda t: t * lax.rsqrt(jnp.sum(t * t, axis=-1, keepdims=True) + NORM_EPS)
    q = [l2n(head(qc, b, hd)) * GDN_DK ** -0.5 for b, hd in chains]
    k = [l2n(head(kc, b, hd)) for b, hd in chains]
    v = [head(vc, b, hd) for b, hd in chains]
    beta = [beta_all[b][:, hd:hd + 1] for b, hd in chains]
    gcum = [gcum_all[b][:, GDN_HEADS + hd:GDN_HEADS + hd + 1] for b, hd in chains]
    gcum_row = [gcum_t[b][GDN_HEADS + hd:GDN_HEADS + hd + 1, :] for b, hd in chains]
    decay = [jnp.where(incl, jnp.exp(jnp.where(incl, gc - gr, 0.0)), 0.0) for gc, gr in zip(gcum, gcum_row)]
    k_beta = [kk * bb for kk, bb in zip(k, beta)]
    kb16 = [kk.astype(BF16) for kk in k]
    a_mat = [jnp.where(strict, _dot_nt(kbt.astype(BF16), k16) * dc, 0.0) for kbt, k16, dc in zip(k_beta, kb16, decay)]
    p_mat, t_inv = _unit_lower_inverses(a_mat, eye, same_blk)
    e_gcum = [jnp.exp(gc) for gc in gcum]
    rhs = [jnp.concatenate([vv * bb, kbt * eg], axis=-1) for vv, bb, kbt, eg in zip(v, beta, k_beta, e_gcum)]
    sol = [_dot(tt, rr.astype(BF16)) for tt, rr in zip(t_inv, rhs)]
    sol = [_dot(pp, ss.astype(BF16)) for pp, ss in zip(p_mat, sol)]
    qk = [jnp.where(incl, _dot_nt(qq.astype(BF16), k16) * dc, 0.0) for qq, k16, dc in zip(q, kb16, decay)]
    g_last = [gc[C - 1:C, :] for gc in gcum]
    k_tail = [(kk * jnp.exp(gl - gc)).astype(BF16) for kk, gl, gc in zip(k, g_last, gcum)]
    q_dec = [(qq * eg).astype(BF16) for qq, eg in zip(q, e_gcum)]

    state = [s_ref[i] for i in range(len(chains))]
    state_b = [st.astype(BF16) for st in state]
    v_new = [ss[:, :GDN_DV] - _dot(ss[:, GDN_DV:].astype(BF16), sb) for ss, sb in zip(sol, state_b)]
    v_new_b = [vn.astype(BF16) for vn in v_new]
    out = [_dot(qd, sb) + _dot(qkm.astype(BF16), vn) for qd, sb, qkm, vn in zip(q_dec, state_b, qk, v_new_b)]
    for i, (b, hd) in enumerate(chains):
        s_ref[i] = state[i] * jnp.exp(g_last[i]) + _dot_tn(k_tail[i], v_new_b[i])
        sl = slice(hd * GDN_DV, (hd + 1) * GDN_DV)
        o_ref[b, :, sl] = _rms(out[i], onorm_ref[...]) * _silu(z_ref[b, :, sl])


def _gdn(proj, conv_w, a_log, dt_bias, out_norm, B, Lp):
    C = BLK
    nc = Lp // C
    proj3 = proj.reshape(B, Lp, proj.shape[1])
    col = lambda j: pl.BlockSpec((B, C, RET_W), lambda n: (0, n, j))
    gate_lanes = jnp.zeros((1, BLK), F32)
    alog = gate_lanes.at[0, GDN_HEADS:2 * GDN_HEADS].set(a_log.astype(F32))
    dtb = gate_lanes.at[0, GDN_HEADS:2 * GDN_HEADS].set(dt_bias.astype(F32))
    cw = conv_w.astype(F32).reshape(GDN_CONV, 3, RET_W).transpose(1, 0, 2)
    out = pl.pallas_call(
        _gdn_kernel,
        grid=(nc,),
        in_specs=[col(4), col(5), col(6), col(7),
                  pl.BlockSpec((B, C, BLK), lambda n: (0, n, AB_GATE_COL // BLK)),
                  _const_spec((3, GDN_CONV, RET_W)), _const_spec((1, BLK)), _const_spec((1, BLK)),
                  _const_spec((1, GDN_DV))],
        out_specs=pl.BlockSpec((B, C, RET_W), lambda n: (0, n, 0)),
        out_shape=jax.ShapeDtypeStruct((B, Lp, RET_W), F32),
        scratch_shapes=[pltpu.VMEM((B * GDN_HEADS, GDN_DK, GDN_DV), F32),
                        pltpu.VMEM((3, B, C + CONV_HALO, RET_W), F32)],
        compiler_params=_params(1, parallel=False),
        name="gated_deltanet",
    )(proj3, proj3, proj3, proj3, proj3, cw, alog, dtb, out_norm.astype(F32).reshape(1, GDN_DV))
    return out.reshape(B * Lp, RET_W)


def _swa_head_perm():
    G = SWA_HEADS // SWA_KV_HEADS
    heads = [kv * G + b for b in range(G) for kv in range(SWA_KV_HEADS)]
    return np.concatenate([np.arange(h * SWA_DH, (h + 1) * SWA_DH) for h in heads])


def _swa_kernel(q_ref, kc_ref, vc_ref, kp_ref, vp_ref, km_ref, vm_ref, sink_ref, o_ref, *, S):
    n = pl.program_id(1)
    G = SWA_HEADS // SWA_KV_HEADS
    R = G * BLK
    r = lax.broadcasted_iota(jnp.int32, (R, BLK), 0) & (BLK - 1)
    col = lax.broadcasted_iota(jnp.int32, (R, BLK), 1)
    lower = col <= r
    upper = col > r
    is_meta = col >= PAD
    halves = [col < SWA_DH, col >= SWA_DH]
    k_meta, v_meta = km_ref[...], vm_ref[...]
    for s in range(S):
        blk = n * S + s
        rows = slice(s * BLK, (s + 1) * BLK)
        k_cur, v_cur = kc_ref[rows, :], vc_ref[rows, :]
        if s == 0:
            k_prev, v_prev = kp_ref[...], vp_ref[...]
        else:
            k_prev, v_prev = kc_ref[(s - 1) * BLK:s * BLK, :], vc_ref[(s - 1) * BLK:s * BLK, :]
        cur_ok = lower & (blk >= 1)
        prev_ok = upper & (blk >= 2)
        meta_ok = is_meta & ((blk >= 1) | lower)
        q_st = jnp.concatenate([q_ref[rows, b * BLK:(b + 1) * BLK] for b in range(G)], axis=0) * SWA_DH ** -0.5
        outs = []
        for kv in range(SWA_KV_HEADS):
            q = jnp.where(halves[kv], q_st, 0)
            s_cur = jnp.where(cur_ok, _dot_nt(q, k_cur), NEG_BIG)
            s_prev = jnp.where(prev_ok, _dot_nt(q, k_prev), NEG_BIG)
            s_meta = jnp.where(meta_ok, _dot_nt(q, k_meta), NEG_BIG)
            sink = jnp.concatenate([jnp.full((BLK, 1), sink_ref[kv * G + b], F32) for b in range(G)], axis=0)
            m = jnp.maximum(jnp.maximum(jnp.max(s_cur, axis=-1, keepdims=True),
                                        jnp.max(s_prev, axis=-1, keepdims=True)),
                            jnp.maximum(jnp.max(s_meta, axis=-1, keepdims=True), sink))
            p_cur = jnp.exp(s_cur - m)
            p_prev = jnp.exp(s_prev - m)
            p_meta = jnp.exp(s_meta - m)
            denom = (jnp.sum(p_cur, axis=-1, keepdims=True) + jnp.sum(p_prev, axis=-1, keepdims=True)
                     + jnp.sum(p_meta, axis=-1, keepdims=True) + jnp.exp(sink - m))
            o = (_dot(p_cur.astype(BF16), v_cur) + _dot(p_prev.astype(BF16), v_prev)
                 + _dot(p_meta.astype(BF16), v_meta))
            outs.append(o / denom)
        o = jnp.where(halves[0], outs[0], outs[1])
        for b in range(G):
            o_ref[rows, b * BLK:(b + 1) * BLK] = o[b * BLK:(b + 1) * BLK, :]


def _swa(proj, sinks, B, Lp):
    T = proj.shape[0]
    nb = Lp // BLK
    S = _blocks_per_step(nb, (5,))
    ns = nb // S
    k_col, v_col = RET_W // BLK, RET_W // BLK + 1
    cur = lambda j: pl.BlockSpec((S * BLK, BLK), lambda b, n: (b * ns + n, j))
    prev = lambda j: pl.BlockSpec((BLK, BLK), lambda b, n: (b * nb + jnp.maximum(n * S - 1, 0), j))
    first = lambda j: pl.BlockSpec((BLK, BLK), lambda b, n: (b * nb, j))
    return pl.pallas_call(
        functools.partial(_swa_kernel, S=S),
        grid=(B, ns),
        in_specs=[pl.BlockSpec((S * BLK, RET_W), lambda b, n: (b * ns + n, 0)),
                  cur(k_col), cur(v_col), prev(k_col), prev(v_col), first(k_col), first(v_col),
                  pl.BlockSpec(memory_space=pltpu.SMEM)],
        out_specs=pl.BlockSpec((S * BLK, RET_W), lambda b, n: (b * ns + n, 0)),
        out_shape=jax.ShapeDtypeStruct((T, RET_W), F32),
        compiler_params=_params(2),
        name="swa_sink",
    )(proj, proj, proj, proj, proj, proj, proj, sinks.astype(F32))


def _sb_kernel(q_ref, k_ref, v_ref, o_ref, acc_ref, run_ref, *, S):
    n = pl.program_id(2)
    heads = BLK // SB_DH
    row = lax.broadcasted_iota(jnp.int32, (BLK, BLK), 0)
    col = lax.broadcasted_iota(jnp.int32, (BLK, BLK), 1)
    suffix = jnp.where(row > col, 1.0, 0.0).astype(BF16)
    ones = jnp.ones((BLK, BLK), BF16)
    sum_rhs = jnp.concatenate([jnp.concatenate([suffix, ones], axis=1)] * 2, axis=0)
    lane = lax.broadcasted_iota(jnp.int32, (S * BLK, BLK), 1)
    q_all = q_ref[...] * SB_DH ** -0.5
    q_head = [jnp.where((lane >= hh * SB_DH) & (lane < (hh + 1) * SB_DH), q_all, 0).reshape(S, BLK, BLK)
              for hh in range(heads)]

    def walk(hh, k, v, valid, first):
        z = jnp.einsum("sqd,skd->sqk", q_head[hh], k, preferred_element_type=F32)
        softplus_neg = jnp.log1p(jnp.exp(-jnp.abs(z)))
        log_beta = jnp.minimum(z, 0.0) - softplus_neg
        log_1m = jnp.where(valid, -jnp.maximum(z, 0.0) - softplus_neg, 0.0)
        hi, lo = _split(log_1m)
        sums = _dot(jnp.concatenate([hi, lo], axis=-1).reshape(S * BLK, 2 * BLK), sum_rhs)
        sums = sums.reshape(S, BLK, 2 * BLK)
        log_stick = sums[..., :BLK] if first else sums[..., :BLK] + run_ref[hh]
        a = jnp.where(valid, jnp.exp(log_beta + log_stick), 0.0)
        av = jnp.einsum("sqk,skd->sqd", a.astype(BF16), v, preferred_element_type=F32)
        if first:
            acc_ref[hh] = av
            run = sums[..., BLK:]
        else:
            acc_ref[hh] += av
            run = run_ref[hh] + sums[..., BLK:]
        run_ref[hh] = run
        return jnp.max(run)

    start = pl.multiple_of(n * (S * BLK), S * BLK)
    k_diag = k_ref[pl.ds(start, S * BLK), :].reshape(S, BLK, BLK)
    v_diag = v_ref[pl.ds(start, S * BLK), :].reshape(S, BLK, BLK)
    valid = jnp.stack([(col < row) & ((col >= PAD) | (n * S + s >= 1)) for s in range(S)])
    top = walk(0, k_diag, v_diag, valid, True)
    for hh in range(1, heads):
        top = jnp.maximum(top, walk(hh, k_diag, v_diag, valid, True))

    def cond(carry):
        d, go = carry
        return (d <= n * S + S - 1) & (go > 0)

    def body(carry):
        d, _ = carry
        ks, vs, valids = [], [], []
        for s in range(S):
            jb = n * S + s - d
            at = pl.multiple_of(jnp.maximum(jb, 0) * BLK, BLK)
            ks.append(k_ref[pl.ds(at, BLK), :])
            vs.append(v_ref[pl.ds(at, BLK), :])
            valids.append(jb * BLK + col >= PAD)
        k, v, ok = jnp.stack(ks), jnp.stack(vs), jnp.stack(valids)
        top = walk(0, k, v, ok, False)
        for hh in range(1, heads):
            top = jnp.maximum(top, walk(hh, k, v, ok, False))
        return d + 1, (top >= SB_SKIP).astype(jnp.int32)

    lax.while_loop(cond, body, (jnp.int32(1), (top >= SB_SKIP).astype(jnp.int32)))
    o = acc_ref[0]
    for hh in range(1, heads):
        o = jnp.where(lane.reshape(S, BLK, BLK) >= hh * SB_DH, acc_ref[hh], o)
    o_ref[...] = o.reshape(S * BLK, BLK)


def _stick_breaking(proj, B, Lp):
    T = proj.shape[0]
    nb = Lp // BLK
    S = _blocks_per_step(nb, (5,))
    ns = nb // S
    pairs = SB_HEADS * SB_DH // BLK
    heads = BLK // SB_DH
    q0 = (SWA_HEADS + 2 * SWA_KV_HEADS) * SWA_DH // BLK
    k0, v0 = q0 + pairs, q0 + 2 * pairs
    proj3 = proj.reshape(B, Lp, proj.shape[1])
    seq = lambda c0: pl.BlockSpec((None, Lp, BLK), lambda b, p, n: (b, 0, c0 + p))
    return pl.pallas_call(
        functools.partial(_sb_kernel, S=S),
        grid=(B, pairs, ns),
        in_specs=[pl.BlockSpec((S * BLK, BLK), lambda b, p, n: (b * ns + n, q0 + p)), seq(k0), seq(v0)],
        out_specs=pl.BlockSpec((S * BLK, BLK), lambda b, p, n: (b * ns + n, p)),
        out_shape=jax.ShapeDtypeStruct((T, RET_W), F32),
        scratch_shapes=[pltpu.VMEM((heads, S, BLK, BLK), F32), pltpu.VMEM((heads, S, BLK, BLK), F32)],
        compiler_params=_params(3),
        name="stick_breaking",
    )(proj, proj3, proj3)


def _rotation_tables(Lp):
    half = RET_DK // 2
    inv_freq = 1.0 / (10000.0 ** jnp.linspace(0.0, 1.0, half, dtype=F32))
    pos = jnp.arange(Lp, dtype=F32) - float(PAD)
    ang = pos[:, None] * inv_freq[None, :]
    cos, sin = jnp.cos(ang), jnp.sin(ang)
    return jnp.concatenate([cos, cos], axis=1), jnp.concatenate([-sin, sin], axis=1)


def _ab_weight(w_in):
    D = w_in.shape[0]
    perm = np.concatenate([np.arange(0, RET_DK, 2), np.arange(1, RET_DK, 2)])
    qk_perm = np.concatenate([h * RET_DK + perm for h in range(RET_HEADS)])
    cols = np.concatenate([qk_perm, RET_W + qk_perm, np.arange(2 * RET_W, w_in.shape[1])])
    return jnp.concatenate([w_in[:, cols], jnp.zeros((D, AB_COLS - w_in.shape[1]), w_in.dtype)], axis=1)


def kernel(x, meta_tokens, norm_gains, ffn_w_gate, ffn_w_up, ffn_w_down, ab_w_in, ab_conv_w, ab_a_log, ab_dt_bias, ab_out_norm, ab_w_out, cd_w_in, cd_sinks, cd_w_out):
    B, S, D = x.shape
    Lp = S + BLK
    meta = jnp.broadcast_to(meta_tokens[None].astype(x.dtype), (B, N_META, D))
    h = jnp.concatenate([jnp.zeros((B, PAD, D), x.dtype), meta, x], axis=1).reshape(B * Lp, D)
    cos, sin = _rotation_tables(Lp)
    swa_perm = _swa_head_perm()
    swa_w = SWA_HEADS * SWA_DH
    for i in range(norm_gains.shape[0]):
        g = norm_gains[i]
        j = i // 2
        h = _ffn(h, g[0], g[1], ffn_w_gate[i, 0], ffn_w_up[i, 0], ffn_w_down[i, 0])
        if i % 2 == 0:
            proj = _proj(h, g[2], _ab_weight(ab_w_in[j]), F32)
            mix_a = _retention(proj, cos, sin, B, Lp)
            mix_b = _gdn(proj, ab_conv_w[j], ab_a_log[j], ab_dt_bias[j], ab_out_norm[j], B, Lp)
            h = _outproj(mix_a, mix_b, h, g[3], ab_w_out[j])
        else:
            w_in = jnp.concatenate([cd_w_in[j][:, swa_perm], cd_w_in[j][:, swa_w:]], axis=1)
            w_out = jnp.concatenate([cd_w_out[j][swa_perm], cd_w_out[j][swa_w:]], axis=0)
            proj = _proj(h, g[2], w_in, BF16)
            mix_a = _swa(proj, cd_sinks[j], B, Lp)
            mix_b = _stick_breaking(proj, B, Lp)
            h = _outproj(mix_a, mix_b, h, g[3], w_out)
        h = _ffn(h, g[4], g[5], ffn_w_gate[i, 1], ffn_w_up[i, 1], ffn_w_down[i, 1])
    return h.reshape(B, Lp, D)[:, BLK:]
```

```python
import functools
import math

import numpy as np
import jax
import jax.numpy as jnp
from jax import lax
from jax.experimental import pallas as pl
from jax.experimental.pallas import tpu as pltpu

F32 = jnp.float32
BF16 = jnp.bfloat16

N_META = 16
NORM_EPS = 1e-6
BLK = 128
PAD = BLK - N_META

RET_HEADS, RET_DK, RET_DV = 4, 128, 128
GDN_HEADS, GDN_DK, GDN_DV, GDN_CONV = 4, 128, 128, 4
SWA_HEADS, SWA_KV_HEADS, SWA_DH = 8, 2, 64
SB_HEADS, SB_DH = 8, 64
SOLVE_BLK = 16

RET_W = RET_HEADS * RET_DK
AB_GATE_COL = 4096
AB_COLS = 4352
CONV_HALO = 8

VMEM_LIMIT = 56 * 1024 * 1024
NEG_BIG = -1e30
SB_SKIP = -87.5


def _rms(x, g):
    return x * lax.rsqrt(jnp.mean(x * x, axis=-1, keepdims=True) + NORM_EPS) * g


def _silu(x):
    return x * jax.nn.sigmoid(x)


def _dot(a, b):
    return jnp.dot(a, b, preferred_element_type=F32)


def _dot_nt(a, b):
    return lax.dot_general(a, b, (((1,), (1,)), ((), ())), preferred_element_type=F32)


def _dot_tn(a, b):
    return lax.dot_general(a, b, (((0,), (0,)), ((), ())), preferred_element_type=F32)


def _split(a):
    hi = a.astype(BF16)
    return hi, (a - hi.astype(F32)).astype(BF16)


def _dot3(a, b, dot=_dot):
    ah, al = _split(a)
    bh, bl = _split(b)
    return dot(ah, bh) + (dot(ah, bl) + dot(al, bh))


def _params(n_grid, parallel=True):
    sem = ("parallel",) + ("arbitrary",) * (n_grid - 1) if parallel else ("arbitrary",) * n_grid
    return pltpu.CompilerParams(dimension_semantics=sem, vmem_limit_bytes=VMEM_LIMIT)


def _const_spec(shape):
    nd = len(shape)
    return pl.BlockSpec(shape, lambda *_: (0,) * nd, pipeline_mode=pl.Buffered(1))


def _row_tile(rows, target):
    tile = target
    while rows % tile:
        tile -= BLK
    return tile


def _blocks_per_step(nb, prefs):
    for s in prefs:
        if nb % s == 0:
            return s
    return 1


def _ffn_kernel(h_ref, gpre_ref, gpost_ref, wg_ref, wu_ref, wd_ref, o_ref, xn_ref, act_ref, *, tf):
    xn_ref[...] = _rms(h_ref[...], gpre_ref[...]).astype(BF16)
    for c in range(0, wg_ref.shape[1], tf):
        xn = xn_ref[...]
        g = _dot(xn, wg_ref[:, c:c + tf])
        u = _dot(xn, wu_ref[:, c:c + tf])
        act_ref[:, c:c + tf] = (_silu(g) * u).astype(BF16)
    y = _dot(act_ref[...], wd_ref[...])
    o_ref[...] = h_ref[...] + 0.5 * _rms(y, gpost_ref[...])


def _ffn(h, g_pre, g_post, w_gate, w_up, w_down, *, tm=512, tf=256):
    T, D = h.shape
    F = w_gate.shape[1]
    tm = _row_tile(T, tm)
    row = pl.BlockSpec((tm, D), lambda i: (i, 0))
    return pl.pallas_call(
        functools.partial(_ffn_kernel, tf=tf),
        grid=(T // tm,),
        in_specs=[row, _const_spec((1, D)), _const_spec((1, D)),
                  _const_spec((D, F)), _const_spec((D, F)), _const_spec((F, D))],
        out_specs=row,
        out_shape=jax.ShapeDtypeStruct((T, D), F32),
        scratch_shapes=[pltpu.VMEM((tm, D), BF16), pltpu.VMEM((tm, F), BF16)],
        compiler_params=_params(1),
        name="ffn",
    )(h, g_pre.reshape(1, D), g_post.reshape(1, D),
      w_gate.astype(BF16), w_up.astype(BF16), w_down.astype(BF16))


def _proj_kernel(h_ref, g_ref, w_ref, o_ref, *, tn):
    xn = _rms(h_ref[...], g_ref[...]).astype(BF16)
    for c in range(0, w_ref.shape[1], tn):
        o_ref[:, c:c + tn] = _dot(xn, w_ref[:, c:c + tn]).astype(o_ref.dtype)


def _proj(h, g, w, out_dtype, *, tm=512, tn=256):
    T, D = h.shape
    N = w.shape[1]
    tm = _row_tile(T, tm)
    return pl.pallas_call(
        functools.partial(_proj_kernel, tn=tn),
        grid=(T // tm,),
        in_specs=[pl.BlockSpec((tm, D), lambda i: (i, 0)), _const_spec((1, D)), _const_spec((D, N))],
        out_specs=pl.BlockSpec((tm, N), lambda i: (i, 0)),
        out_shape=jax.ShapeDtypeStruct((T, N), out_dtype),
        compiler_params=_params(1),
        name="in_proj",
    )(h, g.reshape(1, D), w.astype(BF16))


def _outproj_kernel(a_ref, b_ref, h_ref, g_ref, wa_ref, wb_ref, o_ref):
    y = _dot(a_ref[...].astype(BF16), wa_ref[...]) + _dot(b_ref[...].astype(BF16), wb_ref[...])
    o_ref[...] = h_ref[...] + _rms(y, g_ref[...])


def _outproj(a, b, h, g, w_out, *, tm=512):
    T, D = h.shape
    Ka, Kb = a.shape[1], b.shape[1]
    tm = _row_tile(T, tm)
    w = w_out.astype(BF16)
    row = lambda n: pl.BlockSpec((tm, n), lambda i: (i, 0))
    return pl.pallas_call(
        _outproj_kernel,
        grid=(T // tm,),
        in_specs=[row(Ka), row(Kb), row(D), _const_spec((1, D)), _const_spec((Ka, D)), _const_spec((Kb, D))],
        out_specs=row(D),
        out_shape=jax.ShapeDtypeStruct((T, D), F32),
        compiler_params=_params(1),
        name="out_proj",
    )(a, b, h, g.reshape(1, D), w[:Ka], w[Ka:])


def _ret_kernel(q_ref, k_ref, v_ref, gate_ref, cos_ref, sin_ref, o_ref, s_ref, dec_ref, zx_ref):
    nbatch, C = q_ref.shape[0], q_ref.shape[1]
    n = pl.program_id(0)
    log_gamma = [math.log1p(-2.0 ** (-5.0 - hd)) for hd in range(RET_HEADS)]

    @pl.when(n == 0)
    def _init():
        s_ref[...] = jnp.zeros_like(s_ref)
        diff = (lax.broadcasted_iota(jnp.int32, (C, C), 0) - lax.broadcasted_iota(jnp.int32, (C, C), 1)).astype(F32)
        idx = lax.broadcasted_iota(jnp.int32, (C, RET_DK), 0).astype(F32)
        for hd, lg in enumerate(log_gamma):
            dec_ref[hd] = jnp.where(diff >= 0, jnp.exp(jnp.maximum(diff, 0.0) * lg), 0.0)
            zx_ref[0, hd] = jnp.exp((C - 1.0 - idx) * lg)
            zx_ref[1, hd] = jnp.exp((idx + 1.0) * lg)

    cos = cos_ref[...]
    sin = sin_ref[...]
    chains = [(b, hd) for b in range(nbatch) for hd in range(RET_HEADS)]
    sl = lambda hd: slice(hd * RET_DK, (hd + 1) * RET_DK)
    rot = lambda t: t * cos + pltpu.roll(t, RET_DK // 2, 1) * sin
    qr = [rot(q_ref[b, :, sl(hd)]) for b, hd in chains]
    kr = [rot(k_ref[b, :, sl(hd)]) * RET_DK ** -0.5 for b, hd in chains]
    v = [v_ref[b, :, sl(hd)].astype(BF16) for b, hd in chains]
    scores = [_dot_nt(qq.astype(BF16), kk.astype(BF16)) * dec_ref[hd] for qq, kk, (_, hd) in zip(qr, kr, chains)]
    intra = [_dot(sc.astype(BF16), vv) for sc, vv in zip(scores, v)]
    state = [s_ref[i] for i in range(len(chains))]
    cross = [_dot((qq * zx_ref[1, hd]).astype(BF16), st.astype(BF16)) for qq, st, (_, hd) in zip(qr, state, chains)]
    kv = [_dot_tn((kk * zx_ref[0, hd]).astype(BF16), vv) for kk, vv, (_, hd) in zip(kr, v, chains)]
    for i, (b, hd) in enumerate(chains):
        s_ref[i] = state[i] * math.exp(C * log_gamma[hd]) + kv[i]
        o = intra[i] + cross[i]
        mu = jnp.mean(o, axis=-1, keepdims=True)
        var = jnp.mean(jnp.square(o - mu), axis=-1, keepdims=True)
        o_ref[b, :, sl(hd)] = (o - mu) * lax.rsqrt(var + NORM_EPS) * _silu(gate_ref[b, :, sl(hd)])


def _retention(proj, cos, sin, B, Lp):
    nb = Lp // BLK
    proj3 = proj.reshape(B, Lp, proj.shape[1])
    col = lambda j: pl.BlockSpec((B, BLK, RET_W), lambda n: (0, n, j))
    tab = pl.BlockSpec((BLK, RET_DK), lambda n: (n, 0))
    out = pl.pallas_call(
        _ret_kernel,
        grid=(nb,),
        in_specs=[col(0), col(1), col(2), col(3), tab, tab],
        out_specs=pl.BlockSpec((B, BLK, RET_W), lambda n: (0, n, 0)),
        out_shape=jax.ShapeDtypeStruct((B, Lp, RET_W), F32),
        scratch_shapes=[pltpu.VMEM((B * RET_HEADS, RET_DK, RET_DV), F32),
                        pltpu.VMEM((RET_HEADS, BLK, BLK), F32),
                        pltpu.VMEM((2, RET_HEADS, BLK, RET_DK), F32)],
        compiler_params=_params(1, parallel=False),
        name="retention",
    )(proj3, proj3, proj3, proj3, cos, sin)
    return out.reshape(B * Lp, RET_W)


def _unit_lower_inverses(a_mats, eye, same_blk):
    d = [jnp.where(same_blk, a, 0.0) for a in a_mats]
    e = [(a - dd).astype(BF16) for a, dd in zip(a_mats, d)]
    x = [-dd for dd in d]
    t = [eye + xx for xx in x]
    for _ in range(3):
        xb = [xx.astype(BF16) for xx in x]
        x = [_dot(xx, xx) for xx in xb]
        t = [tt + _dot(tt.astype(BF16), xx.astype(BF16)) for tt, xx in zip(t, x)]
    tb = [tt.astype(BF16) for tt in t]
    f = [_dot(tt, ee) for tt, ee in zip(tb, e)]
    p = [eye - ff for ff in f]
    for _ in range(2):
        fb = [ff.astype(BF16) for ff in f]
        f = [_dot(ff, ff) for ff in fb]
        p = [pp + _dot(pp.astype(BF16), ff.astype(BF16)) for pp, ff in zip(p, f)]
    return [pp.astype(BF16) for pp in p], tb


def _gdn_kernel(q_ref, k_ref, v_ref, z_ref, gate_ref, cw_ref, alog_ref, dtb_ref, onorm_ref,
                o_ref, s_ref, xbuf_ref):
    nbatch, C = q_ref.shape[0], q_ref.shape[1]
    n = pl.program_id(0)

    @pl.when(n == 0)
    def _init():
        s_ref[...] = jnp.zeros_like(s_ref)
        xbuf_ref[:, :, 0:CONV_HALO, :] = jnp.zeros((3, nbatch, CONV_HALO, RET_W), F32)

    qkv = []
    for j, ref in enumerate((q_ref, k_ref, v_ref)):
        xbuf_ref[j, :, CONV_HALO:CONV_HALO + C, :] = ref[...]
        per_batch = []
        for b in range(nbatch):
            x_ext = xbuf_ref[j, b]
            y = cw_ref[j, GDN_CONV - 1:GDN_CONV, :] * x_ext[CONV_HALO:]
            for shift in range(1, GDN_CONV):
                tap = GDN_CONV - 1 - shift
                y = y + cw_ref[j, tap:tap + 1, :] * pltpu.roll(x_ext, shift, 0)[CONV_HALO:]
            per_batch.append(_silu(y))
        xbuf_ref[j, :, 0:CONV_HALO, :] = xbuf_ref[j, :, C:C + CONV_HALO, :]
        qkv.append(per_batch)
    qc, kc, vc = qkv

    row = lax.broadcasted_iota(jnp.int32, (C, C), 0)
    col = lax.broadcasted_iota(jnp.int32, (C, C), 1)
    incl = row >= col
    strict = row > col
    same_blk = (row // SOLVE_BLK) == (col // SOLVE_BLK)
    eye = jnp.where(row == col, 1.0, 0.0).astype(F32)
    tri_incl = jnp.where(incl, 1.0, 0.0).astype(F32)
    tri_upper = jnp.where(row <= col, 1.0, 0.0).astype(F32)

    gates = [gate_ref[b] for b in range(nbatch)]
    beta_all = [jax.nn.sigmoid(gt) for gt in gates]
    g_all = [-jnp.exp(alog_ref[...]) * jax.nn.softplus(gt + dtb_ref[...]) for gt in gates]
    gcum_all = [_dot3(tri_incl, g) for g in g_all]
    gcum_t = [_dot3(g, tri_upper, _dot_tn) for g in g_all]

    chains = [(b, hd) for b in range(nbatch) for hd in range(GDN_HEADS)]
    head = lambda t, b, hd: t[b][:, hd * GDN_DK:(hd + 1) * GDN_DK]
    l2n = lambda t: t * lax.rsqrt(jnp.sum(t * t, axis=-1, keepdims=True) + NORM_EPS)
    q = [l2n(head(qc, b, hd)) * GDN_DK ** -0.5 for b, hd in chains]
    k = [l2n(head(kc, b, hd)) for b, hd in chains]
    v = [head(vc, b, hd) for b, hd in chains]
    beta = [beta_all[b][:, hd:hd + 1] for b, hd in chains]
    gcum = [gcum_all[b][:, GDN_HEADS + hd:GDN_HEADS + hd + 1] for b, hd in chains]
    gcum_row = [gcum_t[b][GDN_HEADS + hd:GDN_HEADS + hd + 1, :] for b, hd in chains]
    decay = [jnp.where(incl, jnp.exp(jnp.where(incl, gc - gr, 0.0)), 0.0) for gc, gr in zip(gcum, gcum_row)]
    k_beta = [kk * bb for kk, bb in zip(k, beta)]
    kb16 = [kk.astype(BF16) for kk in k]
    a_mat = [jnp.where(strict, _dot_nt(kbt.astype(BF16), k16) * dc, 0.0) for kbt, k16, dc in zip(k_beta, kb16, decay)]
    p_mat, t_inv = _unit_lower_inverses(a_mat, eye, same_blk)
    e_gcum = [jnp.exp(gc) for gc in gcum]
    rhs = [jnp.concatenate([vv * bb, kbt * eg], axis=-1) for vv, bb, kbt, eg in zip(v, beta, k_beta, e_gcum)]
    sol = [_dot(tt, rr.astype(BF16)) for tt, rr in zip(t_inv, rhs)]
    sol = [_dot(pp, ss.astype(BF16)) for pp, ss in zip(p_mat, sol)]
    qk = [jnp.where(incl, _dot_nt(qq.astype(BF16), k16) * dc, 0.0) for qq, k16, dc in zip(q, kb16, decay)]
    g_last = [gc[C - 1:C, :] for gc in gcum]
    k_tail = [(kk * jnp.exp(gl - gc)).astype(BF16) for kk, gl, gc in zip(k, g_last, gcum)]
    q_dec = [(qq * eg).astype(BF16) for qq, eg in zip(q, e_gcum)]

    state = [s_ref[i] for i in range(len(chains))]
    state_b = [st.astype(BF16) for st in state]
    v_new = [ss[:, :GDN_DV] - _dot(ss[:, GDN_DV:].astype(BF16), sb) for ss, sb in zip(sol, state_b)]
    v_new_b = [vn.astype(BF16) for vn in v_new]
    out = [_dot(qd, sb) + _dot(qkm.astype(BF16), vn) for qd, sb, qkm, vn in zip(q_dec, state_b, qk, v_new_b)]
    for i, (b, hd) in enumerate(chains):
        s_ref[i] = state[i] * jnp.exp(g_last[i]) + _dot_tn(k_tail[i], v_new_b[i])
        sl = slice(hd * GDN_DV, (hd + 1) * GDN_DV)
        o_ref[b, :, sl] = _rms(out[i], onorm_ref[...]) * _silu(z_ref[b, :, sl])


def _gdn(proj, conv_w, a_log, dt_bias, out_norm, B, Lp):
    C = BLK
    nc = Lp // C
    proj3 = proj.reshape(B, Lp, proj.shape[1])
    col = lambda j: pl.BlockSpec((B, C, RET_W), lambda n: (0, n, j))
    gate_lanes = jnp.zeros((1, BLK), F32)
    alog = gate_lanes.at[0, GDN_HEADS:2 * GDN_HEADS].set(a_log.astype(F32))
    dtb = gate_lanes.at[0, GDN_HEADS:2 * GDN_HEADS].set(dt_bias.astype(F32))
    cw = conv_w.astype(F32).reshape(GDN_CONV, 3, RET_W).transpose(1, 0, 2)
    out = pl.pallas_call(
        _gdn_kernel,
        grid=(nc,),
        in_specs=[col(4), col(5), col(6), col(7),
                  pl.BlockSpec((B, C, BLK), lambda n: (0, n, AB_GATE_COL // BLK)),
                  _const_spec((3, GDN_CONV, RET_W)), _const_spec((1, BLK)), _const_spec((1, BLK)),
                  _const_spec((1, GDN_DV))],
        out_specs=pl.BlockSpec((B, C, RET_W), lambda n: (0, n, 0)),
        out_shape=jax.ShapeDtypeStruct((B, Lp, RET_W), F32),
        scratch_shapes=[pltpu.VMEM((B * GDN_HEADS, GDN_DK, GDN_DV), F32),
                        pltpu.VMEM((3, B, C + CONV_HALO, RET_W), F32)],
        compiler_params=_params(1, parallel=False),
        name="gated_deltanet",
    )(proj3, proj3, proj3, proj3, proj3, cw, alog, dtb, out_norm.astype(F32).reshape(1, GDN_DV))
    return out.reshape(B * Lp, RET_W)


def _swa_head_perm():
    G = SWA_HEADS // SWA_KV_HEADS
    heads = [kv * G + b for b in range(G) for kv in range(SWA_KV_HEADS)]
    return np.concatenate([np.arange(h * SWA_DH, (h + 1) * SWA_DH) for h in heads])


def _swa_kernel(q_ref, kc_ref, vc_ref, kp_ref, vp_ref, km_ref, vm_ref, sink_ref, o_ref, *, S):
    n = pl.program_id(1)
    G = SWA_HEADS // SWA_KV_HEADS
    R = G * BLK
    r = lax.broadcasted_iota(jnp.int32, (R, BLK), 0) & (BLK - 1)
    col = lax.broadcasted_iota(jnp.int32, (R, BLK), 1)
    lower = col <= r
    upper = col > r
    is_meta = col >= PAD
    halves = [col < SWA_DH, col >= SWA_DH]
    k_meta, v_meta = km_ref[...], vm_ref[...]
    for s in range(S):
        blk = n * S + s
        rows = slice(s * BLK, (s + 1) * BLK)
        k_cur, v_cur = kc_ref[rows, :], vc_ref[rows, :]
        if s == 0:
            k_prev, v_prev = kp_ref[...], vp_ref[...]
        else:
            k_prev, v_prev = kc_ref[(s - 1) * BLK:s * BLK, :], vc_ref[(s - 1) * BLK:s * BLK, :]
        cur_ok = lower & (blk >= 1)
        prev_ok = upper & (blk >= 2)
        meta_ok = is_meta & ((blk >= 1) | lower)
        q_st = jnp.concatenate([q_ref[rows, b * BLK:(b + 1) * BLK] for b in range(G)], axis=0) * SWA_DH ** -0.5
        outs = []
        for kv in range(SWA_KV_HEADS):
            q = jnp.where(halves[kv], q_st, 0)
            s_cur = jnp.where(cur_ok, _dot_nt(q, k_cur), NEG_BIG)
            s_prev = jnp.where(prev_ok, _dot_nt(q, k_prev), NEG_BIG)
            s_meta = jnp.where(meta_ok, _dot_nt(q, k_meta), NEG_BIG)
            sink = jnp.concatenate([jnp.full((BLK, 1), sink_ref[kv * G + b], F32) for b in range(G)], axis=0)
            m = jnp.maximum(jnp.max(jnp.maximum(jnp.maximum(s_cur, s_prev), s_meta), axis=-1, keepdims=True), sink)
            p_cur = jnp.exp(s_cur - m)
            p_prev = jnp.exp(s_prev - m)
            p_meta = jnp.exp(s_meta - m)
            denom = jnp.sum(p_cur + p_prev + p_meta, axis=-1, keepdims=True) + jnp.exp(sink - m)
            o = (_dot(p_cur.astype(BF16), v_cur) + _dot(p_prev.astype(BF16), v_prev)
                 + _dot(p_meta.astype(BF16), v_meta))
            outs.append(o / denom)
        o = jnp.where(halves[0], outs[0], outs[1])
        for b in range(G):
            o_ref[rows, b * BLK:(b + 1) * BLK] = o[b * BLK:(b + 1) * BLK, :]


def _swa(proj, sinks, B, Lp):
    T = proj.shape[0]
    nb = Lp // BLK
    S = _blocks_per_step(nb, (5,))
    ns = nb // S
    k_col, v_col = RET_W // BLK, RET_W // BLK + 1
    cur = lambda j: pl.BlockSpec((S * BLK, BLK), lambda b, n: (b * ns + n, j))
    prev = lambda j: pl.BlockSpec((BLK, BLK), lambda b, n: (b * nb + jnp.maximum(n * S - 1, 0), j))
    first = lambda j: pl.BlockSpec((BLK, BLK), lambda b, n: (b * nb, j))
    return pl.pallas_call(
        functools.partial(_swa_kernel, S=S),
        grid=(B, ns),
        in_specs=[pl.BlockSpec((S * BLK, RET_W), lambda b, n: (b * ns + n, 0)),
                  cur(k_col), cur(v_col), prev(k_col), prev(v_col), first(k_col), first(v_col),
                  pl.BlockSpec(memory_space=pltpu.SMEM)],
        out_specs=pl.BlockSpec((S * BLK, RET_W), lambda b, n: (b * ns + n, 0)),
        out_shape=jax.ShapeDtypeStruct((T, RET_W), F32),
        compiler_params=_params(2),
        name="swa_sink",
    )(proj, proj, proj, proj, proj, proj, proj, sinks.astype(F32))


def _sb_kernel(q_ref, k_ref, v_ref, o_ref, acc_ref, run_ref, *, S):
    n = pl.program_id(2)
    heads = BLK // SB_DH
    row = lax.broadcasted_iota(jnp.int32, (BLK, BLK), 0)
    col = lax.broadcasted_iota(jnp.int32, (BLK, BLK), 1)
    suffix = jnp.where(row > col, 1.0, 0.0).astype(BF16)
    ones = jnp.ones((BLK, BLK), BF16)
    sum_rhs = jnp.concatenate([jnp.concatenate([suffix, ones], axis=1)] * 2, axis=0)
    lane = lax.broadcasted_iota(jnp.int32, (S * BLK, BLK), 1)
    q_all = q_ref[...] * SB_DH ** -0.5
    q_head = [jnp.where((lane >= hh * SB_DH) & (lane < (hh + 1) * SB_DH), q_all, 0).reshape(S, BLK, BLK)
              for hh in range(heads)]

    def walk(hh, k, v, valid, first):
        z = jnp.einsum("sqd,skd->sqk", q_head[hh], k, preferred_element_type=F32)
        softplus_neg = jnp.log(1.0 + jnp.exp(-jnp.abs(z)))
        log_beta = jnp.minimum(z, 0.0) - softplus_neg
        log_1m = log_beta - z
        if valid is not None:
            log_1m = jnp.where(valid, log_1m, 0.0)
        hi, lo = _split(log_1m)
        sums = _dot(jnp.concatenate([hi, lo], axis=-1).reshape(S * BLK, 2 * BLK), sum_rhs)
        sums = sums.reshape(S, BLK, 2 * BLK)
        log_stick = sums[..., :BLK] if first else sums[..., :BLK] + run_ref[hh]
        a = jnp.exp(log_beta + log_stick)
        if valid is not None:
            a = jnp.where(valid, a, 0.0)
        av = jnp.einsum("sqk,skd->sqd", a.astype(BF16), v, preferred_element_type=F32)
        if first:
            acc_ref[hh] = av
            run = sums[..., BLK:]
        else:
            acc_ref[hh] += av
            run = run_ref[hh] + sums[..., BLK:]
        run_ref[hh] = run
        return jnp.max(run)

    def walk_heads(k, v, valid, first):
        top = walk(0, k, v, valid, first)
        for hh in range(1, heads):
            top = jnp.maximum(top, walk(hh, k, v, valid, first))
        return (top >= SB_SKIP).astype(jnp.int32)

    def slab(ref, first_blk):
        at = pl.multiple_of(first_blk * BLK, BLK)
        return ref[pl.ds(at, S * BLK), :].reshape(S, BLK, BLK)

    valid = jnp.stack([(col < row) & ((col >= PAD) | (n * S + s >= 1)) for s in range(S)])
    go = walk_heads(slab(k_ref, n * S), slab(v_ref, n * S), valid, True)

    def cond(carry):
        d, go = carry
        return (d <= n * S + S - 1) & (go > 0)

    def body(carry):
        d, _ = carry
        first_blk = n * S - d

        def interior():
            return walk_heads(slab(k_ref, first_blk), slab(v_ref, first_blk), None, False)

        def edge():
            ks, vs, valids = [], [], []
            for s in range(S):
                jb = first_blk + s
                at = pl.multiple_of(jnp.maximum(jb, 0) * BLK, BLK)
                ks.append(k_ref[pl.ds(at, BLK), :])
                vs.append(v_ref[pl.ds(at, BLK), :])
                valids.append(jb * BLK + col >= PAD)
            return walk_heads(jnp.stack(ks), jnp.stack(vs), jnp.stack(valids), False)

        return d + 1, lax.cond(first_blk >= 1, interior, edge)

    lax.while_loop(cond, body, (jnp.int32(1), go))
    o = acc_ref[0]
    for hh in range(1, heads):
        o = jnp.where(lane.reshape(S, BLK, BLK) >= hh * SB_DH, acc_ref[hh], o)
    o_ref[...] = o.reshape(S * BLK, BLK)


def _stick_breaking(proj, B, Lp):
    T = proj.shape[0]
    nb = Lp // BLK
    S = _blocks_per_step(nb, (5,))
    ns = nb // S
    pairs = SB_HEADS * SB_DH // BLK
    heads = BLK // SB_DH
    q0 = (SWA_HEADS + 2 * SWA_KV_HEADS) * SWA_DH // BLK
    k0, v0 = q0 + pairs, q0 + 2 * pairs
    proj3 = proj.reshape(B, Lp, proj.shape[1])
    seq = lambda c0: pl.BlockSpec((None, Lp, BLK), lambda b, p, n: (b, 0, c0 + p))
    return pl.pallas_call(
        functools.partial(_sb_kernel, S=S),
        grid=(B, pairs, ns),
        in_specs=[pl.BlockSpec((S * BLK, BLK), lambda b, p, n: (b * ns + n, q0 + p)), seq(k0), seq(v0)],
        out_specs=pl.BlockSpec((S * BLK, BLK), lambda b, p, n: (b * ns + n, p)),
        out_shape=jax.ShapeDtypeStruct((T, RET_W), F32),
        scratch_shapes=[pltpu.VMEM((heads, S, BLK, BLK), F32), pltpu.VMEM((heads, S, BLK, BLK), F32)],
        compiler_params=_params(3),
        name="stick_breaking",
    )(proj, proj3, proj3)


def _rotation_tables(Lp):
    half = RET_DK // 2
    inv_freq = 1.0 / (10000.0 ** jnp.linspace(0.0, 1.0, half, dtype=F32))
    pos = jnp.arange(Lp, dtype=F32) - float(PAD)
    ang = pos[:, None] * inv_freq[None, :]
    cos, sin = jnp.cos(ang), jnp.sin(ang)
    return jnp.concatenate([cos, cos], axis=1), jnp.concatenate([-sin, sin], axis=1)


def _ab_weight(w_in):
    D = w_in.shape[0]
    perm = np.concatenate([np.arange(0, RET_DK, 2), np.arange(1, RET_DK, 2)])
    qk_perm = np.concatenate([h * RET_DK + perm for h in range(RET_HEADS)])
    cols = np.concatenate([qk_perm, RET_W + qk_perm, np.arange(2 * RET_W, w_in.shape[1])])
    return jnp.concatenate([w_in[:, cols], jnp.zeros((D, AB_COLS - w_in.shape[1]), w_in.dtype)], axis=1)


def kernel(x, meta_tokens, norm_gains, ffn_w_gate, ffn_w_up, ffn_w_down, ab_w_in, ab_conv_w, ab_a_log, ab_dt_bias, ab_out_norm, ab_w_out, cd_w_in, cd_sinks, cd_w_out):
    B, S, D = x.shape
    Lp = S + BLK
    meta = jnp.broadcast_to(meta_tokens[None].astype(x.dtype), (B, N_META, D))
    h = jnp.concatenate([jnp.zeros((B, PAD, D), x.dtype), meta, x], axis=1).reshape(B * Lp, D)
    cos, sin = _rotation_tables(Lp)
    swa_perm = _swa_head_perm()
    swa_w = SWA_HEADS * SWA_DH
    for i in range(norm_gains.shape[0]):
        g = norm_gains[i]
        j = i // 2
        h = _ffn(h, g[0], g[1], ffn_w_gate[i, 0], ffn_w_up[i, 0], ffn_w_down[i, 0])
        if i % 2 == 0:
            proj = _proj(h, g[2], _ab_weight(ab_w_in[j]), F32)
            mix_a = _retention(proj, cos, sin, B, Lp)
            mix_b = _gdn(proj, ab_conv_w[j], ab_a_log[j], ab_dt_bias[j], ab_out_norm[j], B, Lp)
            h = _outproj(mix_a, mix_b, h, g[3], ab_w_out[j])
        else:
            w_in = jnp.concatenate([cd_w_in[j][:, swa_perm], cd_w_in[j][:, swa_w:]], axis=1)
            w_out = jnp.concatenate([cd_w_out[j][swa_perm], cd_w_out[j][swa_w:]], axis=0)
            proj = _proj(h, g[2], w_in, BF16)
            mix_a = _swa(proj, cd_sinks[j], B, Lp)
            mix_b = _stick_breaking(proj, B, Lp)
            h = _outproj(mix_a, mix_b, h, g[3], w_out)
        h = _ffn(h, g[4], g[5], ffn_w_gate[i, 1], ffn_w_up[i, 1], ffn_w_down[i, 1])
    return h.reshape(B, Lp, D)[:, BLK:]
```

```python
import functools
import math

import numpy as np
import jax
import jax.numpy as jnp
from jax import lax
from jax.experimental import pallas as pl
from jax.experimental.pallas import tpu as pltpu

F32 = jnp.float32
BF16 = jnp.bfloat16

N_META = 16
NORM_EPS = 1e-6
BLK = 128
PAD = BLK - N_META

RET_HEADS, RET_DK, RET_DV = 4, 128, 128
GDN_HEADS, GDN_DK, GDN_DV, GDN_CONV = 4, 128, 128, 4
SWA_HEADS, SWA_KV_HEADS, SWA_DH = 8, 2, 64
SB_HEADS, SB_DH = 8, 64
SOLVE_BLK = 16

RET_W = RET_HEADS * RET_DK
AB_GATE_COL = 4096
AB_COLS = AB_GATE_COL + BLK
CONV_HALO = 8

VMEM_LIMIT = 56 * 1024 * 1024
NEG_BIG = -1e30
SB_SKIP = -87.5


def _rms(x, g):
    return x * lax.rsqrt(jnp.mean(x * x, axis=-1, keepdims=True) + NORM_EPS) * g


def _silu(x):
    return x * jax.nn.sigmoid(x)


def _dot(a, b):
    return jnp.dot(a, b, preferred_element_type=F32)


def _dot_nt(a, b):
    return lax.dot_general(a, b, (((1,), (1,)), ((), ())), preferred_element_type=F32)


def _dot_tn(a, b):
    return lax.dot_general(a, b, (((0,), (0,)), ((), ())), preferred_element_type=F32)


def _split(a):
    hi = a.astype(BF16)
    return hi, (a - hi.astype(F32)).astype(BF16)


def _dot3(a, b, dot=_dot):
    ah, al = _split(a)
    bh, bl = _split(b)
    return dot(ah, bh) + (dot(ah, bl) + dot(al, bh))


def _params(n_grid, parallel=True):
    sem = ("parallel",) + ("arbitrary",) * (n_grid - 1) if parallel else ("arbitrary",) * n_grid
    return pltpu.CompilerParams(dimension_semantics=sem, vmem_limit_bytes=VMEM_LIMIT)


def _const_spec(shape):
    nd = len(shape)
    return pl.BlockSpec(shape, lambda *_: (0,) * nd, pipeline_mode=pl.Buffered(1))


def _row_tile(rows, target):
    tile = target
    while rows % tile:
        tile -= BLK
    return tile


def _blocks_per_step(nb, prefs):
    for s in prefs:
        if nb % s == 0:
            return s
    return 1


def _ffn_kernel(h_ref, gpre_ref, gpost_ref, wg_ref, wu_ref, wd_ref, *rest, tf):
    o_ref, xn_ref, act_ref = rest[-3:]
    xn_ref[...] = _rms(h_ref[...], gpre_ref[...]).astype(BF16)
    for c in range(0, wg_ref.shape[1], tf):
        xn = xn_ref[...]
        g = _dot(xn, wg_ref[:, c:c + tf])
        u = _dot(xn, wu_ref[:, c:c + tf])
        act_ref[:, c:c + tf] = (_silu(g) * u).astype(BF16)
    y = _dot(act_ref[...], wd_ref[...])
    o_ref[...] = h_ref[...] + 0.5 * _rms(y, gpost_ref[...])


def _ffn(src, g_pre, g_post, w_gate, w_up, w_down, *, tm, src_spec=None, dst_spec=None, n_tiles=None,
         out_rows=None, into=None, tf=256):
    D = src.shape[1]
    F = w_gate.shape[1]
    row = pl.BlockSpec((tm, D), lambda i: (i, 0))
    operands = [src, g_pre.reshape(1, D), g_post.reshape(1, D),
                w_gate.astype(BF16), w_up.astype(BF16), w_down.astype(BF16)]
    in_specs = [src_spec or row, _const_spec((1, D)), _const_spec((1, D)),
                _const_spec((D, F)), _const_spec((D, F)), _const_spec((F, D))]
    aliases = {}
    if into is not None:
        aliases = {len(operands): 0}
        operands.append(into)
        in_specs.append(pl.BlockSpec(memory_space=pl.ANY))
    return pl.pallas_call(
        functools.partial(_ffn_kernel, tf=tf),
        grid=(n_tiles or src.shape[0] // tm,),
        in_specs=in_specs,
        out_specs=dst_spec or row,
        out_shape=jax.ShapeDtypeStruct((out_rows or src.shape[0], D), F32),
        scratch_shapes=[pltpu.VMEM((tm, D), BF16), pltpu.VMEM((tm, F), BF16)],
        input_output_aliases=aliases,
        compiler_params=_params(1),
        name="ffn",
    )(*operands)


def _token_rows(tm, S, Lp, D):
    per_batch = S // tm
    return pl.BlockSpec((pl.Element(tm), pl.Element(D)),
                        lambda i: (BLK * ((i // per_batch) * (Lp // BLK) + 1 + (i % per_batch) * (tm // BLK)), 0))


def _proj_kernel(h_ref, g_ref, w_ref, o_ref, *, tn):
    xn = _rms(h_ref[...], g_ref[...]).astype(BF16)
    for c in range(0, w_ref.shape[1], tn):
        o_ref[:, c:c + tn] = _dot(xn, w_ref[:, c:c + tn]).astype(o_ref.dtype)


def _proj(h, g, w, out_dtype, *, tm=512, tn=256):
    T, D = h.shape
    N = w.shape[1]
    tm = _row_tile(T, tm)
    return pl.pallas_call(
        functools.partial(_proj_kernel, tn=tn),
        grid=(T // tm,),
        in_specs=[pl.BlockSpec((tm, D), lambda i: (i, 0)), _const_spec((1, D)), _const_spec((D, N))],
        out_specs=pl.BlockSpec((tm, N), lambda i: (i, 0)),
        out_shape=jax.ShapeDtypeStruct((T, N), out_dtype),
        compiler_params=_params(1),
        name="in_proj",
    )(h, g.reshape(1, D), w.astype(BF16))


def _proj_ab_kernel(h_ref, g_ref, w_ref, cos_ref, sin_ref, cw_ref, o16_ref, o32_ref, xn_ref, halo_ref, *, tn):
    rows = h_ref.shape[0]
    W = RET_W

    @pl.when(pl.program_id(1) == 0)
    def _start_of_sequence():
        halo_ref[...] = jnp.zeros_like(halo_ref)

    xn_ref[...] = _rms(h_ref[...], g_ref[...]).astype(BF16)
    cos, sin = cos_ref[...], sin_ref[...]
    rot = lambda t: t * cos + pltpu.roll(t, RET_DK // 2, 1) * sin
    l2n = lambda t: t * lax.rsqrt(jnp.sum(t * t, axis=-1, keepdims=True) + NORM_EPS)
    for c in range(0, AB_GATE_COL, tn):
        pre = _dot(xn_ref[...], w_ref[:, c:c + tn])
        group, off = divmod(c, W)
        if group <= 1:
            scale = 1.0 if group == 0 else RET_DK ** -0.5
            for hs in range(0, tn, RET_DK):
                o16_ref[:, c + hs:c + hs + RET_DK] = (rot(pre[:, hs:hs + RET_DK]) * scale).astype(BF16)
        elif group == 2:
            o16_ref[:, c:c + tn] = pre.astype(BF16)
        elif group == 3:
            o32_ref[:, off:off + tn] = _silu(pre)
        elif group <= 6:
            ch = c - 4 * W
            x_ext = jnp.concatenate([halo_ref[:, ch:ch + tn], pre], axis=0)
            y = cw_ref[GDN_CONV - 1:GDN_CONV, ch:ch + tn] * pre
            for shift in range(1, GDN_CONV):
                tap = GDN_CONV - 1 - shift
                y = y + cw_ref[tap:tap + 1, ch:ch + tn] * pltpu.roll(x_ext, shift, 0)[CONV_HALO:]
            halo_ref[:, ch:ch + tn] = pre[rows - CONV_HALO:rows]
            act = _silu(y)
            if group == 6:
                o32_ref[:, W + ch:W + ch + tn] = act
            else:
                scale = GDN_DK ** -0.5 if group == 4 else 1.0
                for hs in range(0, tn, GDN_DK):
                    o32_ref[:, W + ch + hs:W + ch + hs + GDN_DK] = l2n(act[:, hs:hs + GDN_DK]) * scale
        else:
            o32_ref[:, 4 * W + off:4 * W + off + tn] = _silu(pre)
    o32_ref[:, 5 * W:5 * W + BLK] = _dot(xn_ref[...], w_ref[:, AB_GATE_COL:AB_GATE_COL + BLK])


def _proj_ab(h, g, w_in, conv_w, cos, sin, B, Lp, *, tm=640, tn=256):
    D = h.shape[1]
    tm = _row_tile(Lp, tm)
    W = RET_W
    seq = lambda n: pl.BlockSpec((None, tm, n), lambda b, j: (b, j, 0))
    tab = pl.BlockSpec((tm, RET_DK), lambda b, j: (j, 0))
    return pl.pallas_call(
        functools.partial(_proj_ab_kernel, tn=tn),
        grid=(B, Lp // tm),
        in_specs=[seq(D), _const_spec((1, D)), _const_spec((D, AB_COLS)), tab, tab,
                  _const_spec((GDN_CONV, 3 * W))],
        out_specs=[seq(3 * W), seq(5 * W + BLK)],
        out_shape=[jax.ShapeDtypeStruct((B, Lp, 3 * W), BF16), jax.ShapeDtypeStruct((B, Lp, 5 * W + BLK), F32)],
        scratch_shapes=[pltpu.VMEM((tm, D), BF16), pltpu.VMEM((CONV_HALO, 3 * W), F32)],
        compiler_params=_params(2, parallel=False),
        name="in_proj_ab",
    )(h.reshape(B, Lp, D), g.reshape(1, D), _ab_weight(w_in).astype(BF16), cos, sin, conv_w.astype(F32))


def _outproj_kernel(a_ref, b_ref, h_ref, g_ref, wa_ref, wb_ref, o_ref):
    y = _dot(a_ref[...].astype(BF16), wa_ref[...]) + _dot(b_ref[...].astype(BF16), wb_ref[...])
    o_ref[...] = h_ref[...] + _rms(y, g_ref[...])


def _outproj(a, b, h, g, w_out, *, tm=512):
    T, D = h.shape
    Ka, Kb = a.shape[1], b.shape[1]
    tm = _row_tile(T, tm)
    w = w_out.astype(BF16)
    row = lambda n: pl.BlockSpec((tm, n), lambda i: (i, 0))
    return pl.pallas_call(
        _outproj_kernel,
        grid=(T // tm,),
        in_specs=[row(Ka), row(Kb), row(D), _const_spec((1, D)), _const_spec((Ka, D)), _const_spec((Kb, D))],
        out_specs=row(D),
        out_shape=jax.ShapeDtypeStruct((T, D), F32),
        compiler_params=_params(1),
        name="out_proj",
    )(a, b, h, g.reshape(1, D), w[:Ka], w[Ka:])


def _ret_kernel(q_ref, k_ref, v_ref, gate_ref, o_ref, s_ref, dec_ref, zx_ref):
    nbatch, C = q_ref.shape[0], q_ref.shape[1]
    n = pl.program_id(0)
    log_gamma = [math.log1p(-2.0 ** (-5.0 - hd)) for hd in range(RET_HEADS)]

    @pl.when(n == 0)
    def _init():
        s_ref[...] = jnp.zeros_like(s_ref)
        diff = (lax.broadcasted_iota(jnp.int32, (C, C), 0) - lax.broadcasted_iota(jnp.int32, (C, C), 1)).astype(F32)
        idx = lax.broadcasted_iota(jnp.int32, (C, RET_DK), 0).astype(F32)
        for hd, lg in enumerate(log_gamma):
            dec_ref[hd] = jnp.where(diff >= 0, jnp.exp(jnp.maximum(diff, 0.0) * lg), 0.0)
            zx_ref[0, hd] = jnp.exp((C - 1.0 - idx) * lg)
            zx_ref[1, hd] = jnp.exp((idx + 1.0) * lg)

    chains = [(b, hd) for b in range(nbatch) for hd in range(RET_HEADS)]
    sl = lambda hd: slice(hd * RET_DK, (hd + 1) * RET_DK)
    q = [q_ref[b, :, sl(hd)] for b, hd in chains]
    k = [k_ref[b, :, sl(hd)] for b, hd in chains]
    v = [v_ref[b, :, sl(hd)] for b, hd in chains]
    scores = [_dot_nt(qq, kk) * dec_ref[hd] for qq, kk, (_, hd) in zip(q, k, chains)]
    intra = [_dot(sc.astype(BF16), vv) for sc, vv in zip(scores, v)]
    state = [s_ref[i] for i in range(len(chains))]
    cross = [_dot(qq, st.astype(BF16)) * zx_ref[1, hd] for qq, st, (_, hd) in zip(q, state, chains)]
    kv = [_dot_tn(kk, (vv.astype(F32) * zx_ref[0, hd]).astype(BF16)) for kk, vv, (_, hd) in zip(k, v, chains)]
    for i, (b, hd) in enumerate(chains):
        s_ref[i] = state[i] * math.exp(C * log_gamma[hd]) + kv[i]
        o = intra[i] + cross[i]
        mu = jnp.mean(o, axis=-1, keepdims=True)
        var = jnp.mean(jnp.square(o - mu), axis=-1, keepdims=True)
        o_ref[b, :, sl(hd)] = (o - mu) * lax.rsqrt(var + NORM_EPS) * gate_ref[b, :, sl(hd)]


def _retention(p16, p32, B, Lp):
    nb = Lp // BLK
    col = lambda j: pl.BlockSpec((B, BLK, RET_W), lambda n: (0, n, j))
    out = pl.pallas_call(
        _ret_kernel,
        grid=(nb,),
        in_specs=[col(0), col(1), col(2), col(0)],
        out_specs=pl.BlockSpec((B, BLK, RET_W), lambda n: (0, n, 0)),
        out_shape=jax.ShapeDtypeStruct((B, Lp, RET_W), F32),
        scratch_shapes=[pltpu.VMEM((B * RET_HEADS, RET_DK, RET_DV), F32),
                        pltpu.VMEM((RET_HEADS, BLK, BLK), F32),
                        pltpu.VMEM((2, RET_HEADS, BLK, RET_DK), F32)],
        compiler_params=_params(1, parallel=False),
        name="retention",
    )(p16, p16, p16, p32)
    return out.reshape(B * Lp, RET_W)


def _unit_lower_inverses(a_mats, eye, same_blk):
    d = [jnp.where(same_blk, a, 0.0) for a in a_mats]
    e = [(a - dd).astype(BF16) for a, dd in zip(a_mats, d)]
    x = [-dd for dd in d]
    t = [eye + xx for xx in x]
    for _ in range(3):
        xb = [xx.astype(BF16) for xx in x]
        x = [_dot(xx, xx) for xx in xb]
        t = [tt + _dot(tt.astype(BF16), xx.astype(BF16)) for tt, xx in zip(t, x)]
    tb = [tt.astype(BF16) for tt in t]
    f = [_dot(tt, ee) for tt, ee in zip(tb, e)]
    p = [eye - ff for ff in f]
    for _ in range(2):
        fb = [ff.astype(BF16) for ff in f]
        f = [_dot(ff, ff) for ff in fb]
        p = [pp + _dot(pp.astype(BF16), ff.astype(BF16)) for pp, ff in zip(p, f)]
    return [pp.astype(BF16) for pp in p], tb


def _gdn_kernel(q_ref, k_ref, v_ref, z_ref, gate_ref, alog_ref, dtb_ref, onorm_ref, o_ref, s_ref):
    nbatch, C = q_ref.shape[0], q_ref.shape[1]
    n = pl.program_id(0)

    @pl.when(n == 0)
    def _init():
        s_ref[...] = jnp.zeros_like(s_ref)

    row = lax.broadcasted_iota(jnp.int32, (C, C), 0)
    col = lax.broadcasted_iota(jnp.int32, (C, C), 1)
    incl = row >= col
    strict = row > col
    same_blk = (row // SOLVE_BLK) == (col // SOLVE_BLK)
    eye = jnp.where(row == col, 1.0, 0.0).astype(F32)
    tri_incl = jnp.where(incl, 1.0, 0.0).astype(F32)
    tri_upper = jnp.where(row <= col, 1.0, 0.0).astype(F32)

    gates = [gate_ref[b] for b in range(nbatch)]
    beta_all = [jax.nn.sigmoid(gt) for gt in gates]
    g_all = [-jnp.exp(alog_ref[...]) * jax.nn.softplus(gt + dtb_ref[...]) for gt in gates]
    gcum_all = [_dot3(tri_incl, g) for g in g_all]
    gcum_t = [_dot3(g, tri_upper, _dot_tn) for g in g_all]

    chains = [(b, hd) for b in range(nbatch) for hd in range(GDN_HEADS)]
    head = lambda ref, b, hd: ref[b, :, hd * GDN_DK:(hd + 1) * GDN_DK]
    q = [head(q_ref, b, hd) for b, hd in chains]
    k = [head(k_ref, b, hd) for b, hd in chains]
    v = [head(v_ref, b, hd) for b, hd in chains]
    beta = [beta_all[b][:, hd:hd + 1] for b, hd in chains]
    gcum = [gcum_all[b][:, GDN_HEADS + hd:GDN_HEADS + hd + 1] for b, hd in chains]
    gcum_row = [gcum_t[b][GDN_HEADS + hd:GDN_HEADS + hd + 1, :] for b, hd in chains]
    decay = [jnp.where(incl, jnp.exp(jnp.where(incl, gc - gr, 0.0)), 0.0) for gc, gr in zip(gcum, gcum_row)]
    k_beta = [kk * bb for kk, bb in zip(k, beta)]
    kb16 = [kk.astype(BF16) for kk in k]
    a_mat = [jnp.where(strict, _dot_nt(kbt.astype(BF16), k16) * dc, 0.0) for kbt, k16, dc in zip(k_beta, kb16, decay)]
    p_mat, t_inv = _unit_lower_inverses(a_mat, eye, same_blk)
    e_gcum = [jnp.exp(gc) for gc in gcum]
    rhs = [jnp.concatenate([vv * bb, kbt * eg], axis=-1) for vv, bb, kbt, eg in zip(v, beta, k_beta, e_gcum)]
    sol = [_dot(tt, rr.astype(BF16)) for tt, rr in zip(t_inv, rhs)]
    sol = [_dot(pp, ss.astype(BF16)) for pp, ss in zip(p_mat, sol)]
    qk = [jnp.where(incl, _dot_nt(qq.astype(BF16), k16) * dc, 0.0) for qq, k16, dc in zip(q, kb16, decay)]
    g_last = [gc[C - 1:C, :] for gc in gcum]
    k_tail = [(kk * jnp.exp(gl - gc)).astype(BF16) for kk, gl, gc in zip(k, g_last, gcum)]
    q_dec = [(qq * eg).astype(BF16) for qq, eg in zip(q, e_gcum)]

    state = [s_ref[i] for i in range(len(chains))]
    state_b = [st.astype(BF16) for st in state]
    v_new = [ss[:, :GDN_DV] - _dot(ss[:, GDN_DV:].astype(BF16), sb) for ss, sb in zip(sol, state_b)]
    v_new_b = [vn.astype(BF16) for vn in v_new]
    out = [_dot(qd, sb) + _dot(qkm.astype(BF16), vn) for qd, sb, qkm, vn in zip(q_dec, state_b, qk, v_new_b)]
    for i, (b, hd) in enumerate(chains):
        s_ref[i] = state[i] * jnp.exp(g_last[i]) + _dot_tn(k_tail[i], v_new_b[i])
        sl = slice(hd * GDN_DV, (hd + 1) * GDN_DV)
        o_ref[b, :, sl] = _rms(out[i], onorm_ref[...]) * z_ref[b, :, sl]


def _gdn(p32, a_log, dt_bias, out_norm, B, Lp):
    C = BLK
    nc = Lp // C
    col = lambda j: pl.BlockSpec((B, C, RET_W), lambda n: (0, n, j))
    gate_lanes = jnp.zeros((1, BLK), F32)
    alog = gate_lanes.at[0, GDN_HEADS:2 * GDN_HEADS].set(a_log.astype(F32))
    dtb = gate_lanes.at[0, GDN_HEADS:2 * GDN_HEADS].set(dt_bias.astype(F32))
    out = pl.pallas_call(
        _gdn_kernel,
        grid=(nc,),
        in_specs=[col(1), col(2), col(3), col(4),
                  pl.BlockSpec((B, C, BLK), lambda n: (0, n, 5 * RET_W // BLK)),
                  _const_spec((1, BLK)), _const_spec((1, BLK)), _const_spec((1, GDN_DV))],
        out_specs=pl.BlockSpec((B, C, RET_W), lambda n: (0, n, 0)),
        out_shape=jax.ShapeDtypeStruct((B, Lp, RET_W), F32),
        scratch_shapes=[pltpu.VMEM((B * GDN_HEADS, GDN_DK, GDN_DV), F32)],
        compiler_params=_params(1, parallel=False),
        name="gated_deltanet",
    )(p32, p32, p32, p32, p32, alog, dtb, out_norm.astype(F32).reshape(1, GDN_DV))
    return out.reshape(B * Lp, RET_W)


def _swa_head_perm():
    G = SWA_HEADS // SWA_KV_HEADS
    heads = [kv * G + b for b in range(G) for kv in range(SWA_KV_HEADS)]
    return np.concatenate([np.arange(h * SWA_DH, (h + 1) * SWA_DH) for h in heads])


def _swa_kernel(q_ref, kc_ref, vc_ref, kp_ref, vp_ref, km_ref, vm_ref, sink_ref, o_ref, *, S):
    n = pl.program_id(1)
    G = SWA_HEADS // SWA_KV_HEADS
    R = G * BLK
    r = lax.broadcasted_iota(jnp.int32, (R, BLK), 0) & (BLK - 1)
    col = lax.broadcasted_iota(jnp.int32, (R, BLK), 1)
    lower = col <= r
    upper = col > r
    is_meta = col >= PAD
    halves = [col < SWA_DH, col >= SWA_DH]
    k_meta, v_meta = km_ref[...], vm_ref[...]
    for s in range(S):
        blk = n * S + s
        rows = slice(s * BLK, (s + 1) * BLK)
        k_cur, v_cur = kc_ref[rows, :], vc_ref[rows, :]
        if s == 0:
            k_prev, v_prev = kp_ref[...], vp_ref[...]
        else:
            k_prev, v_prev = kc_ref[(s - 1) * BLK:s * BLK, :], vc_ref[(s - 1) * BLK:s * BLK, :]
        cur_ok = lower & (blk >= 1)
        prev_ok = upper & (blk >= 2)
        meta_ok = is_meta & ((blk >= 1) | lower)
        q_st = jnp.concatenate([q_ref[rows, b * BLK:(b + 1) * BLK] for b in range(G)], axis=0) * SWA_DH ** -0.5
        outs = []
        for kv in range(SWA_KV_HEADS):
            q = jnp.where(halves[kv], q_st, 0)
            s_cur = jnp.where(cur_ok, _dot_nt(q, k_cur), NEG_BIG)
            s_prev = jnp.where(prev_ok, _dot_nt(q, k_prev), NEG_BIG)
            s_meta = jnp.where(meta_ok, _dot_nt(q, k_meta), NEG_BIG)
            sink = jnp.concatenate([jnp.full((BLK, 1), sink_ref[kv * G + b], F32) for b in range(G)], axis=0)
            m = jnp.maximum(jnp.max(jnp.maximum(jnp.maximum(s_cur, s_prev), s_meta), axis=-1, keepdims=True), sink)
            p_cur = jnp.exp(s_cur - m)
            p_prev = jnp.exp(s_prev - m)
            p_meta = jnp.exp(s_meta - m)
            denom = jnp.sum(p_cur + p_prev + p_meta, axis=-1, keepdims=True) + jnp.exp(sink - m)
            o = (_dot(p_cur.astype(BF16), v_cur) + _dot(p_prev.astype(BF16), v_prev)
                 + _dot(p_meta.astype(BF16), v_meta))
            outs.append(o / denom)
        o = jnp.where(halves[0], outs[0], outs[1])
        for b in range(G):
            o_ref[rows, b * BLK:(b + 1) * BLK] = o[b * BLK:(b + 1) * BLK, :]


def _swa(proj, sinks, B, Lp):
    T = proj.shape[0]
    nb = Lp // BLK
    S = _blocks_per_step(nb, (5,))
    ns = nb // S
    k_col, v_col = RET_W // BLK, RET_W // BLK + 1
    cur = lambda j: pl.BlockSpec((S * BLK, BLK), lambda b, n: (b * ns + n, j))
    prev = lambda j: pl.BlockSpec((BLK, BLK), lambda b, n: (b * nb + jnp.maximum(n * S - 1, 0), j))
    first = lambda j: pl.BlockSpec((BLK, BLK), lambda b, n: (b * nb, j))
    return pl.pallas_call(
        functools.partial(_swa_kernel, S=S),
        grid=(B, ns),
        in_specs=[pl.BlockSpec((S * BLK, RET_W), lambda b, n: (b * ns + n, 0)),
                  cur(k_col), cur(v_col), prev(k_col), prev(v_col), first(k_col), first(v_col),
                  pl.BlockSpec(memory_space=pltpu.SMEM)],
        out_specs=pl.BlockSpec((S * BLK, RET_W), lambda b, n: (b * ns + n, 0)),
        out_shape=jax.ShapeDtypeStruct((T, RET_W), F32),
        compiler_params=_params(2),
        name="swa_sink",
    )(proj, proj, proj, proj, proj, proj, proj, sinks.astype(F32))


def _sb_kernel(q_ref, k_ref, v_ref, o_ref, acc_ref, run_ref, *, S):
    n = pl.program_id(2)
    heads = BLK // SB_DH
    row = lax.broadcasted_iota(jnp.int32, (BLK, BLK), 0)
    col = lax.broadcasted_iota(jnp.int32, (BLK, BLK), 1)
    suffix = jnp.where(row > col, 1.0, 0.0).astype(BF16)
    ones = jnp.ones((BLK, BLK), BF16)
    sum_rhs = jnp.concatenate([jnp.concatenate([suffix, ones], axis=1)] * 2, axis=0)
    lane = lax.broadcasted_iota(jnp.int32, (S * BLK, BLK), 1)
    q_all = q_ref[...] * SB_DH ** -0.5
    q_head = [jnp.where((lane >= hh * SB_DH) & (lane < (hh + 1) * SB_DH), q_all, 0).reshape(S, BLK, BLK)
              for hh in range(heads)]

    def diagonal(hh, k, v, valid, run_in):
        z = jnp.einsum("sqd,skd->sqk", q_head[hh], k, preferred_element_type=F32)
        softplus_neg = jnp.log(1.0 + jnp.exp(-jnp.abs(z)))
        log_beta = jnp.minimum(z, 0.0) - softplus_neg
        log_1m = log_beta - z
        if valid is not None:
            log_1m = jnp.where(valid, log_1m, 0.0)
        hi, lo = _split(log_1m)
        sums = _dot(jnp.concatenate([hi, lo], axis=-1).reshape(S * BLK, 2 * BLK), sum_rhs)
        sums = sums.reshape(S, BLK, 2 * BLK)
        log_stick = sums[..., :BLK] if run_in is None else sums[..., :BLK] + run_in
        a = jnp.exp(log_beta + log_stick)
        if valid is not None:
            a = jnp.where(valid, a, 0.0)
        av = jnp.einsum("sqk,skd->sqd", a.astype(BF16), v, preferred_element_type=F32)
        run = sums[..., BLK:] if run_in is None else run_in + sums[..., BLK:]
        return av, run

    def keep_going(runs):
        top = jnp.max(runs[0])
        for run in runs[1:]:
            top = jnp.maximum(top, jnp.max(run))
        return (top >= SB_SKIP).astype(jnp.int32)

    def slab(ref, first_blk):
        at = pl.multiple_of(first_blk * BLK, BLK)
        return ref[pl.ds(at, S * BLK), :].reshape(S, BLK, BLK)

    def first_two(at_sequence_start):
        k_diag, v_diag = slab(k_ref, n * S), slab(v_ref, n * S)
        at = pl.multiple_of(jnp.maximum(n * S - 1, 0) * BLK, BLK)
        k_prev = jnp.concatenate([k_ref[pl.ds(at, BLK), :][None], k_diag[:-1]], axis=0)
        v_prev = jnp.concatenate([v_ref[pl.ds(at, BLK), :][None], v_diag[:-1]], axis=0)
        if at_sequence_start:
            diag_ok = jnp.stack([(col < row) & ((n * S + s) * BLK + col >= PAD) for s in range(S)])
            prev_ok = jnp.stack([(n * S + s - 1) * BLK + col >= PAD for s in range(S)])
        else:
            diag_ok = jnp.broadcast_to(col < row, (S, BLK, BLK))
            prev_ok = None
        first = [diagonal(hh, k_diag, v_diag, diag_ok, None) for hh in range(heads)]
        second = [diagonal(hh, k_prev, v_prev, prev_ok, first[hh][1]) for hh in range(heads)]
        for hh in range(heads):
            acc_ref[hh] = first[hh][0] + second[hh][0]
            run_ref[hh] = second[hh][1]
        return keep_going([run for _, run in second])

    go = lax.cond(n * S <= 1, lambda: first_two(True), lambda: first_two(False))

    def cond(carry):
        d, go = carry
        return (d <= n * S + S - 1) & (go > 0)

    def body(carry):
        d, _ = carry
        first_blk = n * S - d

        def further(k, v, valid):
            runs = []
            for hh in range(heads):
                av, run = diagonal(hh, k, v, valid, run_ref[hh])
                acc_ref[hh] += av
                run_ref[hh] = run
                runs.append(run)
            return keep_going(runs)

        def interior():
            return further(slab(k_ref, first_blk), slab(v_ref, first_blk), None)

        def edge():
            ks, vs, valids = [], [], []
            for s in range(S):
                jb = first_blk + s
                at = pl.multiple_of(jnp.maximum(jb, 0) * BLK, BLK)
                ks.append(k_ref[pl.ds(at, BLK), :])
                vs.append(v_ref[pl.ds(at, BLK), :])
                valids.append(jb * BLK + col >= PAD)
            return further(jnp.stack(ks), jnp.stack(vs), jnp.stack(valids))

        return d + 1, lax.cond(first_blk >= 1, interior, edge)

    lax.while_loop(cond, body, (jnp.int32(2), go))
    o = acc_ref[0]
    for hh in range(1, heads):
        o = jnp.where(lane.reshape(S, BLK, BLK) >= hh * SB_DH, acc_ref[hh], o)
    o_ref[...] = o.reshape(S * BLK, BLK)


def _stick_breaking(proj, B, Lp):
    T = proj.shape[0]
    nb = Lp // BLK
    S = _blocks_per_step(nb, (5,))
    ns = nb // S
    pairs = SB_HEADS * SB_DH // BLK
    heads = BLK // SB_DH
    q0 = (SWA_HEADS + 2 * SWA_KV_HEADS) * SWA_DH // BLK
    k0, v0 = q0 + pairs, q0 + 2 * pairs
    proj3 = proj.reshape(B, Lp, proj.shape[1])
    seq = lambda c0: pl.BlockSpec((None, Lp, BLK), lambda b, p, n: (b, 0, c0 + p))
    return pl.pallas_call(
        functools.partial(_sb_kernel, S=S),
        grid=(B, pairs, ns),
        in_specs=[pl.BlockSpec((S * BLK, BLK), lambda b, p, n: (b * ns + n, q0 + p)), seq(k0), seq(v0)],
        out_specs=pl.BlockSpec((S * BLK, BLK), lambda b, p, n: (b * ns + n, p)),
        out_shape=jax.ShapeDtypeStruct((T, RET_W), F32),
        scratch_shapes=[pltpu.VMEM((heads, S, BLK, BLK), F32), pltpu.VMEM((heads, S, BLK, BLK), F32)],
        compiler_params=_params(3),
        name="stick_breaking",
    )(proj, proj3, proj3)


def _rotation_tables(Lp):
    half = RET_DK // 2
    inv_freq = 1.0 / (10000.0 ** jnp.linspace(0.0, 1.0, half, dtype=F32))
    pos = jnp.arange(Lp, dtype=F32) - float(PAD)
    ang = pos[:, None] * inv_freq[None, :]
    cos, sin = jnp.cos(ang), jnp.sin(ang)
    return jnp.concatenate([cos, cos], axis=1), jnp.concatenate([-sin, sin], axis=1)


def _ab_weight(w_in):
    D = w_in.shape[0]
    perm = np.concatenate([np.arange(0, RET_DK, 2), np.arange(1, RET_DK, 2)])
    qk_perm = np.concatenate([h * RET_DK + perm for h in range(RET_HEADS)])
    cols = np.concatenate([qk_perm, RET_W + qk_perm, np.arange(2 * RET_W, w_in.shape[1])])
    return jnp.concatenate([w_in[:, cols], jnp.zeros((D, AB_COLS - w_in.shape[1]), w_in.dtype)], axis=1)


def kernel(x, meta_tokens, norm_gains, ffn_w_gate, ffn_w_up, ffn_w_down, ab_w_in, ab_conv_w, ab_a_log, ab_dt_bias, ab_out_norm, ab_w_out, cd_w_in, cd_sinks, cd_w_out):
    B, S, D = x.shape
    Lp = S + BLK
    depth = norm_gains.shape[0]
    meta = jnp.broadcast_to(meta_tokens[None].astype(x.dtype), (B, N_META, D))
    head_blocks = jnp.concatenate([jnp.zeros((B, PAD, D), x.dtype), meta], axis=1).reshape(B * BLK, D)
    tm = _row_tile(B * Lp, 512)
    tm_tok = _row_tile(S, 512)
    token_rows = _token_rows(tm_tok, S, Lp, D)
    cos, sin = _rotation_tables(Lp)
    swa_perm = _swa_head_perm()
    swa_w = SWA_HEADS * SWA_DH
    for i in range(depth):
        g = norm_gains[i]
        j = i // 2
        ffn1 = (g[0], g[1], ffn_w_gate[i, 0], ffn_w_up[i, 0], ffn_w_down[i, 0])
        if i == 0:
            h = _ffn(x.reshape(B * S, D), *ffn1, tm=tm_tok, dst_spec=token_rows, out_rows=B * Lp)
            h = _ffn(head_blocks, *ffn1, tm=BLK, dst_spec=pl.BlockSpec((BLK, D), lambda b: (b * (Lp // BLK), 0)),
                     out_rows=B * Lp, into=h)
        else:
            h = _ffn(h, *ffn1, tm=tm)
        if i % 2 == 0:
            p16, p32 = _proj_ab(h, g[2], ab_w_in[j], ab_conv_w[j], cos, sin, B, Lp)
            mix_a = _retention(p16, p32, B, Lp)
            mix_b = _gdn(p32, ab_a_log[j], ab_dt_bias[j], ab_out_norm[j], B, Lp)
            h = _outproj(mix_a, mix_b, h, g[3], ab_w_out[j])
        else:
            w_in = jnp.concatenate([cd_w_in[j][:, swa_perm], cd_w_in[j][:, swa_w:]], axis=1)
            w_out = jnp.concatenate([cd_w_out[j][swa_perm], cd_w_out[j][swa_w:]], axis=0)
            proj = _proj(h, g[2], w_in, BF16)
            mix_a = _swa(proj, cd_sinks[j], B, Lp)
            mix_b = _stick_breaking(proj, B, Lp)
            h = _outproj(mix_a, mix_b, h, g[3], w_out)
        ffn2 = (g[4], g[5], ffn_w_gate[i, 1], ffn_w_up[i, 1], ffn_w_down[i, 1])
        if i == depth - 1:
            h = _ffn(h, *ffn2, tm=tm_tok, src_spec=token_rows, n_tiles=B * S // tm_tok, out_rows=B * S)
        else:
            h = _ffn(h, *ffn2, tm=tm)
    return h.reshape(B, S, D)
```

```python
import functools
import math

import numpy as np
import jax
import jax.numpy as jnp
from jax import lax
from jax.experimental import pallas as pl
from jax.experimental.pallas import tpu as pltpu

F32 = jnp.float32
BF16 = jnp.bfloat16

N_META = 16
NORM_EPS = 1e-6
BLK = 128
PAD = BLK - N_META

RET_HEADS, RET_DK, RET_DV = 4, 128, 128
GDN_HEADS, GDN_DK, GDN_DV, GDN_CONV = 4, 128, 128, 4
SWA_HEADS, SWA_KV_HEADS, SWA_DH = 8, 2, 64
SB_HEADS, SB_DH = 8, 64
SOLVE_BLK = 16

RET_W = RET_HEADS * RET_DK
AB_GATE_COL = 4096
AB_COLS = AB_GATE_COL + BLK
CONV_HALO = 8

VMEM_LIMIT = 56 * 1024 * 1024
NEG_BIG = -1e30
SB_SKIP = -87.5
SB_LEAD = 3


def _rms(x, g):
    return x * lax.rsqrt(jnp.mean(x * x, axis=-1, keepdims=True) + NORM_EPS) * g


def _silu(x):
    return x * jax.nn.sigmoid(x)


def _dot(a, b):
    return jnp.dot(a, b, preferred_element_type=F32)


def _dot_nt(a, b):
    return lax.dot_general(a, b, (((1,), (1,)), ((), ())), preferred_element_type=F32)


def _dot_tn(a, b):
    return lax.dot_general(a, b, (((0,), (0,)), ((), ())), preferred_element_type=F32)


def _split(a):
    hi = a.astype(BF16)
    return hi, (a - hi.astype(F32)).astype(BF16)


def _dot3(a, b, dot=_dot):
    ah, al = _split(a)
    bh, bl = _split(b)
    return dot(ah, bh) + (dot(ah, bl) + dot(al, bh))


def _params(n_grid, parallel=True):
    sem = ("parallel",) + ("arbitrary",) * (n_grid - 1) if parallel else ("arbitrary",) * n_grid
    return pltpu.CompilerParams(dimension_semantics=sem, vmem_limit_bytes=VMEM_LIMIT)


def _const_spec(shape):
    nd = len(shape)
    return pl.BlockSpec(shape, lambda *_: (0,) * nd, pipeline_mode=pl.Buffered(1))


def _row_tile(rows, target):
    tile = target
    while rows % tile:
        tile -= BLK
    return tile


def _blocks_per_step(nb, prefs):
    for s in prefs:
        if nb % s == 0:
            return s
    return 1


def _ffn_kernel(h_ref, gpre_ref, gpost_ref, wg_ref, wu_ref, wd_ref, *rest, tf):
    o_ref, xn_ref, act_ref = rest[-3:]
    xn_ref[...] = _rms(h_ref[...], gpre_ref[...]).astype(BF16)
    for c in range(0, wg_ref.shape[1], tf):
        xn = xn_ref[...]
        g = _dot(xn, wg_ref[:, c:c + tf])
        u = _dot(xn, wu_ref[:, c:c + tf])
        act_ref[:, c:c + tf] = (_silu(g) * u).astype(BF16)
    y = _dot(act_ref[...], wd_ref[...])
    o_ref[...] = h_ref[...] + 0.5 * _rms(y, gpost_ref[...])


def _ffn(src, g_pre, g_post, w_gate, w_up, w_down, *, tm, src_spec=None, dst_spec=None, n_tiles=None,
         out_rows=None, into=None, tf=256):
    D = src.shape[1]
    F = w_gate.shape[1]
    row = pl.BlockSpec((tm, D), lambda i: (i, 0))
    operands = [src, g_pre.reshape(1, D), g_post.reshape(1, D),
                w_gate.astype(BF16), w_up.astype(BF16), w_down.astype(BF16)]
    in_specs = [src_spec or row, _const_spec((1, D)), _const_spec((1, D)),
                _const_spec((D, F)), _const_spec((D, F)), _const_spec((F, D))]
    aliases = {}
    if into is not None:
        aliases = {len(operands): 0}
        operands.append(into)
        in_specs.append(pl.BlockSpec(memory_space=pl.ANY))
    return pl.pallas_call(
        functools.partial(_ffn_kernel, tf=tf),
        grid=(n_tiles or src.shape[0] // tm,),
        in_specs=in_specs,
        out_specs=dst_spec or row,
        out_shape=jax.ShapeDtypeStruct((out_rows or src.shape[0], D), F32),
        scratch_shapes=[pltpu.VMEM((tm, D), BF16), pltpu.VMEM((tm, F), BF16)],
        input_output_aliases=aliases,
        compiler_params=_params(1),
        name="ffn",
    )(*operands)


def _token_rows(tm, S, Lp, D):
    per_batch = S // tm
    return pl.BlockSpec((pl.Element(tm), pl.Element(D)),
                        lambda i: (BLK * ((i // per_batch) * (Lp // BLK) + 1 + (i % per_batch) * (tm // BLK)), 0))


def _proj_kernel(h_ref, g_ref, w_ref, o_ref, *, tn):
    xn = _rms(h_ref[...], g_ref[...]).astype(BF16)
    for c in range(0, w_ref.shape[1], tn):
        o_ref[:, c:c + tn] = _dot(xn, w_ref[:, c:c + tn]).astype(o_ref.dtype)


def _proj(h, g, w, out_dtype, *, tm=512, tn=256):
    T, D = h.shape
    N = w.shape[1]
    tm = _row_tile(T, tm)
    return pl.pallas_call(
        functools.partial(_proj_kernel, tn=tn),
        grid=(T // tm,),
        in_specs=[pl.BlockSpec((tm, D), lambda i: (i, 0)), _const_spec((1, D)), _const_spec((D, N))],
        out_specs=pl.BlockSpec((tm, N), lambda i: (i, 0)),
        out_shape=jax.ShapeDtypeStruct((T, N), out_dtype),
        compiler_params=_params(1),
        name="in_proj",
    )(h, g.reshape(1, D), w.astype(BF16))


def _proj_ab_kernel(h_ref, g_ref, w_ref, cos_ref, sin_ref, cw_ref, o16_ref, o32_ref, xn_ref, halo_ref, *, tn):
    rows = h_ref.shape[0]
    W = RET_W

    @pl.when(pl.program_id(1) == 0)
    def _start_of_sequence():
        halo_ref[...] = jnp.zeros_like(halo_ref)

    xn_ref[...] = _rms(h_ref[...], g_ref[...]).astype(BF16)
    cos, sin = cos_ref[...], sin_ref[...]
    rot = lambda t: t * cos + pltpu.roll(t, RET_DK // 2, 1) * sin
    l2n = lambda t: t * lax.rsqrt(jnp.sum(t * t, axis=-1, keepdims=True) + NORM_EPS)
    heavy = list(range(4 * W, 7 * W, tn))
    light = [c for c in range(0, AB_GATE_COL, tn) if c not in heavy]
    order = [c for pair in zip(heavy, light) for c in pair] + light[len(heavy):]
    for c in order:
        pre = _dot(xn_ref[...], w_ref[:, c:c + tn])
        group, off = divmod(c, W)
        if group <= 1:
            scale = 1.0 if group == 0 else RET_DK ** -0.5
            for hs in range(0, tn, RET_DK):
                o16_ref[:, c + hs:c + hs + RET_DK] = (rot(pre[:, hs:hs + RET_DK]) * scale).astype(BF16)
        elif group == 2:
            o16_ref[:, c:c + tn] = pre.astype(BF16)
        elif group == 3:
            o32_ref[:, off:off + tn] = _silu(pre)
        elif group <= 6:
            ch = c - 4 * W
            x_ext = jnp.concatenate([halo_ref[:, ch:ch + tn], pre], axis=0)
            y = cw_ref[GDN_CONV - 1:GDN_CONV, ch:ch + tn] * pre
            for shift in range(1, GDN_CONV):
                tap = GDN_CONV - 1 - shift
                y = y + cw_ref[tap:tap + 1, ch:ch + tn] * pltpu.roll(x_ext, shift, 0)[CONV_HALO:]
            halo_ref[:, ch:ch + tn] = pre[rows - CONV_HALO:rows]
            act = _silu(y)
            if group == 6:
                o32_ref[:, W + ch:W + ch + tn] = act
            else:
                scale = GDN_DK ** -0.5 if group == 4 else 1.0
                for hs in range(0, tn, GDN_DK):
                    o32_ref[:, W + ch + hs:W + ch + hs + GDN_DK] = l2n(act[:, hs:hs + GDN_DK]) * scale
        else:
            o32_ref[:, 4 * W + off:4 * W + off + tn] = _silu(pre)
    o32_ref[:, 5 * W:5 * W + BLK] = _dot(xn_ref[...], w_ref[:, AB_GATE_COL:AB_GATE_COL + BLK])


def _proj_ab(h, g, w_in, conv_w, cos, sin, B, Lp, *, tm=640, tn=256):
    D = h.shape[1]
    tm = _row_tile(Lp, tm)
    W = RET_W
    seq = lambda n: pl.BlockSpec((None, tm, n), lambda b, j: (b, j, 0))
    tab = pl.BlockSpec((tm, RET_DK), lambda b, j: (j, 0))
    return pl.pallas_call(
        functools.partial(_proj_ab_kernel, tn=tn),
        grid=(B, Lp // tm),
        in_specs=[seq(D), _const_spec((1, D)), _const_spec((D, AB_COLS)), tab, tab,
                  _const_spec((GDN_CONV, 3 * W))],
        out_specs=[seq(3 * W), seq(5 * W + BLK)],
        out_shape=[jax.ShapeDtypeStruct((B, Lp, 3 * W), BF16), jax.ShapeDtypeStruct((B, Lp, 5 * W + BLK), F32)],
        scratch_shapes=[pltpu.VMEM((tm, D), BF16), pltpu.VMEM((CONV_HALO, 3 * W), F32)],
        compiler_params=_params(2, parallel=False),
        name="in_proj_ab",
    )(h.reshape(B, Lp, D), g.reshape(1, D), _ab_weight(w_in.astype(BF16)), cos, sin, conv_w.astype(F32))


def _outproj_kernel(a_ref, b_ref, h_ref, g_ref, wa_ref, wb_ref, o_ref):
    y = _dot(a_ref[...], wa_ref[...]) + _dot(b_ref[...], wb_ref[...])
    o_ref[...] = h_ref[...] + _rms(y, g_ref[...])


def _outproj(a, b, h, g, w_out, *, tm=512):
    T, D = h.shape
    Ka, Kb = a.shape[1], b.shape[1]
    tm = _row_tile(T, tm)
    w = w_out.astype(BF16)
    row = lambda n: pl.BlockSpec((tm, n), lambda i: (i, 0))
    return pl.pallas_call(
        _outproj_kernel,
        grid=(T // tm,),
        in_specs=[row(Ka), row(Kb), row(D), _const_spec((1, D)), _const_spec((Ka, D)), _const_spec((Kb, D))],
        out_specs=row(D),
        out_shape=jax.ShapeDtypeStruct((T, D), F32),
        compiler_params=_params(1),
        name="out_proj",
    )(a, b, h, g.reshape(1, D), w[:Ka], w[Ka:])


def _ret_kernel(q_ref, k_ref, v_ref, gate_ref, o_ref, s_ref, dec_ref, zx_ref):
    nbatch, C = q_ref.shape[0], q_ref.shape[1]
    n = pl.program_id(0)
    log_gamma = [math.log1p(-2.0 ** (-5.0 - hd)) for hd in range(RET_HEADS)]

    @pl.when(n == 0)
    def _init():
        s_ref[...] = jnp.zeros_like(s_ref)
        diff = (lax.broadcasted_iota(jnp.int32, (C, C), 0) - lax.broadcasted_iota(jnp.int32, (C, C), 1)).astype(F32)
        idx = lax.broadcasted_iota(jnp.int32, (C, RET_DK), 0).astype(F32)
        for hd, lg in enumerate(log_gamma):
            dec_ref[hd] = jnp.where(diff >= 0, jnp.exp(jnp.maximum(diff, 0.0) * lg), 0.0)
            zx_ref[0, hd] = jnp.exp((C - 1.0 - idx) * lg)
            zx_ref[1, hd] = jnp.exp((idx + 1.0) * lg)

    chains = [(b, hd) for b in range(nbatch) for hd in range(RET_HEADS)]
    sl = lambda hd: slice(hd * RET_DK, (hd + 1) * RET_DK)
    q = [q_ref[b, :, sl(hd)] for b, hd in chains]
    k = [k_ref[b, :, sl(hd)] for b, hd in chains]
    v = [v_ref[b, :, sl(hd)] for b, hd in chains]
    scores = [_dot_nt(qq, kk) * dec_ref[hd] for qq, kk, (_, hd) in zip(q, k, chains)]
    intra = [_dot(sc.astype(BF16), vv) for sc, vv in zip(scores, v)]
    state = [s_ref[i] for i in range(len(chains))]
    cross = [_dot(qq, st.astype(BF16)) * zx_ref[1, hd] for qq, st, (_, hd) in zip(q, state, chains)]
    kv = [_dot_tn(kk, (vv.astype(F32) * zx_ref[0, hd]).astype(BF16)) for kk, vv, (_, hd) in zip(k, v, chains)]
    for i, (b, hd) in enumerate(chains):
        s_ref[i] = state[i] * math.exp(C * log_gamma[hd]) + kv[i]
        o = intra[i] + cross[i]
        mu = jnp.mean(o, axis=-1, keepdims=True)
        var = jnp.mean(jnp.square(o - mu), axis=-1, keepdims=True)
        o_ref[b, :, sl(hd)] = ((o - mu) * lax.rsqrt(var + NORM_EPS) * gate_ref[b, :, sl(hd)]).astype(o_ref.dtype)


def _retention(p16, p32, B, Lp):
    nb = Lp // BLK
    col = lambda j: pl.BlockSpec((B, BLK, RET_W), lambda n: (0, n, j))
    out = pl.pallas_call(
        _ret_kernel,
        grid=(nb,),
        in_specs=[col(0), col(1), col(2), col(0)],
        out_specs=pl.BlockSpec((B, BLK, RET_W), lambda n: (0, n, 0)),
        out_shape=jax.ShapeDtypeStruct((B, Lp, RET_W), BF16),
        scratch_shapes=[pltpu.VMEM((B * RET_HEADS, RET_DK, RET_DV), F32),
                        pltpu.VMEM((RET_HEADS, BLK, BLK), F32),
                        pltpu.VMEM((2, RET_HEADS, BLK, RET_DK), F32)],
        compiler_params=_params(1, parallel=False),
        name="retention",
    )(p16, p16, p16, p32)
    return out.reshape(B * Lp, RET_W)


def _unit_lower_inverses(a_mats, eye, same_blk):
    d = [jnp.where(same_blk, a, 0.0) for a in a_mats]
    e = [(a - dd).astype(BF16) for a, dd in zip(a_mats, d)]
    x = [-dd for dd in d]
    t = _product_of_powers([eye + xx for xx in x], x, 3)
    tb = [tt.astype(BF16) for tt in t]
    f = [-_dot(tt, ee) for tt, ee in zip(tb, e)]
    p = _product_of_powers([eye + ff for ff in f], f, 2)
    return [pp.astype(BF16) for pp in p], tb


def _product_of_powers(prod, x, steps):
    size = x[0].shape[1]
    x = [_dot(xb, xb) for xb in [xx.astype(BF16) for xx in x]]
    for step in range(steps):
        xb = [xx.astype(BF16) for xx in x]
        if step == steps - 1:
            return [pp + _dot(xx, pp.astype(BF16)) for pp, xx in zip(prod, xb)]
        both = [_dot(xx, jnp.concatenate([xx, pp.astype(BF16)], axis=1)) for pp, xx in zip(prod, xb)]
        x = [bb[:, :size] for bb in both]
        prod = [pp + bb[:, size:] for pp, bb in zip(prod, both)]


def _gdn_kernel(q_ref, k_ref, v_ref, z_ref, gate_ref, alog_ref, dtb_ref, onorm_ref, o_ref, s_ref):
    nbatch, C = q_ref.shape[0], q_ref.shape[1]
    n = pl.program_id(0)

    @pl.when(n == 0)
    def _init():
        s_ref[...] = jnp.zeros_like(s_ref)

    row = lax.broadcasted_iota(jnp.int32, (C, C), 0)
    col = lax.broadcasted_iota(jnp.int32, (C, C), 1)
    incl = row >= col
    strict = row > col
    same_blk = (row // SOLVE_BLK) == (col // SOLVE_BLK)
    eye = jnp.where(row == col, 1.0, 0.0).astype(F32)
    tri_incl = jnp.where(incl, 1.0, 0.0).astype(F32)
    tri_upper = jnp.where(row <= col, 1.0, 0.0).astype(F32)

    gates = [gate_ref[b] for b in range(nbatch)]
    beta_all = [jax.nn.sigmoid(gt) for gt in gates]
    g_all = [-jnp.exp(alog_ref[...]) * jax.nn.softplus(gt + dtb_ref[...]) for gt in gates]
    gcum_all = [_dot3(tri_incl, g) for g in g_all]
    gcum_t = [_dot3(g, tri_upper, _dot_tn) for g in g_all]

    chains = [(b, hd) for b in range(nbatch) for hd in range(GDN_HEADS)]
    head = lambda ref, b, hd: ref[b, :, hd * GDN_DK:(hd + 1) * GDN_DK]
    q = [head(q_ref, b, hd) for b, hd in chains]
    k = [head(k_ref, b, hd) for b, hd in chains]
    v = [head(v_ref, b, hd) for b, hd in chains]
    beta = [beta_all[b][:, hd:hd + 1] for b, hd in chains]
    gcum = [gcum_all[b][:, GDN_HEADS + hd:GDN_HEADS + hd + 1] for b, hd in chains]
    gcum_row = [gcum_t[b][GDN_HEADS + hd:GDN_HEADS + hd + 1, :] for b, hd in chains]
    decay = [jnp.where(incl, jnp.exp(jnp.where(incl, gc - gr, 0.0)), 0.0) for gc, gr in zip(gcum, gcum_row)]
    k_beta = [kk * bb for kk, bb in zip(k, beta)]
    scores = [_dot_nt(jnp.concatenate([kbt.astype(BF16), qq.astype(BF16)], axis=0), kk.astype(BF16))
              for kbt, qq, kk in zip(k_beta, q, k)]
    a_mat = [jnp.where(strict, sc[:C] * dc, 0.0) for sc, dc in zip(scores, decay)]
    p_mat, t_inv = _unit_lower_inverses(a_mat, eye, same_blk)
    e_gcum = [jnp.exp(gc) for gc in gcum]
    rhs = [jnp.concatenate([vv * bb, kbt * eg], axis=-1) for vv, bb, kbt, eg in zip(v, beta, k_beta, e_gcum)]
    sol = [_dot(tt, rr.astype(BF16)) for tt, rr in zip(t_inv, rhs)]
    sol = [_dot(pp, ss.astype(BF16)) for pp, ss in zip(p_mat, sol)]
    qk = [jnp.where(incl, sc[C:] * dc, 0.0) for sc, dc in zip(scores, decay)]
    g_last = [gc[C - 1:C, :] for gc in gcum]
    k_tail = [(kk * jnp.exp(gl - gc)).astype(BF16) for kk, gl, gc in zip(k, g_last, gcum)]
    q_dec = [(qq * eg).astype(BF16) for qq, eg in zip(q, e_gcum)]

    state = [s_ref[i] for i in range(len(chains))]
    state_b = [st.astype(BF16) for st in state]
    v_new = [ss[:, :GDN_DV] - _dot(ss[:, GDN_DV:].astype(BF16), sb) for ss, sb in zip(sol, state_b)]
    v_new_b = [vn.astype(BF16) for vn in v_new]
    out = [_dot(qd, sb) + _dot(qkm.astype(BF16), vn) for qd, sb, qkm, vn in zip(q_dec, state_b, qk, v_new_b)]
    for i, (b, hd) in enumerate(chains):
        s_ref[i] = state[i] * jnp.exp(g_last[i]) + _dot_tn(k_tail[i], v_new_b[i])
        sl = slice(hd * GDN_DV, (hd + 1) * GDN_DV)
        o_ref[b, :, sl] = (_rms(out[i], onorm_ref[...]) * z_ref[b, :, sl]).astype(o_ref.dtype)


def _gdn(p32, a_log, dt_bias, out_norm, B, Lp):
    C = BLK
    nc = Lp // C
    col = lambda j: pl.BlockSpec((B, C, RET_W), lambda n: (0, n, j))
    gate_lanes = jnp.zeros((1, BLK), F32)
    alog = gate_lanes.at[0, GDN_HEADS:2 * GDN_HEADS].set(a_log.astype(F32))
    dtb = gate_lanes.at[0, GDN_HEADS:2 * GDN_HEADS].set(dt_bias.astype(F32))
    out = pl.pallas_call(
        _gdn_kernel,
        grid=(nc,),
        in_specs=[col(1), col(2), col(3), col(4),
                  pl.BlockSpec((B, C, BLK), lambda n: (0, n, 5 * RET_W // BLK)),
                  _const_spec((1, BLK)), _const_spec((1, BLK)), _const_spec((1, GDN_DV))],
        out_specs=pl.BlockSpec((B, C, RET_W), lambda n: (0, n, 0)),
        out_shape=jax.ShapeDtypeStruct((B, Lp, RET_W), BF16),
        scratch_shapes=[pltpu.VMEM((B * GDN_HEADS, GDN_DK, GDN_DV), F32)],
        compiler_params=_params(1, parallel=False),
        name="gated_deltanet",
    )(p32, p32, p32, p32, p32, alog, dtb, out_norm.astype(F32).reshape(1, GDN_DV))
    return out.reshape(B * Lp, RET_W)


def _swa_head_perm():
    G = SWA_HEADS // SWA_KV_HEADS
    heads = [kv * G + b for b in range(G) for kv in range(SWA_KV_HEADS)]
    return np.concatenate([np.arange(h * SWA_DH, (h + 1) * SWA_DH) for h in heads])


def _swa_kernel(q_ref, kc_ref, vc_ref, kp_ref, vp_ref, km_ref, vm_ref, sink_ref, o_ref, *, S):
    n = pl.program_id(1)
    G = SWA_HEADS // SWA_KV_HEADS
    R = G * BLK
    r = lax.broadcasted_iota(jnp.int32, (R, BLK), 0) & (BLK - 1)
    col = lax.broadcasted_iota(jnp.int32, (R, BLK), 1)
    lower = col <= r
    upper = col > r
    is_meta = col >= PAD
    halves = [col < SWA_DH, col >= SWA_DH]
    k_meta, v_meta = km_ref[...], vm_ref[...]
    for s in range(S):
        blk = n * S + s
        rows = slice(s * BLK, (s + 1) * BLK)
        k_cur, v_cur = kc_ref[rows, :], vc_ref[rows, :]
        if s == 0:
            k_prev, v_prev = kp_ref[...], vp_ref[...]
        else:
            k_prev, v_prev = kc_ref[(s - 1) * BLK:s * BLK, :], vc_ref[(s - 1) * BLK:s * BLK, :]
        cur_ok = lower & (blk >= 1)
        prev_ok = upper & (blk >= 2)
        meta_ok = is_meta & ((blk >= 1) | lower)
        q_st = jnp.concatenate([q_ref[rows, b * BLK:(b + 1) * BLK] for b in range(G)], axis=0) * SWA_DH ** -0.5
        outs = []
        for kv in range(SWA_KV_HEADS):
            q = jnp.where(halves[kv], q_st, 0)
            s_cur = jnp.where(cur_ok, _dot_nt(q, k_cur), NEG_BIG)
            s_prev = jnp.where(prev_ok, _dot_nt(q, k_prev), NEG_BIG)
            s_meta = jnp.where(meta_ok, _dot_nt(q, k_meta), NEG_BIG)
            sink = jnp.concatenate([jnp.full((BLK, 1), sink_ref[kv * G + b], F32) for b in range(G)], axis=0)
            m = jnp.maximum(jnp.max(jnp.maximum(jnp.maximum(s_cur, s_prev), s_meta), axis=-1, keepdims=True), sink)
            p_cur = jnp.exp(s_cur - m)
            p_prev = jnp.exp(s_prev - m)
            p_meta = jnp.exp(s_meta - m)
            denom = jnp.sum(p_cur + p_prev + p_meta, axis=-1, keepdims=True) + jnp.exp(sink - m)
            o = (_dot(p_cur.astype(BF16), v_cur) + _dot(p_prev.astype(BF16), v_prev)
                 + _dot(p_meta.astype(BF16), v_meta))
            outs.append(o / denom)
        o = jnp.where(halves[0], outs[0], outs[1])
        for b in range(G):
            o_ref[rows, b * BLK:(b + 1) * BLK] = o[b * BLK:(b + 1) * BLK, :].astype(o_ref.dtype)


def _swa(proj, sinks, B, Lp):
    T = proj.shape[0]
    nb = Lp // BLK
    S = _blocks_per_step(nb, (5,))
    ns = nb // S
    k_col, v_col = RET_W // BLK, RET_W // BLK + 1
    cur = lambda j: pl.BlockSpec((S * BLK, BLK), lambda b, n: (b * ns + n, j))
    prev = lambda j: pl.BlockSpec((BLK, BLK), lambda b, n: (b * nb + jnp.maximum(n * S - 1, 0), j))
    first = lambda j: pl.BlockSpec((BLK, BLK), lambda b, n: (b * nb, j))
    return pl.pallas_call(
        functools.partial(_swa_kernel, S=S),
        grid=(B, ns),
        in_specs=[pl.BlockSpec((S * BLK, RET_W), lambda b, n: (b * ns + n, 0)),
                  cur(k_col), cur(v_col), prev(k_col), prev(v_col), first(k_col), first(v_col),
                  pl.BlockSpec(memory_space=pltpu.SMEM)],
        out_specs=pl.BlockSpec((S * BLK, RET_W), lambda b, n: (b * ns + n, 0)),
        out_shape=jax.ShapeDtypeStruct((T, RET_W), BF16),
        compiler_params=_params(2),
        name="swa_sink",
    )(proj, proj, proj, proj, proj, proj, proj, sinks.astype(F32))


def _sb_kernel(q_ref, k_ref, v_ref, o_ref, acc_ref, run_ref, *, S):
    n = pl.program_id(2)
    heads = BLK // SB_DH
    row = lax.broadcasted_iota(jnp.int32, (BLK, BLK), 0)
    col = lax.broadcasted_iota(jnp.int32, (BLK, BLK), 1)
    suffix = jnp.where(row > col, 1.0, 0.0).astype(BF16)
    ones = jnp.ones((BLK, BLK), BF16)
    sum_rhs = jnp.concatenate([jnp.concatenate([suffix, ones], axis=1)] * 2, axis=0)
    lane = lax.broadcasted_iota(jnp.int32, (S * BLK, BLK), 1)
    q_all = q_ref[...] * SB_DH ** -0.5
    q_head = [jnp.where((lane >= hh * SB_DH) & (lane < (hh + 1) * SB_DH), q_all, 0).reshape(S, BLK, BLK)
              for hh in range(heads)]

    def diagonal(hh, k, v, valid, run_in):
        z = jnp.einsum("sqd,skd->sqk", q_head[hh], k, preferred_element_type=F32)
        softplus_neg = jnp.log(1.0 + jnp.exp(-jnp.abs(z)))
        log_beta = jnp.minimum(z, 0.0) - softplus_neg
        log_1m = log_beta - z
        if valid is not None:
            log_1m = jnp.where(valid, log_1m, 0.0)
        hi, lo = _split(log_1m)
        sums = _dot(jnp.concatenate([hi, lo], axis=-1).reshape(S * BLK, 2 * BLK), sum_rhs)
        sums = sums.reshape(S, BLK, 2 * BLK)
        log_stick = sums[..., :BLK] if run_in is None else sums[..., :BLK] + run_in
        a = jnp.exp(log_beta + log_stick)
        if valid is not None:
            a = jnp.where(valid, a, 0.0)
        av = jnp.einsum("sqk,skd->sqd", a.astype(BF16), v, preferred_element_type=F32)
        run = sums[..., BLK:] if run_in is None else run_in + sums[..., BLK:]
        return av, run

    def keep_going(runs):
        top = jnp.max(runs[0])
        for run in runs[1:]:
            top = jnp.maximum(top, jnp.max(run))
        return (top >= SB_SKIP).astype(jnp.int32)

    def slab(ref, first_blk):
        at = pl.multiple_of(first_blk * BLK, BLK)
        return ref[pl.ds(at, S * BLK), :].reshape(S, BLK, BLK)

    lead = min(SB_LEAD, S)

    def leading(at_sequence_start):
        k_diag, v_diag = slab(k_ref, n * S), slab(v_ref, n * S)
        k_before, v_before = [], []
        for back in range(lead - 1, 0, -1):
            at = pl.multiple_of(jnp.maximum(n * S - back, 0) * BLK, BLK)
            k_before.append(k_ref[pl.ds(at, BLK), :][None])
            v_before.append(v_ref[pl.ds(at, BLK), :][None])
        accs, runs = [None] * heads, [None] * heads
        for d in range(lead):
            k = jnp.concatenate(k_before[lead - 1 - d:] + [k_diag[:S - d]], axis=0)
            v = jnp.concatenate(v_before[lead - 1 - d:] + [v_diag[:S - d]], axis=0)
            if at_sequence_start:
                ok = jnp.stack([(n * S + s - d) * BLK + col >= PAD for s in range(S)])
                ok = ok & (col < row) if d == 0 else ok
            else:
                ok = jnp.broadcast_to(col < row, (S, BLK, BLK)) if d == 0 else None
            for hh in range(heads):
                av, runs[hh] = diagonal(hh, k, v, ok, runs[hh])
                accs[hh] = av if d == 0 else accs[hh] + av
        for hh in range(heads):
            acc_ref[hh] = accs[hh]
            run_ref[hh] = runs[hh]
        return keep_going(runs)

    go = lax.cond(n * S <= lead - 1, lambda: leading(True), lambda: leading(False))

    def cond(carry):
        d, go = carry
        return (d <= n * S + S - 1) & (go > 0)

    def body(carry):
        d, _ = carry
        first_blk = n * S - d

        def further(k, v, valid):
            runs = []
            for hh in range(heads):
                av, run = diagonal(hh, k, v, valid, run_ref[hh])
                acc_ref[hh] += av
                run_ref[hh] = run
                runs.append(run)
            return keep_going(runs)

        def interior():
            return further(slab(k_ref, first_blk), slab(v_ref, first_blk), None)

        def edge():
            ks, vs, valids = [], [], []
            for s in range(S):
                jb = first_blk + s
                at = pl.multiple_of(jnp.maximum(jb, 0) * BLK, BLK)
                ks.append(k_ref[pl.ds(at, BLK), :])
                vs.append(v_ref[pl.ds(at, BLK), :])
                valids.append(jb * BLK + col >= PAD)
            return further(jnp.stack(ks), jnp.stack(vs), jnp.stack(valids))

        return d + 1, lax.cond(first_blk >= 1, interior, edge)

    lax.while_loop(cond, body, (jnp.int32(lead), go))
    o = acc_ref[0]
    for hh in range(1, heads):
        o = jnp.where(lane.reshape(S, BLK, BLK) >= hh * SB_DH, acc_ref[hh], o)
    o_ref[...] = o.reshape(S * BLK, BLK).astype(o_ref.dtype)


def _stick_breaking(proj, B, Lp):
    T = proj.shape[0]
    nb = Lp // BLK
    S = _blocks_per_step(nb, (5,))
    ns = nb // S
    pairs = SB_HEADS * SB_DH // BLK
    heads = BLK // SB_DH
    q0 = (SWA_HEADS + 2 * SWA_KV_HEADS) * SWA_DH // BLK
    k0, v0 = q0 + pairs, q0 + 2 * pairs
    proj3 = proj.reshape(B, Lp, proj.shape[1])
    seq = lambda c0: pl.BlockSpec((None, Lp, BLK), lambda b, p, n: (b, 0, c0 + p))
    return pl.pallas_call(
        functools.partial(_sb_kernel, S=S),
        grid=(B, pairs, ns),
        in_specs=[pl.BlockSpec((S * BLK, BLK), lambda b, p, n: (b * ns + n, q0 + p)), seq(k0), seq(v0)],
        out_specs=pl.BlockSpec((S * BLK, BLK), lambda b, p, n: (b * ns + n, p)),
        out_shape=jax.ShapeDtypeStruct((T, RET_W), BF16),
        scratch_shapes=[pltpu.VMEM((heads, S, BLK, BLK), F32), pltpu.VMEM((heads, S, BLK, BLK), F32)],
        compiler_params=_params(3),
        name="stick_breaking",
    )(proj, proj3, proj3)


def _rotation_tables(Lp):
    half = RET_DK // 2
    inv_freq = 1.0 / (10000.0 ** jnp.linspace(0.0, 1.0, half, dtype=F32))
    pos = jnp.arange(Lp, dtype=F32) - float(PAD)
    ang = pos[:, None] * inv_freq[None, :]
    cos, sin = jnp.cos(ang), jnp.sin(ang)
    return jnp.concatenate([cos, cos], axis=1), jnp.concatenate([-sin, sin], axis=1)


def _ab_weight(w_in):
    D = w_in.shape[0]
    perm = np.concatenate([np.arange(0, RET_DK, 2), np.arange(1, RET_DK, 2)])
    qk_perm = np.concatenate([h * RET_DK + perm for h in range(RET_HEADS)])
    cols = np.concatenate([qk_perm, RET_W + qk_perm, np.arange(2 * RET_W, w_in.shape[1])])
    return jnp.concatenate([w_in[:, cols], jnp.zeros((D, AB_COLS - w_in.shape[1]), w_in.dtype)], axis=1)


def kernel(x, meta_tokens, norm_gains, ffn_w_gate, ffn_w_up, ffn_w_down, ab_w_in, ab_conv_w, ab_a_log, ab_dt_bias, ab_out_norm, ab_w_out, cd_w_in, cd_sinks, cd_w_out):
    B, S, D = x.shape
    Lp = S + BLK
    depth = norm_gains.shape[0]
    meta = jnp.broadcast_to(meta_tokens[None].astype(x.dtype), (B, N_META, D))
    head_blocks = jnp.concatenate([jnp.zeros((B, PAD, D), x.dtype), meta], axis=1).reshape(B * BLK, D)
    tm = _row_tile(B * Lp, 512)
    tm_tok = _row_tile(S, 512)
    token_rows = _token_rows(tm_tok, S, Lp, D)
    cos, sin = _rotation_tables(Lp)
    swa_perm = _swa_head_perm()
    swa_w = SWA_HEADS * SWA_DH
    for i in range(depth):
        g = norm_gains[i]
        j = i // 2
        ffn1 = (g[0], g[1], ffn_w_gate[i, 0], ffn_w_up[i, 0], ffn_w_down[i, 0])
        if i == 0:
            h = _ffn(x.reshape(B * S, D), *ffn1, tm=tm_tok, dst_spec=token_rows, out_rows=B * Lp)
            h = _ffn(head_blocks, *ffn1, tm=BLK, dst_spec=pl.BlockSpec((BLK, D), lambda b: (b * (Lp // BLK), 0)),
                     out_rows=B * Lp, into=h)
        else:
            h = _ffn(h, *ffn1, tm=tm)
        if i % 2 == 0:
            p16, p32 = _proj_ab(h, g[2], ab_w_in[j], ab_conv_w[j], cos, sin, B, Lp)
            mix_a = _retention(p16, p32, B, Lp)
            mix_b = _gdn(p32, ab_a_log[j], ab_dt_bias[j], ab_out_norm[j], B, Lp)
            h = _outproj(mix_a, mix_b, h, g[3], ab_w_out[j])
        else:
            w_in, w_out = cd_w_in[j].astype(BF16), cd_w_out[j].astype(BF16)
            w_in = jnp.concatenate([w_in[:, swa_perm], w_in[:, swa_w:]], axis=1)
            w_out = jnp.concatenate([w_out[swa_perm], w_out[swa_w:]], axis=0)
            proj = _proj(h, g[2], w_in, BF16, tm=640)
            mix_a = _swa(proj, cd_sinks[j], B, Lp)
            mix_b = _stick_breaking(proj, B, Lp)
            h = _outproj(mix_a, mix_b, h, g[3], w_out)
        ffn2 = (g[4], g[5], ffn_w_gate[i, 1], ffn_w_up[i, 1], ffn_w_down[i, 1])
        if i == depth - 1:
            h = _ffn(h, *ffn2, tm=tm_tok, src_spec=token_rows, n_tiles=B * S // tm_tok, out_rows=B * S)
        else:
            h = _ffn(h, *ffn2, tm=tm)
    return h.reshape(B, S, D)
```

```python
import functools
import math

import numpy as np
import jax
import jax.numpy as jnp
from jax import lax
from jax.experimental import pallas as pl
from jax.experimental.pallas import tpu as pltpu

F32 = jnp.float32
BF16 = jnp.bfloat16

N_META = 16
NORM_EPS = 1e-6
BLK = 128
PAD = BLK - N_META

RET_HEADS, RET_DK, RET_DV = 4, 128, 128
GDN_HEADS, GDN_DK, GDN_DV, GDN_CONV = 4, 128, 128, 4
SWA_HEADS, SWA_KV_HEADS, SWA_DH = 8, 2, 64
SB_HEADS, SB_DH = 8, 64
SOLVE_BLK = 16

RET_W = RET_HEADS * RET_DK
AB_GATE_COL = 4096
AB_COLS = AB_GATE_COL + BLK
CONV_HALO = 8

VMEM_LIMIT = 56 * 1024 * 1024
NEG_BIG = -1e30
SB_SKIP = -87.5
SB_LEAD = 3


def _rms(x, g):
    return x * lax.rsqrt(jnp.mean(x * x, axis=-1, keepdims=True) + NORM_EPS) * g


def _silu(x):
    return x * jax.nn.sigmoid(x)


def _dot(a, b):
    return jnp.dot(a, b, preferred_element_type=F32)


def _dot_nt(a, b):
    return lax.dot_general(a, b, (((1,), (1,)), ((), ())), preferred_element_type=F32)


def _dot_tn(a, b):
    return lax.dot_general(a, b, (((0,), (0,)), ((), ())), preferred_element_type=F32)


def _split(a):
    hi = a.astype(BF16)
    return hi, (a - hi.astype(F32)).astype(BF16)


def _dot3(a, b, dot=_dot):
    ah, al = _split(a)
    bh, bl = _split(b)
    return dot(ah, bh) + (dot(ah, bl) + dot(al, bh))


def _params(n_grid, parallel=True):
    sem = ("parallel",) + ("arbitrary",) * (n_grid - 1) if parallel else ("arbitrary",) * n_grid
    return pltpu.CompilerParams(dimension_semantics=sem, vmem_limit_bytes=VMEM_LIMIT)


def _const_spec(shape):
    nd = len(shape)
    return pl.BlockSpec(shape, lambda *_: (0,) * nd, pipeline_mode=pl.Buffered(1))


def _row_tile(rows, target):
    tile = target
    while rows % tile:
        tile -= BLK
    return tile


def _blocks_per_step(nb, prefs):
    for s in prefs:
        if nb % s == 0:
            return s
    return 1


def _ffn_kernel(h_ref, gpre_ref, gpost_ref, wg_ref, wu_ref, wd_ref, *rest, tf, lead_every):
    o_ref, xn_ref, act_ref = rest[-3:]
    h = h_ref[...]
    if lead_every:
        h = jnp.where(pl.program_id(0) % lead_every == 0, rest[0][...], h)
    xn_ref[...] = _rms(h, gpre_ref[...]).astype(BF16)
    for c in range(0, wg_ref.shape[1], tf):
        xn = xn_ref[...]
        g = _dot(xn, wg_ref[:, c:c + tf])
        u = _dot(xn, wu_ref[:, c:c + tf])
        act_ref[:, c:c + tf] = (_silu(g) * u).astype(BF16)
    y = _dot(act_ref[...], wd_ref[...])
    o_ref[...] = h + 0.5 * _rms(y, gpost_ref[...])


def _ffn(src, g_pre, g_post, w_gate, w_up, w_down, *, tm, src_spec=None, n_tiles=None, lead=None, lead_every=0,
         tf=256):
    D = src.shape[1]
    F = w_gate.shape[1]
    n_tiles = n_tiles or src.shape[0] // tm
    row = pl.BlockSpec((tm, D), lambda i: (i, 0))
    operands = [src, g_pre.reshape(1, D), g_post.reshape(1, D),
                w_gate.astype(BF16), w_up.astype(BF16), w_down.astype(BF16)]
    in_specs = [src_spec or row, _const_spec((1, D)), _const_spec((1, D)),
                _const_spec((D, F)), _const_spec((D, F)), _const_spec((F, D))]
    if lead is not None:
        operands.append(lead)
        in_specs.append(pl.BlockSpec((tm, D), lambda i: (i // lead_every, 0)))
    return pl.pallas_call(
        functools.partial(_ffn_kernel, tf=tf, lead_every=lead_every),
        grid=(n_tiles,),
        in_specs=in_specs,
        out_specs=row,
        out_shape=jax.ShapeDtypeStruct((n_tiles * tm, D), F32),
        scratch_shapes=[pltpu.VMEM((tm, D), BF16), pltpu.VMEM((tm, F), BF16)],
        compiler_params=_params(1),
        name="ffn",
    )(*operands)


def _shifted_token_rows(tm, S, Lp, D):
    per_batch = Lp // tm
    return pl.BlockSpec((pl.Element(tm), pl.Element(D)),
                        lambda i: (BLK * ((i // per_batch) * (S // BLK)
                                          + jnp.maximum((i % per_batch) * (tm // BLK) - 1, 0)), 0))


def _token_rows(tm, S, Lp, D):
    per_batch = S // tm
    return pl.BlockSpec((pl.Element(tm), pl.Element(D)),
                        lambda i: (BLK * ((i // per_batch) * (Lp // BLK) + 1 + (i % per_batch) * (tm // BLK)), 0))


def _proj_kernel(h_ref, g_ref, w_ref, o_ref, *, tn):
    xn = _rms(h_ref[...], g_ref[...]).astype(BF16)
    for c in range(0, w_ref.shape[1], tn):
        o_ref[:, c:c + tn] = _dot(xn, w_ref[:, c:c + tn]).astype(o_ref.dtype)


def _proj(h, g, w, out_dtype, *, tm=512, tn=256):
    T, D = h.shape
    N = w.shape[1]
    tm = _row_tile(T, tm)
    return pl.pallas_call(
        functools.partial(_proj_kernel, tn=tn),
        grid=(T // tm,),
        in_specs=[pl.BlockSpec((tm, D), lambda i: (i, 0)), _const_spec((1, D)), _const_spec((D, N))],
        out_specs=pl.BlockSpec((tm, N), lambda i: (i, 0)),
        out_shape=jax.ShapeDtypeStruct((T, N), out_dtype),
        compiler_params=_params(1),
        name="in_proj",
    )(h, g.reshape(1, D), w.astype(BF16))


def _proj_ab_kernel(h_ref, g_ref, w_ref, cos_ref, sin_ref, cw_ref, o16_ref, o32_ref, xn_ref, halo_ref, *, tn):
    rows = h_ref.shape[0]
    W = RET_W

    @pl.when(pl.program_id(1) == 0)
    def _start_of_sequence():
        halo_ref[...] = jnp.zeros_like(halo_ref)

    xn_ref[...] = _rms(h_ref[...], g_ref[...]).astype(BF16)
    cos, sin = cos_ref[...], sin_ref[...]
    rot = lambda t: t * cos + pltpu.roll(t, RET_DK // 2, 1) * sin
    l2n = lambda t: t * lax.rsqrt(jnp.sum(t * t, axis=-1, keepdims=True) + NORM_EPS)
    heavy = list(range(4 * W, 7 * W, tn))
    light = [c for c in range(0, AB_GATE_COL, tn) if c not in heavy]
    order = [c for pair in zip(heavy, light) for c in pair] + light[len(heavy):]
    for c in order:
        pre = _dot(xn_ref[...], w_ref[:, c:c + tn])
        group, off = divmod(c, W)
        if group <= 1:
            scale = 1.0 if group == 0 else RET_DK ** -0.5
            for hs in range(0, tn, RET_DK):
                o16_ref[:, c + hs:c + hs + RET_DK] = (rot(pre[:, hs:hs + RET_DK]) * scale).astype(BF16)
        elif group == 2:
            o16_ref[:, c:c + tn] = pre.astype(BF16)
        elif group == 3:
            o32_ref[:, off:off + tn] = _silu(pre)
        elif group <= 6:
            ch = c - 4 * W
            x_ext = jnp.concatenate([halo_ref[:, ch:ch + tn], pre], axis=0)
            y = cw_ref[GDN_CONV - 1:GDN_CONV, ch:ch + tn] * pre
            for shift in range(1, GDN_CONV):
                tap = GDN_CONV - 1 - shift
                y = y + cw_ref[tap:tap + 1, ch:ch + tn] * pltpu.roll(x_ext, shift, 0)[CONV_HALO:]
            halo_ref[:, ch:ch + tn] = pre[rows - CONV_HALO:rows]
            act = _silu(y)
            if group == 6:
                o32_ref[:, W + ch:W + ch + tn] = act
            else:
                scale = GDN_DK ** -0.5 if group == 4 else 1.0
                for hs in range(0, tn, GDN_DK):
                    o32_ref[:, W + ch + hs:W + ch + hs + GDN_DK] = l2n(act[:, hs:hs + GDN_DK]) * scale
        else:
            o32_ref[:, 4 * W + off:4 * W + off + tn] = _silu(pre)
    o32_ref[:, 5 * W:5 * W + BLK] = _dot(xn_ref[...], w_ref[:, AB_GATE_COL:AB_GATE_COL + BLK])


def _proj_ab(h, g, w_in, conv_w, cos, sin, B, Lp, *, tm=640, tn=256):
    D = h.shape[1]
    tm = _row_tile(Lp, tm)
    W = RET_W
    seq = lambda n: pl.BlockSpec((None, tm, n), lambda b, j: (b, j, 0))
    tab = pl.BlockSpec((tm, RET_DK), lambda b, j: (j, 0))
    return pl.pallas_call(
        functools.partial(_proj_ab_kernel, tn=tn),
        grid=(B, Lp // tm),
        in_specs=[seq(D), _const_spec((1, D)), _const_spec((D, AB_COLS)), tab, tab,
                  _const_spec((GDN_CONV, 3 * W))],
        out_specs=[seq(3 * W), seq(5 * W + BLK)],
        out_shape=[jax.ShapeDtypeStruct((B, Lp, 3 * W), BF16), jax.ShapeDtypeStruct((B, Lp, 5 * W + BLK), F32)],
        scratch_shapes=[pltpu.VMEM((tm, D), BF16), pltpu.VMEM((CONV_HALO, 3 * W), F32)],
        compiler_params=_params(2, parallel=False),
        name="in_proj_ab",
    )(h.reshape(B, Lp, D), g.reshape(1, D), _ab_weight(w_in.astype(BF16)), cos, sin, conv_w.astype(F32))


def _outproj_kernel(a_ref, b_ref, h_ref, g_ref, wa_ref, wb_ref, o_ref):
    y = _dot(a_ref[...], wa_ref[...]) + _dot(b_ref[...], wb_ref[...])
    o_ref[...] = h_ref[...] + _rms(y, g_ref[...])


def _outproj(a, b, h, g, w_out, *, tm=512):
    T, D = h.shape
    Ka, Kb = a.shape[1], b.shape[1]
    tm = _row_tile(T, tm)
    w = w_out.astype(BF16)
    row = lambda n: pl.BlockSpec((tm, n), lambda i: (i, 0))
    return pl.pallas_call(
        _outproj_kernel,
        grid=(T // tm,),
        in_specs=[row(Ka), row(Kb), row(D), _const_spec((1, D)), _const_spec((Ka, D)), _const_spec((Kb, D))],
        out_specs=row(D),
        out_shape=jax.ShapeDtypeStruct((T, D), F32),
        compiler_params=_params(1),
        name="out_proj",
    )(a, b, h, g.reshape(1, D), w[:Ka], w[Ka:])


def _ret_kernel(q_ref, k_ref, v_ref, gate_ref, o_ref, s_ref, dec_ref, zx_ref):
    nbatch, C = q_ref.shape[0], q_ref.shape[1]
    n = pl.program_id(0)
    log_gamma = [math.log1p(-2.0 ** (-5.0 - hd)) for hd in range(RET_HEADS)]

    @pl.when(n == 0)
    def _init():
        s_ref[...] = jnp.zeros_like(s_ref)
        diff = (lax.broadcasted_iota(jnp.int32, (C, C), 0) - lax.broadcasted_iota(jnp.int32, (C, C), 1)).astype(F32)
        idx = lax.broadcasted_iota(jnp.int32, (C, RET_DK), 0).astype(F32)
        for hd, lg in enumerate(log_gamma):
            dec_ref[hd] = jnp.where(diff >= 0, jnp.exp(jnp.maximum(diff, 0.0) * lg), 0.0)
            zx_ref[0, hd] = jnp.exp((C - 1.0 - idx) * lg)
            zx_ref[1, hd] = jnp.exp((idx + 1.0) * lg)

    chains = [(b, hd) for b in range(nbatch) for hd in range(RET_HEADS)]
    sl = lambda hd: slice(hd * RET_DK, (hd + 1) * RET_DK)
    q = [q_ref[b, :, sl(hd)] for b, hd in chains]
    k = [k_ref[b, :, sl(hd)] for b, hd in chains]
    v = [v_ref[b, :, sl(hd)] for b, hd in chains]
    scores = [_dot_nt(qq, kk) * dec_ref[hd] for qq, kk, (_, hd) in zip(q, k, chains)]
    intra = [_dot(sc.astype(BF16), vv) for sc, vv in zip(scores, v)]
    state = [s_ref[i] for i in range(len(chains))]
    cross = [_dot(qq, st.astype(BF16)) * zx_ref[1, hd] for qq, st, (_, hd) in zip(q, state, chains)]
    kv = [_dot_tn(kk, (vv.astype(F32) * zx_ref[0, hd]).astype(BF16)) for kk, vv, (_, hd) in zip(k, v, chains)]
    for i, (b, hd) in enumerate(chains):
        s_ref[i] = state[i] * math.exp(C * log_gamma[hd]) + kv[i]
        o = intra[i] + cross[i]
        mu = jnp.mean(o, axis=-1, keepdims=True)
        var = jnp.mean(jnp.square(o - mu), axis=-1, keepdims=True)
        o_ref[b, :, sl(hd)] = ((o - mu) * lax.rsqrt(var + NORM_EPS) * gate_ref[b, :, sl(hd)]).astype(o_ref.dtype)


def _retention(p16, p32, B, Lp):
    nb = Lp // BLK
    col = lambda j: pl.BlockSpec((B, BLK, RET_W), lambda n: (0, n, j))
    out = pl.pallas_call(
        _ret_kernel,
        grid=(nb,),
        in_specs=[col(0), col(1), col(2), col(0)],
        out_specs=pl.BlockSpec((B, BLK, RET_W), lambda n: (0, n, 0)),
        out_shape=jax.ShapeDtypeStruct((B, Lp, RET_W), BF16),
        scratch_shapes=[pltpu.VMEM((B * RET_HEADS, RET_DK, RET_DV), F32),
                        pltpu.VMEM((RET_HEADS, BLK, BLK), F32),
                        pltpu.VMEM((2, RET_HEADS, BLK, RET_DK), F32)],
        compiler_params=_params(1, parallel=False),
        name="retention",
    )(p16, p16, p16, p32)
    return out.reshape(B * Lp, RET_W)


def _unit_lower_inverses(a_mats, eye, same_blk):
    d = [jnp.where(same_blk, a, 0.0) for a in a_mats]
    e = [(a - dd).astype(BF16) for a, dd in zip(a_mats, d)]
    x = [-dd for dd in d]
    t = _product_of_powers([eye + xx for xx in x], x, 3)
    tb = [tt.astype(BF16) for tt in t]
    f = [-_dot(tt, ee) for tt, ee in zip(tb, e)]
    p = _product_of_powers([eye + ff for ff in f], f, 2)
    return [pp.astype(BF16) for pp in p], tb


def _product_of_powers(prod, x, steps):
    size = x[0].shape[1]
    x = [_dot(xb, xb) for xb in [xx.astype(BF16) for xx in x]]
    for step in range(steps):
        xb = [xx.astype(BF16) for xx in x]
        if step == steps - 1:
            return [pp + _dot(xx, pp.astype(BF16)) for pp, xx in zip(prod, xb)]
        both = [_dot(xx, jnp.concatenate([xx, pp.astype(BF16)], axis=1)) for pp, xx in zip(prod, xb)]
        x = [bb[:, :size] for bb in both]
        prod = [pp + bb[:, size:] for pp, bb in zip(prod, both)]


def _gdn_kernel(q_ref, k_ref, v_ref, z_ref, gate_ref, alog_ref, dtb_ref, onorm_ref, o_ref, s_ref):
    nbatch, C = q_ref.shape[0], q_ref.shape[1]
    n = pl.program_id(0)

    @pl.when(n == 0)
    def _init():
        s_ref[...] = jnp.zeros_like(s_ref)

    row = lax.broadcasted_iota(jnp.int32, (C, C), 0)
    col = lax.broadcasted_iota(jnp.int32, (C, C), 1)
    incl = row >= col
    strict = row > col
    same_blk = (row // SOLVE_BLK) == (col // SOLVE_BLK)
    eye = jnp.where(row == col, 1.0, 0.0).astype(F32)
    tri_incl = jnp.where(incl, 1.0, 0.0).astype(F32)
    tri_upper = jnp.where(row <= col, 1.0, 0.0).astype(F32)

    gates = [gate_ref[b] for b in range(nbatch)]
    beta_all = [jax.nn.sigmoid(gt) for gt in gates]
    g_all = [-jnp.exp(alog_ref[...]) * jax.nn.softplus(gt + dtb_ref[...]) for gt in gates]
    gcum_all = [_dot3(tri_incl, g) for g in g_all]
    gcum_t = [_dot3(g, tri_upper, _dot_tn) for g in g_all]

    chains = [(b, hd) for b in range(nbatch) for hd in range(GDN_HEADS)]
    head = lambda ref, b, hd: ref[b, :, hd * GDN_DK:(hd + 1) * GDN_DK]
    q = [head(q_ref, b, hd) for b, hd in chains]
    k = [head(k_ref, b, hd) for b, hd in chains]
    v = [head(v_ref, b, hd) for b, hd in chains]
    beta = [beta_all[b][:, hd:hd + 1] for b, hd in chains]
    gcum = [gcum_all[b][:, GDN_HEADS + hd:GDN_HEADS + hd + 1] for b, hd in chains]
    gcum_row = [gcum_t[b][GDN_HEADS + hd:GDN_HEADS + hd + 1, :] for b, hd in chains]
    decay = [jnp.where(incl, jnp.exp(jnp.where(incl, gc - gr, 0.0)), 0.0) for gc, gr in zip(gcum, gcum_row)]
    k_beta = [kk * bb for kk, bb in zip(k, beta)]
    scores = [_dot_nt(jnp.concatenate([kbt.astype(BF16), qq.astype(BF16)], axis=0), kk.astype(BF16))
              for kbt, qq, kk in zip(k_beta, q, k)]
    a_mat = [jnp.where(strict, sc[:C] * dc, 0.0) for sc, dc in zip(scores, decay)]
    p_mat, t_inv = _unit_lower_inverses(a_mat, eye, same_blk)
    e_gcum = [jnp.exp(gc) for gc in gcum]
    rhs = [jnp.concatenate([vv * bb, kbt * eg], axis=-1) for vv, bb, kbt, eg in zip(v, beta, k_beta, e_gcum)]
    sol = [_dot(tt, rr.astype(BF16)) for tt, rr in zip(t_inv, rhs)]
    sol = [_dot(pp, ss.astype(BF16)) for pp, ss in zip(p_mat, sol)]
    qk = [jnp.where(incl, sc[C:] * dc, 0.0) for sc, dc in zip(scores, decay)]
    g_last = [gc[C - 1:C, :] for gc in gcum]
    k_tail = [(kk * jnp.exp(gl - gc)).astype(BF16) for kk, gl, gc in zip(k, g_last, gcum)]
    q_dec = [(qq * eg).astype(BF16) for qq, eg in zip(q, e_gcum)]

    state = [s_ref[i] for i in range(len(chains))]
    state_b = [st.astype(BF16) for st in state]
    v_new = [ss[:, :GDN_DV] - _dot(ss[:, GDN_DV:].astype(BF16), sb) for ss, sb in zip(sol, state_b)]
    v_new_b = [vn.astype(BF16) for vn in v_new]
    out = [_dot(qd, sb) + _dot(qkm.astype(BF16), vn) for qd, sb, qkm, vn in zip(q_dec, state_b, qk, v_new_b)]
    for i, (b, hd) in enumerate(chains):
        s_ref[i] = state[i] * jnp.exp(g_last[i]) + _dot_tn(k_tail[i], v_new_b[i])
        sl = slice(hd * GDN_DV, (hd + 1) * GDN_DV)
        o_ref[b, :, sl] = (_rms(out[i], onorm_ref[...]) * z_ref[b, :, sl]).astype(o_ref.dtype)


def _gdn(p32, a_log, dt_bias, out_norm, B, Lp):
    C = BLK
    nc = Lp // C
    col = lambda j: pl.BlockSpec((B, C, RET_W), lambda n: (0, n, j))
    gate_lanes = jnp.zeros((1, BLK), F32)
    alog = gate_lanes.at[0, GDN_HEADS:2 * GDN_HEADS].set(a_log.astype(F32))
    dtb = gate_lanes.at[0, GDN_HEADS:2 * GDN_HEADS].set(dt_bias.astype(F32))
    out = pl.pallas_call(
        _gdn_kernel,
        grid=(nc,),
        in_specs=[col(1), col(2), col(3), col(4),
                  pl.BlockSpec((B, C, BLK), lambda n: (0, n, 5 * RET_W // BLK)),
                  _const_spec((1, BLK)), _const_spec((1, BLK)), _const_spec((1, GDN_DV))],
        out_specs=pl.BlockSpec((B, C, RET_W), lambda n: (0, n, 0)),
        out_shape=jax.ShapeDtypeStruct((B, Lp, RET_W), BF16),
        scratch_shapes=[pltpu.VMEM((B * GDN_HEADS, GDN_DK, GDN_DV), F32)],
        compiler_params=_params(1, parallel=False),
        name="gated_deltanet",
    )(p32, p32, p32, p32, p32, alog, dtb, out_norm.astype(F32).reshape(1, GDN_DV))
    return out.reshape(B * Lp, RET_W)


def _swa_head_perm():
    G = SWA_HEADS // SWA_KV_HEADS
    heads = [kv * G + b for b in range(G) for kv in range(SWA_KV_HEADS)]
    return np.concatenate([np.arange(h * SWA_DH, (h + 1) * SWA_DH) for h in heads])


def _swa_kernel(q_ref, kc_ref, vc_ref, kp_ref, vp_ref, km_ref, vm_ref, sink_ref, o_ref, *, S):
    n = pl.program_id(1)
    G = SWA_HEADS // SWA_KV_HEADS
    R = G * BLK
    r = lax.broadcasted_iota(jnp.int32, (R, BLK), 0) & (BLK - 1)
    col = lax.broadcasted_iota(jnp.int32, (R, BLK), 1)
    lower = col <= r
    upper = col > r
    is_meta = col >= PAD
    halves = [col < SWA_DH, col >= SWA_DH]
    k_meta, v_meta = km_ref[...], vm_ref[...]
    for s in range(S):
        blk = n * S + s
        rows = slice(s * BLK, (s + 1) * BLK)
        k_cur, v_cur = kc_ref[rows, :], vc_ref[rows, :]
        if s == 0:
            k_prev, v_prev = kp_ref[...], vp_ref[...]
        else:
            k_prev, v_prev = kc_ref[(s - 1) * BLK:s * BLK, :], vc_ref[(s - 1) * BLK:s * BLK, :]
        cur_ok = lower & (blk >= 1)
        prev_ok = upper & (blk >= 2)
        meta_ok = is_meta & ((blk >= 1) | lower)
        q_st = jnp.concatenate([q_ref[rows, b * BLK:(b + 1) * BLK] for b in range(G)], axis=0) * SWA_DH ** -0.5
        outs = []
        for kv in range(SWA_KV_HEADS):
            q = jnp.where(halves[kv], q_st, 0)
            s_cur = jnp.where(cur_ok, _dot_nt(q, k_cur), NEG_BIG)
            s_prev = jnp.where(prev_ok, _dot_nt(q, k_prev), NEG_BIG)
            s_meta = jnp.where(meta_ok, _dot_nt(q, k_meta), NEG_BIG)
            sink = jnp.concatenate([jnp.full((BLK, 1), sink_ref[kv * G + b], F32) for b in range(G)], axis=0)
            m = jnp.maximum(jnp.max(jnp.maximum(jnp.maximum(s_cur, s_prev), s_meta), axis=-1, keepdims=True), sink)
            p_cur = jnp.exp(s_cur - m)
            p_prev = jnp.exp(s_prev - m)
            p_meta = jnp.exp(s_meta - m)
            denom = jnp.sum(p_cur + p_prev + p_meta, axis=-1, keepdims=True) + jnp.exp(sink - m)
            o = (_dot(p_cur.astype(BF16), v_cur) + _dot(p_prev.astype(BF16), v_prev)
                 + _dot(p_meta.astype(BF16), v_meta))
            outs.append(o / denom)
        o = jnp.where(halves[0], outs[0], outs[1])
        for b in range(G):
            o_ref[rows, b * BLK:(b + 1) * BLK] = o[b * BLK:(b + 1) * BLK, :].astype(o_ref.dtype)


def _swa(proj, sinks, B, Lp):
    T = proj.shape[0]
    nb = Lp // BLK
    S = _blocks_per_step(nb, (5,))
    ns = nb // S
    k_col, v_col = RET_W // BLK, RET_W // BLK + 1
    cur = lambda j: pl.BlockSpec((S * BLK, BLK), lambda b, n: (b * ns + n, j))
    prev = lambda j: pl.BlockSpec((BLK, BLK), lambda b, n: (b * nb + jnp.maximum(n * S - 1, 0), j))
    first = lambda j: pl.BlockSpec((BLK, BLK), lambda b, n: (b * nb, j))
    return pl.pallas_call(
        functools.partial(_swa_kernel, S=S),
        grid=(B, ns),
        in_specs=[pl.BlockSpec((S * BLK, RET_W), lambda b, n: (b * ns + n, 0)),
                  cur(k_col), cur(v_col), prev(k_col), prev(v_col), first(k_col), first(v_col),
                  pl.BlockSpec(memory_space=pltpu.SMEM)],
        out_specs=pl.BlockSpec((S * BLK, RET_W), lambda b, n: (b * ns + n, 0)),
        out_shape=jax.ShapeDtypeStruct((T, RET_W), BF16),
        compiler_params=_params(2),
        name="swa_sink",
    )(proj, proj, proj, proj, proj, proj, proj, sinks.astype(F32))


def _sb_kernel(q_ref, k_ref, v_ref, o_ref, acc_ref, run_ref, *, S):
    n = pl.program_id(2)
    heads = BLK // SB_DH
    row = lax.broadcasted_iota(jnp.int32, (BLK, BLK), 0)
    col = lax.broadcasted_iota(jnp.int32, (BLK, BLK), 1)
    suffix = jnp.where(row > col, 1.0, 0.0).astype(BF16)
    ones = jnp.ones((BLK, BLK), BF16)
    sum_rhs = jnp.concatenate([jnp.concatenate([suffix, ones], axis=1)] * 2, axis=0)
    lane = lax.broadcasted_iota(jnp.int32, (S * BLK, BLK), 1)
    q_all = q_ref[...] * SB_DH ** -0.5
    q_head = [jnp.where((lane >= hh * SB_DH) & (lane < (hh + 1) * SB_DH), q_all, 0).reshape(S, BLK, BLK)
              for hh in range(heads)]

    def diagonal(hh, k, v, valid, run_in):
        z = jnp.einsum("sqd,skd->sqk", q_head[hh], k, preferred_element_type=F32)
        softplus_neg = jnp.log(1.0 + jnp.exp(-jnp.abs(z)))
        log_beta = jnp.minimum(z, 0.0) - softplus_neg
        log_1m = log_beta - z
        if valid is not None:
            log_1m = jnp.where(valid, log_1m, 0.0)
        hi, lo = _split(log_1m)
        sums = _dot(jnp.concatenate([hi, lo], axis=-1).reshape(S * BLK, 2 * BLK), sum_rhs)
        sums = sums.reshape(S, BLK, 2 * BLK)
        log_stick = sums[..., :BLK] if run_in is None else sums[..., :BLK] + run_in
        a = jnp.exp(log_beta + log_stick)
        if valid is not None:
            a = jnp.where(valid, a, 0.0)
        av = jnp.einsum("sqk,skd->sqd", a.astype(BF16), v, preferred_element_type=F32)
        run = sums[..., BLK:] if run_in is None else run_in + sums[..., BLK:]
        return av, run

    def keep_going(runs):
        top = jnp.max(runs[0])
        for run in runs[1:]:
            top = jnp.maximum(top, jnp.max(run))
        return (top >= SB_SKIP).astype(jnp.int32)

    def slab(ref, first_blk):
        at = pl.multiple_of(first_blk * BLK, BLK)
        return ref[pl.ds(at, S * BLK), :].reshape(S, BLK, BLK)

    lead = min(SB_LEAD, S)

    def leading(at_sequence_start):
        k_diag, v_diag = slab(k_ref, n * S), slab(v_ref, n * S)
        k_before, v_before = [], []
        for back in range(lead - 1, 0, -1):
            at = pl.multiple_of(jnp.maximum(n * S - back, 0) * BLK, BLK)
            k_before.append(k_ref[pl.ds(at, BLK), :][None])
            v_before.append(v_ref[pl.ds(at, BLK), :][None])
        accs, runs = [None] * heads, [None] * heads
        for d in range(lead):
            k = jnp.concatenate(k_before[lead - 1 - d:] + [k_diag[:S - d]], axis=0)
            v = jnp.concatenate(v_before[lead - 1 - d:] + [v_diag[:S - d]], axis=0)
            if at_sequence_start:
                ok = jnp.stack([(n * S + s - d) * BLK + col >= PAD for s in range(S)])
                ok = ok & (col < row) if d == 0 else ok
            else:
                ok = jnp.broadcast_to(col < row, (S, BLK, BLK)) if d == 0 else None
            for hh in range(heads):
                av, runs[hh] = diagonal(hh, k, v, ok, runs[hh])
                accs[hh] = av if d == 0 else accs[hh] + av
        for hh in range(heads):
            acc_ref[hh] = accs[hh]
            run_ref[hh] = runs[hh]
        return keep_going(runs)

    go = lax.cond(n * S <= lead - 1, lambda: leading(True), lambda: leading(False))

    def cond(carry):
        d, go = carry
        return (d <= n * S + S - 1) & (go > 0)

    def body(carry):
        d, _ = carry
        first_blk = n * S - d

        def further(k, v, valid):
            runs = []
            for hh in range(heads):
                av, run = diagonal(hh, k, v, valid, run_ref[hh])
                acc_ref[hh] += av
                run_ref[hh] = run
                runs.append(run)
            return keep_going(runs)

        def interior():
            return further(slab(k_ref, first_blk), slab(v_ref, first_blk), None)

        def edge():
            ks, vs, valids = [], [], []
            for s in range(S):
                jb = first_blk + s
                at = pl.multiple_of(jnp.maximum(jb, 0) * BLK, BLK)
                ks.append(k_ref[pl.ds(at, BLK), :])
                vs.append(v_ref[pl.ds(at, BLK), :])
                valids.append(jb * BLK + col >= PAD)
            return further(jnp.stack(ks), jnp.stack(vs), jnp.stack(valids))

        return d + 1, lax.cond(first_blk >= 1, interior, edge)

    lax.while_loop(cond, body, (jnp.int32(lead), go))
    o = acc_ref[0]
    for hh in range(1, heads):
        o = jnp.where(lane.reshape(S, BLK, BLK) >= hh * SB_DH, acc_ref[hh], o)
    o_ref[...] = o.reshape(S * BLK, BLK).astype(o_ref.dtype)


def _stick_breaking(proj, B, Lp):
    T = proj.shape[0]
    nb = Lp // BLK
    S = _blocks_per_step(nb, (13, 5))
    ns = nb // S
    pairs = SB_HEADS * SB_DH // BLK
    heads = BLK // SB_DH
    q0 = (SWA_HEADS + 2 * SWA_KV_HEADS) * SWA_DH // BLK
    k0, v0 = q0 + pairs, q0 + 2 * pairs
    proj3 = proj.reshape(B, Lp, proj.shape[1])
    seq = lambda c0: pl.BlockSpec((None, Lp, BLK), lambda b, p, n: (b, 0, c0 + p))
    return pl.pallas_call(
        functools.partial(_sb_kernel, S=S),
        grid=(B, pairs, ns),
        in_specs=[pl.BlockSpec((S * BLK, BLK), lambda b, p, n: (b * ns + n, q0 + p)), seq(k0), seq(v0)],
        out_specs=pl.BlockSpec((S * BLK, BLK), lambda b, p, n: (b * ns + n, p)),
        out_shape=jax.ShapeDtypeStruct((T, RET_W), BF16),
        scratch_shapes=[pltpu.VMEM((heads, S, BLK, BLK), F32), pltpu.VMEM((heads, S, BLK, BLK), F32)],
        compiler_params=_params(3),
        name="stick_breaking",
    )(proj, proj3, proj3)


def _rotation_tables(Lp):
    half = RET_DK // 2
    inv_freq = 1.0 / (10000.0 ** jnp.linspace(0.0, 1.0, half, dtype=F32))
    pos = jnp.arange(Lp, dtype=F32) - float(PAD)
    ang = pos[:, None] * inv_freq[None, :]
    cos, sin = jnp.cos(ang), jnp.sin(ang)
    return jnp.concatenate([cos, cos], axis=1), jnp.concatenate([-sin, sin], axis=1)


def _ab_weight(w_in):
    D = w_in.shape[0]
    perm = np.concatenate([np.arange(0, RET_DK, 2), np.arange(1, RET_DK, 2)])
    qk_perm = np.concatenate([h * RET_DK + perm for h in range(RET_HEADS)])
    cols = np.concatenate([qk_perm, RET_W + qk_perm, np.arange(2 * RET_W, w_in.shape[1])])
    return jnp.concatenate([w_in[:, cols], jnp.zeros((D, AB_COLS - w_in.shape[1]), w_in.dtype)], axis=1)


def kernel(x, meta_tokens, norm_gains, ffn_w_gate, ffn_w_up, ffn_w_down, ab_w_in, ab_conv_w, ab_a_log, ab_dt_bias, ab_out_norm, ab_w_out, cd_w_in, cd_sinks, cd_w_out):
    B, S, D = x.shape
    Lp = S + BLK
    depth = norm_gains.shape[0]
    meta = jnp.broadcast_to(meta_tokens[None].astype(x.dtype), (B, N_META, D))
    tm = _row_tile(B * Lp, 512)
    tm_tok = _row_tile(S, 512)
    token_rows = _token_rows(tm_tok, S, Lp, D)
    tm_seq = _row_tile(Lp, 640)
    lead_tiles = jnp.concatenate([jnp.zeros((B, PAD, D), x.dtype), meta, x[:, :tm_seq - BLK]], axis=1)
    cos, sin = _rotation_tables(Lp)
    swa_perm = _swa_head_perm()
    swa_w = SWA_HEADS * SWA_DH
    for i in range(depth):
        g = norm_gains[i]
        j = i // 2
        ffn1 = (g[0], g[1], ffn_w_gate[i, 0], ffn_w_up[i, 0], ffn_w_down[i, 0])
        if i == 0:
            h = _ffn(x.reshape(B * S, D), *ffn1, tm=tm_seq, src_spec=_shifted_token_rows(tm_seq, S, Lp, D),
                     n_tiles=B * Lp // tm_seq, lead=lead_tiles.reshape(B * tm_seq, D), lead_every=Lp // tm_seq)
        else:
            h = _ffn(h, *ffn1, tm=tm)
        if i % 2 == 0:
            p16, p32 = _proj_ab(h, g[2], ab_w_in[j], ab_conv_w[j], cos, sin, B, Lp)
            mix_a = _retention(p16, p32, B, Lp)
            mix_b = _gdn(p32, ab_a_log[j], ab_dt_bias[j], ab_out_norm[j], B, Lp)
            h = _outproj(mix_a, mix_b, h, g[3], ab_w_out[j])
        else:
            w_in, w_out = cd_w_in[j].astype(BF16), cd_w_out[j].astype(BF16)
            w_in = jnp.concatenate([w_in[:, swa_perm], w_in[:, swa_w:]], axis=1)
            w_out = jnp.concatenate([w_out[swa_perm], w_out[swa_w:]], axis=0)
            proj = _proj(h, g[2], w_in, BF16, tm=640)
            mix_a = _swa(proj, cd_sinks[j], B, Lp)
            mix_b = _stick_breaking(proj, B, Lp)
            h = _outproj(mix_a, mix_b, h, g[3], w_out)
        ffn2 = (g[4], g[5], ffn_w_gate[i, 1], ffn_w_up[i, 1], ffn_w_down[i, 1])
        if i == depth - 1:
            h = _ffn(h, *ffn2, tm=tm_tok, src_spec=token_rows, n_tiles=B * S // tm_tok)
        else:
            h = _ffn(h, *ffn2, tm=tm)
    return h.reshape(B, S, D)
```

```python
import functools
import math

import numpy as np
import jax
import jax.numpy as jnp
from jax import lax
from jax.experimental import pallas as pl
from jax.experimental.pallas import tpu as pltpu

F32 = jnp.float32
BF16 = jnp.bfloat16

N_META = 16
NORM_EPS = 1e-6
BLK = 128
PAD = BLK - N_META

RET_HEADS, RET_DK, RET_DV = 4, 128, 128
GDN_HEADS, GDN_DK, GDN_DV, GDN_CONV = 4, 128, 128, 4
SWA_HEADS, SWA_KV_HEADS, SWA_DH = 8, 2, 64
SB_HEADS, SB_DH = 8, 64
SOLVE_BLK = 16

RET_W = RET_HEADS * RET_DK
AB_GATE_COL = 4096
AB_COLS = AB_GATE_COL + BLK
CONV_HALO = 8

VMEM_LIMIT = 56 * 1024 * 1024
NEG_BIG = -1e30
SB_SKIP = -87.5
SB_LEAD = 3


def _rms(x, g):
    return x * lax.rsqrt(jnp.mean(x * x, axis=-1, keepdims=True) + NORM_EPS) * g


def _silu(x):
    return x * jax.nn.sigmoid(x)


def _dot(a, b):
    return jnp.dot(a, b, preferred_element_type=F32)


def _dot_nt(a, b):
    return lax.dot_general(a, b, (((1,), (1,)), ((), ())), preferred_element_type=F32)


def _dot_tn(a, b):
    return lax.dot_general(a, b, (((0,), (0,)), ((), ())), preferred_element_type=F32)


def _split(a):
    hi = a.astype(BF16)
    return hi, (a - hi.astype(F32)).astype(BF16)


def _dot3(a, b, dot=_dot):
    ah, al = _split(a)
    bh, bl = _split(b)
    return dot(ah, bh) + (dot(ah, bl) + dot(al, bh))


def _params(n_grid, parallel=True):
    sem = ("parallel",) + ("arbitrary",) * (n_grid - 1) if parallel else ("arbitrary",) * n_grid
    return pltpu.CompilerParams(dimension_semantics=sem, vmem_limit_bytes=VMEM_LIMIT)


def _const_spec(shape):
    nd = len(shape)
    return pl.BlockSpec(shape, lambda *_: (0,) * nd, pipeline_mode=pl.Buffered(1))


def _row_tile(rows, target):
    tile = target
    while rows % tile:
        tile -= BLK
    return tile


def _blocks_per_step(nb, prefs):
    for s in prefs:
        if nb % s == 0:
            return s
    return 1


def _ffn_kernel(h_ref, gpre_ref, gpost_ref, wg_ref, wu_ref, wd_ref, *rest, tf, lead_every, mixed):
    o_ref, xn_ref, act_ref = rest[-3:]
    h = h_ref[...]
    if lead_every:
        h = jnp.where(pl.program_id(0) % lead_every == 0, rest[0][...], h)
    if mixed:
        a_ref, b_ref, gmix_ref, wa_ref, wb_ref = rest[:5]
        h = h + _rms(_dot(a_ref[...], wa_ref[...]) + _dot(b_ref[...], wb_ref[...]), gmix_ref[...])
    xn_ref[...] = _rms(h, gpre_ref[...]).astype(BF16)
    for c in range(0, wg_ref.shape[1], tf):
        xn = xn_ref[...]
        g = _dot(xn, wg_ref[:, c:c + tf])
        u = _dot(xn, wu_ref[:, c:c + tf])
        act_ref[:, c:c + tf] = (_silu(g) * u).astype(BF16)
    y = _dot(act_ref[...], wd_ref[...])
    o_ref[...] = h + 0.5 * _rms(y, gpost_ref[...])


def _ffn(src, g_pre, g_post, w_gate, w_up, w_down, *, tm, rows=None, n_tiles=None, lead=None, lead_every=0,
         mix=None, tf=256):
    D = src.shape[1]
    F = w_gate.shape[1]
    n_tiles = n_tiles or src.shape[0] // tm
    rows = rows or (lambda width: pl.BlockSpec((tm, width), lambda i: (i, 0)))
    operands = [src, g_pre.reshape(1, D), g_post.reshape(1, D),
                w_gate.astype(BF16), w_up.astype(BF16), w_down.astype(BF16)]
    in_specs = [rows(D), _const_spec((1, D)), _const_spec((1, D)),
                _const_spec((D, F)), _const_spec((D, F)), _const_spec((F, D))]
    if lead is not None:
        operands.append(lead)
        in_specs.append(pl.BlockSpec((tm, D), lambda i: (i // lead_every, 0)))
    if mix is not None:
        a, b, g_mix, w_out = mix
        Ka, Kb = a.shape[1], b.shape[1]
        w_out = w_out.astype(BF16)
        operands += [a, b, g_mix.reshape(1, D), w_out[:Ka], w_out[Ka:]]
        in_specs += [rows(Ka), rows(Kb), _const_spec((1, D)), _const_spec((Ka, D)), _const_spec((Kb, D))]
    return pl.pallas_call(
        functools.partial(_ffn_kernel, tf=tf, lead_every=lead_every, mixed=mix is not None),
        grid=(n_tiles,),
        in_specs=in_specs,
        out_specs=pl.BlockSpec((tm, D), lambda i: (i, 0)),
        out_shape=jax.ShapeDtypeStruct((n_tiles * tm, D), F32),
        scratch_shapes=[pltpu.VMEM((tm, D), BF16), pltpu.VMEM((tm, F), BF16)],
        compiler_params=_params(1),
        name="ffn",
    )(*operands)


def _shifted_token_rows(tm, S, Lp):
    per_batch = Lp // tm
    return lambda width: pl.BlockSpec(
        (pl.Element(tm), pl.Element(width)),
        lambda i: (BLK * ((i // per_batch) * (S // BLK) + jnp.maximum((i % per_batch) * (tm // BLK) - 1, 0)), 0))


def _token_rows(tm, S, Lp):
    per_batch = S // tm
    return lambda width: pl.BlockSpec(
        (pl.Element(tm), pl.Element(width)),
        lambda i: (BLK * ((i // per_batch) * (Lp // BLK) + 1 + (i % per_batch) * (tm // BLK)), 0))


def _proj_kernel(h_ref, g_ref, w_ref, o_ref, *, tn):
    xn = _rms(h_ref[...], g_ref[...]).astype(BF16)
    for c in range(0, w_ref.shape[1], tn):
        o_ref[:, c:c + tn] = _dot(xn, w_ref[:, c:c + tn]).astype(o_ref.dtype)


def _proj(h, g, w, out_dtype, *, tm=512, tn=256):
    T, D = h.shape
    N = w.shape[1]
    tm = _row_tile(T, tm)
    return pl.pallas_call(
        functools.partial(_proj_kernel, tn=tn),
        grid=(T // tm,),
        in_specs=[pl.BlockSpec((tm, D), lambda i: (i, 0)), _const_spec((1, D)), _const_spec((D, N))],
        out_specs=pl.BlockSpec((tm, N), lambda i: (i, 0)),
        out_shape=jax.ShapeDtypeStruct((T, N), out_dtype),
        compiler_params=_params(1),
        name="in_proj",
    )(h, g.reshape(1, D), w.astype(BF16))


def _proj_ab_kernel(h_ref, g_ref, w_ref, cos_ref, sin_ref, cw_ref, o16_ref, o32_ref, xn_ref, halo_ref, *, tn):
    rows = h_ref.shape[0]
    W = RET_W

    @pl.when(pl.program_id(1) == 0)
    def _start_of_sequence():
        halo_ref[...] = jnp.zeros_like(halo_ref)

    xn_ref[...] = _rms(h_ref[...], g_ref[...]).astype(BF16)
    cos, sin = cos_ref[...], sin_ref[...]
    rot = lambda t: t * cos + pltpu.roll(t, RET_DK // 2, 1) * sin
    l2n = lambda t: t * lax.rsqrt(jnp.sum(t * t, axis=-1, keepdims=True) + NORM_EPS)
    heavy = list(range(4 * W, 7 * W, tn))
    light = [c for c in range(0, AB_GATE_COL, tn) if c not in heavy]
    order = [c for pair in zip(heavy, light) for c in pair] + light[len(heavy):]
    for c in order:
        pre = _dot(xn_ref[...], w_ref[:, c:c + tn])
        group, off = divmod(c, W)
        if group <= 1:
            scale = 1.0 if group == 0 else RET_DK ** -0.5
            for hs in range(0, tn, RET_DK):
                o16_ref[:, c + hs:c + hs + RET_DK] = (rot(pre[:, hs:hs + RET_DK]) * scale).astype(BF16)
        elif group == 2:
            o16_ref[:, c:c + tn] = pre.astype(BF16)
        elif group == 3:
            o32_ref[:, off:off + tn] = _silu(pre)
        elif group <= 6:
            ch = c - 4 * W
            x_ext = jnp.concatenate([halo_ref[:, ch:ch + tn], pre], axis=0)
            y = cw_ref[GDN_CONV - 1:GDN_CONV, ch:ch + tn] * pre
            for shift in range(1, GDN_CONV):
                tap = GDN_CONV - 1 - shift
                y = y + cw_ref[tap:tap + 1, ch:ch + tn] * pltpu.roll(x_ext, shift, 0)[CONV_HALO:]
            halo_ref[:, ch:ch + tn] = pre[rows - CONV_HALO:rows]
            act = _silu(y)
            if group == 6:
                o32_ref[:, W + ch:W + ch + tn] = act
            else:
                scale = GDN_DK ** -0.5 if group == 4 else 1.0
                for hs in range(0, tn, GDN_DK):
                    o32_ref[:, W + ch + hs:W + ch + hs + GDN_DK] = l2n(act[:, hs:hs + GDN_DK]) * scale
        else:
            o32_ref[:, 4 * W + off:4 * W + off + tn] = _silu(pre)
    o32_ref[:, 5 * W:5 * W + BLK] = _dot(xn_ref[...], w_ref[:, AB_GATE_COL:AB_GATE_COL + BLK])


def _proj_ab(h, g, w_in, conv_w, cos, sin, B, Lp, *, tm=640, tn=256):
    D = h.shape[1]
    tm = _row_tile(Lp, tm)
    W = RET_W
    seq = lambda n: pl.BlockSpec((None, tm, n), lambda b, j: (b, j, 0))
    tab = pl.BlockSpec((tm, RET_DK), lambda b, j: (j, 0))
    return pl.pallas_call(
        functools.partial(_proj_ab_kernel, tn=tn),
        grid=(B, Lp // tm),
        in_specs=[seq(D), _const_spec((1, D)), _const_spec((D, AB_COLS)), tab, tab,
                  _const_spec((GDN_CONV, 3 * W))],
        out_specs=[seq(3 * W), seq(5 * W + BLK)],
        out_shape=[jax.ShapeDtypeStruct((B, Lp, 3 * W), BF16), jax.ShapeDtypeStruct((B, Lp, 5 * W + BLK), F32)],
        scratch_shapes=[pltpu.VMEM((tm, D), BF16), pltpu.VMEM((CONV_HALO, 3 * W), F32)],
        compiler_params=_params(2, parallel=False),
        name="in_proj_ab",
    )(h.reshape(B, Lp, D), g.reshape(1, D), _ab_weight(w_in.astype(BF16)), cos, sin, conv_w.astype(F32))


def _ret_kernel(q_ref, k_ref, v_ref, gate_ref, o_ref, s_ref, dec_ref, zx_ref):
    nbatch, C = q_ref.shape[0], q_ref.shape[1]
    n = pl.program_id(0)
    log_gamma = [math.log1p(-2.0 ** (-5.0 - hd)) for hd in range(RET_HEADS)]

    @pl.when(n == 0)
    def _init():
        s_ref[...] = jnp.zeros_like(s_ref)
        diff = (lax.broadcasted_iota(jnp.int32, (C, C), 0) - lax.broadcasted_iota(jnp.int32, (C, C), 1)).astype(F32)
        idx = lax.broadcasted_iota(jnp.int32, (C, RET_DK), 0).astype(F32)
        for hd, lg in enumerate(log_gamma):
            dec_ref[hd] = jnp.where(diff >= 0, jnp.exp(jnp.maximum(diff, 0.0) * lg), 0.0)
            zx_ref[0, hd] = jnp.exp((C - 1.0 - idx) * lg)
            zx_ref[1, hd] = jnp.exp((idx + 1.0) * lg)

    chains = [(b, hd) for b in range(nbatch) for hd in range(RET_HEADS)]
    sl = lambda hd: slice(hd * RET_DK, (hd + 1) * RET_DK)
    q = [q_ref[b, :, sl(hd)] for b, hd in chains]
    k = [k_ref[b, :, sl(hd)] for b, hd in chains]
    v = [v_ref[b, :, sl(hd)] for b, hd in chains]
    scores = [_dot_nt(qq, kk) * dec_ref[hd] for qq, kk, (_, hd) in zip(q, k, chains)]
    intra = [_dot(sc.astype(BF16), vv) for sc, vv in zip(scores, v)]
    state = [s_ref[i] for i in range(len(chains))]
    cross = [_dot(qq, st.astype(BF16)) * zx_ref[1, hd] for qq, st, (_, hd) in zip(q, state, chains)]
    kv = [_dot_tn(kk, (vv.astype(F32) * zx_ref[0, hd]).astype(BF16)) for kk, vv, (_, hd) in zip(k, v, chains)]
    for i, (b, hd) in enumerate(chains):
        s_ref[i] = state[i] * math.exp(C * log_gamma[hd]) + kv[i]
        o = intra[i] + cross[i]
        mu = jnp.mean(o, axis=-1, keepdims=True)
        var = jnp.mean(jnp.square(o - mu), axis=-1, keepdims=True)
        o_ref[b, :, sl(hd)] = ((o - mu) * lax.rsqrt(var + NORM_EPS) * gate_ref[b, :, sl(hd)]).astype(o_ref.dtype)


def _retention(p16, p32, B, Lp):
    nb = Lp // BLK
    col = lambda j: pl.BlockSpec((B, BLK, RET_W), lambda n: (0, n, j))
    out = pl.pallas_call(
        _ret_kernel,
        grid=(nb,),
        in_specs=[col(0), col(1), col(2), col(0)],
        out_specs=pl.BlockSpec((B, BLK, RET_W), lambda n: (0, n, 0)),
        out_shape=jax.ShapeDtypeStruct((B, Lp, RET_W), BF16),
        scratch_shapes=[pltpu.VMEM((B * RET_HEADS, RET_DK, RET_DV), F32),
                        pltpu.VMEM((RET_HEADS, BLK, BLK), F32),
                        pltpu.VMEM((2, RET_HEADS, BLK, RET_DK), F32)],
        compiler_params=_params(1, parallel=False),
        name="retention",
    )(p16, p16, p16, p32)
    return out.reshape(B * Lp, RET_W)


def _unit_lower_inverses(a_mats, eye, same_blk):
    d = [jnp.where(same_blk, a, 0.0) for a in a_mats]
    e = [(a - dd).astype(BF16) for a, dd in zip(a_mats, d)]
    x = [-dd for dd in d]
    t = _product_of_powers([eye + xx for xx in x], x, 3)
    tb = [tt.astype(BF16) for tt in t]
    f = [-_dot(tt, ee) for tt, ee in zip(tb, e)]
    p = _product_of_powers([eye + ff for ff in f], f, 2)
    return [pp.astype(BF16) for pp in p], tb


def _product_of_powers(prod, x, steps):
    size = x[0].shape[1]
    x = [_dot(xb, xb) for xb in [xx.astype(BF16) for xx in x]]
    for step in range(steps):
        xb = [xx.astype(BF16) for xx in x]
        if step == steps - 1:
            return [pp + _dot(xx, pp.astype(BF16)) for pp, xx in zip(prod, xb)]
        both = [_dot(xx, jnp.concatenate([xx, pp.astype(BF16)], axis=1)) for pp, xx in zip(prod, xb)]
        x = [bb[:, :size] for bb in both]
        prod = [pp + bb[:, size:] for pp, bb in zip(prod, both)]


def _gdn_kernel(q_ref, k_ref, v_ref, z_ref, gate_ref, alog_ref, dtb_ref, onorm_ref, o_ref, s_ref):
    nbatch, C = q_ref.shape[0], q_ref.shape[1]
    n = pl.program_id(0)

    @pl.when(n == 0)
    def _init():
        s_ref[...] = jnp.zeros_like(s_ref)

    row = lax.broadcasted_iota(jnp.int32, (C, C), 0)
    col = lax.broadcasted_iota(jnp.int32, (C, C), 1)
    incl = row >= col
    strict = row > col
    same_blk = (row // SOLVE_BLK) == (col // SOLVE_BLK)
    eye = jnp.where(row == col, 1.0, 0.0).astype(F32)
    tri_incl = jnp.where(incl, 1.0, 0.0).astype(F32)
    tri_upper = jnp.where(row <= col, 1.0, 0.0).astype(F32)

    gates = [gate_ref[b] for b in range(nbatch)]
    beta_all = [jax.nn.sigmoid(gt) for gt in gates]
    g_all = [-jnp.exp(alog_ref[...]) * jax.nn.softplus(gt + dtb_ref[...]) for gt in gates]
    gcum_all = [_dot3(tri_incl, g) for g in g_all]
    gcum_t = [_dot3(g, tri_upper, _dot_tn) for g in g_all]

    chains = [(b, hd) for b in range(nbatch) for hd in range(GDN_HEADS)]
    head = lambda ref, b, hd: ref[b, :, hd * GDN_DK:(hd + 1) * GDN_DK]
    q = [head(q_ref, b, hd) for b, hd in chains]
    k = [head(k_ref, b, hd) for b, hd in chains]
    v = [head(v_ref, b, hd) for b, hd in chains]
    beta = [beta_all[b][:, hd:hd + 1] for b, hd in chains]
    gcum = [gcum_all[b][:, GDN_HEADS + hd:GDN_HEADS + hd + 1] for b, hd in chains]
    gcum_row = [gcum_t[b][GDN_HEADS + hd:GDN_HEADS + hd + 1, :] for b, hd in chains]
    decay = [jnp.where(incl, jnp.exp(jnp.where(incl, gc - gr, 0.0)), 0.0) for gc, gr in zip(gcum, gcum_row)]
    k_beta = [kk * bb for kk, bb in zip(k, beta)]
    scores = [_dot_nt(jnp.concatenate([kbt.astype(BF16), qq.astype(BF16)], axis=0), kk.astype(BF16))
              for kbt, qq, kk in zip(k_beta, q, k)]
    a_mat = [jnp.where(strict, sc[:C] * dc, 0.0) for sc, dc in zip(scores, decay)]
    p_mat, t_inv = _unit_lower_inverses(a_mat, eye, same_blk)
    e_gcum = [jnp.exp(gc) for gc in gcum]
    rhs = [jnp.concatenate([vv * bb, kbt * eg], axis=-1) for vv, bb, kbt, eg in zip(v, beta, k_beta, e_gcum)]
    sol = [_dot(tt, rr.astype(BF16)) for tt, rr in zip(t_inv, rhs)]
    sol = [_dot(pp, ss.astype(BF16)) for pp, ss in zip(p_mat, sol)]
    qk = [jnp.where(incl, sc[C:] * dc, 0.0) for sc, dc in zip(scores, decay)]
    g_last = [gc[C - 1:C, :] for gc in gcum]
    k_tail = [(kk * jnp.exp(gl - gc)).astype(BF16) for kk, gl, gc in zip(k, g_last, gcum)]
    q_dec = [(qq * eg).astype(BF16) for qq, eg in zip(q, e_gcum)]

    state = [s_ref[i] for i in range(len(chains))]
    state_b = [st.astype(BF16) for st in state]
    v_new = [ss[:, :GDN_DV] - _dot(ss[:, GDN_DV:].astype(BF16), sb) for ss, sb in zip(sol, state_b)]
    v_new_b = [vn.astype(BF16) for vn in v_new]
    out = [_dot(qd, sb) + _dot(qkm.astype(BF16), vn) for qd, sb, qkm, vn in zip(q_dec, state_b, qk, v_new_b)]
    for i, (b, hd) in enumerate(chains):
        s_ref[i] = state[i] * jnp.exp(g_last[i]) + _dot_tn(k_tail[i], v_new_b[i])
        sl = slice(hd * GDN_DV, (hd + 1) * GDN_DV)
        o_ref[b, :, sl] = (_rms(out[i], onorm_ref[...]) * z_ref[b, :, sl]).astype(o_ref.dtype)


def _gdn(p32, a_log, dt_bias, out_norm, B, Lp):
    C = BLK
    nc = Lp // C
    col = lambda j: pl.BlockSpec((B, C, RET_W), lambda n: (0, n, j))
    gate_lanes = jnp.zeros((1, BLK), F32)
    alog = gate_lanes.at[0, GDN_HEADS:2 * GDN_HEADS].set(a_log.astype(F32))
    dtb = gate_lanes.at[0, GDN_HEADS:2 * GDN_HEADS].set(dt_bias.astype(F32))
    out = pl.pallas_call(
        _gdn_kernel,
        grid=(nc,),
        in_specs=[col(1), col(2), col(3), col(4),
                  pl.BlockSpec((B, C, BLK), lambda n: (0, n, 5 * RET_W // BLK)),
                  _const_spec((1, BLK)), _const_spec((1, BLK)), _const_spec((1, GDN_DV))],
        out_specs=pl.BlockSpec((B, C, RET_W), lambda n: (0, n, 0)),
        out_shape=jax.ShapeDtypeStruct((B, Lp, RET_W), BF16),
        scratch_shapes=[pltpu.VMEM((B * GDN_HEADS, GDN_DK, GDN_DV), F32)],
        compiler_params=_params(1, parallel=False),
        name="gated_deltanet",
    )(p32, p32, p32, p32, p32, alog, dtb, out_norm.astype(F32).reshape(1, GDN_DV))
    return out.reshape(B * Lp, RET_W)


def _swa_head_perm():
    G = SWA_HEADS // SWA_KV_HEADS
    heads = [kv * G + b for b in range(G) for kv in range(SWA_KV_HEADS)]
    return np.concatenate([np.arange(h * SWA_DH, (h + 1) * SWA_DH) for h in heads])


def _swa_kernel(q_ref, kc_ref, vc_ref, kp_ref, vp_ref, km_ref, vm_ref, sink_ref, o_ref, *, S):
    n = pl.program_id(1)
    G = SWA_HEADS // SWA_KV_HEADS
    R = G * BLK
    r = lax.broadcasted_iota(jnp.int32, (R, BLK), 0) & (BLK - 1)
    col = lax.broadcasted_iota(jnp.int32, (R, BLK), 1)
    lower = col <= r
    upper = col > r
    is_meta = col >= PAD
    halves = [col < SWA_DH, col >= SWA_DH]
    k_meta, v_meta = km_ref[...], vm_ref[...]
    for s in range(S):
        blk = n * S + s
        rows = slice(s * BLK, (s + 1) * BLK)
        k_cur, v_cur = kc_ref[rows, :], vc_ref[rows, :]
        if s == 0:
            k_prev, v_prev = kp_ref[...], vp_ref[...]
        else:
            k_prev, v_prev = kc_ref[(s - 1) * BLK:s * BLK, :], vc_ref[(s - 1) * BLK:s * BLK, :]
        cur_ok = lower & (blk >= 1)
        prev_ok = upper & (blk >= 2)
        meta_ok = is_meta & ((blk >= 1) | lower)
        q_st = jnp.concatenate([q_ref[rows, b * BLK:(b + 1) * BLK] for b in range(G)], axis=0) * SWA_DH ** -0.5
        outs = []
        for kv in range(SWA_KV_HEADS):
            q = jnp.where(halves[kv], q_st, 0)
            s_cur = jnp.where(cur_ok, _dot_nt(q, k_cur), NEG_BIG)
            s_prev = jnp.where(prev_ok, _dot_nt(q, k_prev), NEG_BIG)
            s_meta = jnp.where(meta_ok, _dot_nt(q, k_meta), NEG_BIG)
            sink = jnp.concatenate([jnp.full((BLK, 1), sink_ref[kv * G + b], F32) for b in range(G)], axis=0)
            m = jnp.maximum(jnp.max(jnp.maximum(jnp.maximum(s_cur, s_prev), s_meta), axis=-1, keepdims=True), sink)
            p_cur = jnp.exp(s_cur - m)
            p_prev = jnp.exp(s_prev - m)
            p_meta = jnp.exp(s_meta - m)
            denom = jnp.sum(p_cur + p_prev + p_meta, axis=-1, keepdims=True) + jnp.exp(sink - m)
            o = (_dot(p_cur.astype(BF16), v_cur) + _dot(p_prev.astype(BF16), v_prev)
                 + _dot(p_meta.astype(BF16), v_meta))
            outs.append(o / denom)
        o = jnp.where(halves[0], outs[0], outs[1])
        for b in range(G):
            o_ref[rows, b * BLK:(b + 1) * BLK] = o[b * BLK:(b + 1) * BLK, :].astype(o_ref.dtype)


def _swa(proj, sinks, B, Lp):
    T = proj.shape[0]
    nb = Lp // BLK
    S = _blocks_per_step(nb, (5,))
    ns = nb // S
    k_col, v_col = RET_W // BLK, RET_W // BLK + 1
    cur = lambda j: pl.BlockSpec((S * BLK, BLK), lambda b, n: (b * ns + n, j))
    prev = lambda j: pl.BlockSpec((BLK, BLK), lambda b, n: (b * nb + jnp.maximum(n * S - 1, 0), j))
    first = lambda j: pl.BlockSpec((BLK, BLK), lambda b, n: (b * nb, j))
    return pl.pallas_call(
        functools.partial(_swa_kernel, S=S),
        grid=(B, ns),
        in_specs=[pl.BlockSpec((S * BLK, RET_W), lambda b, n: (b * ns + n, 0)),
                  cur(k_col), cur(v_col), prev(k_col), prev(v_col), first(k_col), first(v_col),
                  pl.BlockSpec(memory_space=pltpu.SMEM)],
        out_specs=pl.BlockSpec((S * BLK, RET_W), lambda b, n: (b * ns + n, 0)),
        out_shape=jax.ShapeDtypeStruct((T, RET_W), BF16),
        compiler_params=_params(2),
        name="swa_sink",
    )(proj, proj, proj, proj, proj, proj, proj, sinks.astype(F32))


def _sb_kernel(q_ref, k_ref, v_ref, o_ref, acc_ref, run_ref, *, S):
    n = pl.program_id(2)
    heads = BLK // SB_DH
    row = lax.broadcasted_iota(jnp.int32, (BLK, BLK), 0)
    col = lax.broadcasted_iota(jnp.int32, (BLK, BLK), 1)
    suffix = jnp.where(row > col, 1.0, 0.0).astype(BF16)
    ones = jnp.ones((BLK, BLK), BF16)
    sum_rhs = jnp.concatenate([jnp.concatenate([suffix, ones], axis=1)] * 2, axis=0)
    lane = lax.broadcasted_iota(jnp.int32, (S * BLK, BLK), 1)
    q_all = q_ref[...] * SB_DH ** -0.5
    q_head = [jnp.where((lane >= hh * SB_DH) & (lane < (hh + 1) * SB_DH), q_all, 0).reshape(S, BLK, BLK)
              for hh in range(heads)]

    def diagonal(hh, k, v, valid, run_in):
        z = jnp.einsum("sqd,skd->sqk", q_head[hh], k, preferred_element_type=F32)
        softplus_neg = jnp.log(1.0 + jnp.exp(-jnp.abs(z)))
        log_beta = jnp.minimum(z, 0.0) - softplus_neg
        log_1m = log_beta - z
        if valid is not None:
            log_1m = jnp.where(valid, log_1m, 0.0)
        hi, lo = _split(log_1m)
        sums = _dot(jnp.concatenate([hi, lo], axis=-1).reshape(S * BLK, 2 * BLK), sum_rhs)
        sums = sums.reshape(S, BLK, 2 * BLK)
        log_stick = sums[..., :BLK] if run_in is None else sums[..., :BLK] + run_in
        a = jnp.exp(log_beta + log_stick)
        if valid is not None:
            a = jnp.where(valid, a, 0.0)
        av = jnp.einsum("sqk,skd->sqd", a.astype(BF16), v, preferred_element_type=F32)
        run = sums[..., BLK:] if run_in is None else run_in + sums[..., BLK:]
        return av, run

    sub_block = lax.broadcasted_iota(jnp.int32, (S, BLK, BLK), 0)

    def keep_going(runs, next_d):
        alive = n * S + sub_block >= next_d
        top = jnp.max(jnp.where(alive, runs[0], NEG_BIG))
        for run in runs[1:]:
            top = jnp.maximum(top, jnp.max(jnp.where(alive, run, NEG_BIG)))
        return (top >= SB_SKIP).astype(jnp.int32)

    def slab(ref, first_blk):
        at = pl.multiple_of(first_blk * BLK, BLK)
        return ref[pl.ds(at, S * BLK), :].reshape(S, BLK, BLK)

    lead = min(SB_LEAD, S)

    def leading(at_sequence_start):
        k_diag, v_diag = slab(k_ref, n * S), slab(v_ref, n * S)
        k_before, v_before = [], []
        for back in range(lead - 1, 0, -1):
            at = pl.multiple_of(jnp.maximum(n * S - back, 0) * BLK, BLK)
            k_before.append(k_ref[pl.ds(at, BLK), :][None])
            v_before.append(v_ref[pl.ds(at, BLK), :][None])
        accs, runs = [None] * heads, [None] * heads
        for d in range(lead):
            k = jnp.concatenate(k_before[lead - 1 - d:] + [k_diag[:S - d]], axis=0)
            v = jnp.concatenate(v_before[lead - 1 - d:] + [v_diag[:S - d]], axis=0)
            if at_sequence_start:
                ok = jnp.stack([(n * S + s - d) * BLK + col >= PAD for s in range(S)])
                ok = ok & (col < row) if d == 0 else ok
            else:
                ok = jnp.broadcast_to(col < row, (S, BLK, BLK)) if d == 0 else None
            for hh in range(heads):
                av, runs[hh] = diagonal(hh, k, v, ok, runs[hh])
                accs[hh] = av if d == 0 else accs[hh] + av
        for hh in range(heads):
            acc_ref[hh] = accs[hh]
            run_ref[hh] = runs[hh]
        return keep_going(runs, lead)

    go = lax.cond(n * S <= lead - 1, lambda: leading(True), lambda: leading(False))

    def cond(carry):
        d, go = carry
        return (d <= n * S + S - 1) & (go > 0)

    def body(carry):
        d, _ = carry
        first_blk = n * S - d

        def further(k, v, valid):
            runs = []
            for hh in range(heads):
                av, run = diagonal(hh, k, v, valid, run_ref[hh])
                acc_ref[hh] += av
                run_ref[hh] = run
                runs.append(run)
            return keep_going(runs, d + 1)

        def interior():
            return further(slab(k_ref, first_blk), slab(v_ref, first_blk), None)

        def edge():
            ks, vs, valids = [], [], []
            for s in range(S):
                jb = first_blk + s
                at = pl.multiple_of(jnp.maximum(jb, 0) * BLK, BLK)
                ks.append(k_ref[pl.ds(at, BLK), :])
                vs.append(v_ref[pl.ds(at, BLK), :])
                valids.append(jb * BLK + col >= PAD)
            return further(jnp.stack(ks), jnp.stack(vs), jnp.stack(valids))

        return d + 1, lax.cond(first_blk >= 1, interior, edge)

    lax.while_loop(cond, body, (jnp.int32(lead), go))
    o = acc_ref[0]
    for hh in range(1, heads):
        o = jnp.where(lane.reshape(S, BLK, BLK) >= hh * SB_DH, acc_ref[hh], o)
    o_ref[...] = o.reshape(S * BLK, BLK).astype(o_ref.dtype)


def _stick_breaking(proj, B, Lp):
    T = proj.shape[0]
    nb = Lp // BLK
    S = _blocks_per_step(nb, (13, 5))
    ns = nb // S
    pairs = SB_HEADS * SB_DH // BLK
    heads = BLK // SB_DH
    q0 = (SWA_HEADS + 2 * SWA_KV_HEADS) * SWA_DH // BLK
    k0, v0 = q0 + pairs, q0 + 2 * pairs
    proj3 = proj.reshape(B, Lp, proj.shape[1])
    seq = lambda c0: pl.BlockSpec((None, Lp, BLK), lambda b, p, n: (b, 0, c0 + p))
    return pl.pallas_call(
        functools.partial(_sb_kernel, S=S),
        grid=(B, pairs, ns),
        in_specs=[pl.BlockSpec((S * BLK, BLK), lambda b, p, n: (b * ns + n, q0 + p)), seq(k0), seq(v0)],
        out_specs=pl.BlockSpec((S * BLK, BLK), lambda b, p, n: (b * ns + n, p)),
        out_shape=jax.ShapeDtypeStruct((T, RET_W), BF16),
        scratch_shapes=[pltpu.VMEM((heads, S, BLK, BLK), F32), pltpu.VMEM((heads, S, BLK, BLK), F32)],
        compiler_params=_params(3),
        name="stick_breaking",
    )(proj, proj3, proj3)


def _rotation_tables(Lp):
    half = RET_DK // 2
    inv_freq = 1.0 / (10000.0 ** jnp.linspace(0.0, 1.0, half, dtype=F32))
    pos = jnp.arange(Lp, dtype=F32) - float(PAD)
    ang = pos[:, None] * inv_freq[None, :]
    cos, sin = jnp.cos(ang), jnp.sin(ang)
    return jnp.concatenate([cos, cos], axis=1), jnp.concatenate([-sin, sin], axis=1)


def _ab_weight(w_in):
    D = w_in.shape[0]
    perm = np.concatenate([np.arange(0, RET_DK, 2), np.arange(1, RET_DK, 2)])
    qk_perm = np.concatenate([h * RET_DK + perm for h in range(RET_HEADS)])
    cols = np.concatenate([qk_perm, RET_W + qk_perm, np.arange(2 * RET_W, w_in.shape[1])])
    return jnp.concatenate([w_in[:, cols], jnp.zeros((D, AB_COLS - w_in.shape[1]), w_in.dtype)], axis=1)


def kernel(x, meta_tokens, norm_gains, ffn_w_gate, ffn_w_up, ffn_w_down, ab_w_in, ab_conv_w, ab_a_log, ab_dt_bias, ab_out_norm, ab_w_out, cd_w_in, cd_sinks, cd_w_out):
    B, S, D = x.shape
    Lp = S + BLK
    depth = norm_gains.shape[0]
    meta = jnp.broadcast_to(meta_tokens[None].astype(x.dtype), (B, N_META, D))
    tm = _row_tile(B * Lp, 512)
    tm_tok = _row_tile(S, 512)
    token_rows = _token_rows(tm_tok, S, Lp)
    tm_seq = _row_tile(Lp, 640)
    lead_tiles = jnp.concatenate([jnp.zeros((B, PAD, D), x.dtype), meta, x[:, :tm_seq - BLK]], axis=1)
    cos, sin = _rotation_tables(Lp)
    swa_perm = _swa_head_perm()
    swa_w = SWA_HEADS * SWA_DH
    for i in range(depth):
        g = norm_gains[i]
        j = i // 2
        ffn1 = (g[0], g[1], ffn_w_gate[i, 0], ffn_w_up[i, 0], ffn_w_down[i, 0])
        if i == 0:
            h = _ffn(x.reshape(B * S, D), *ffn1, tm=tm_seq, rows=_shifted_token_rows(tm_seq, S, Lp),
                     n_tiles=B * Lp // tm_seq, lead=lead_tiles.reshape(B * tm_seq, D), lead_every=Lp // tm_seq)
        else:
            h = _ffn(h, *ffn1, tm=tm)
        if i % 2 == 0:
            p16, p32 = _proj_ab(h, g[2], ab_w_in[j], ab_conv_w[j], cos, sin, B, Lp)
            mix_a = _retention(p16, p32, B, Lp)
            mix_b = _gdn(p32, ab_a_log[j], ab_dt_bias[j], ab_out_norm[j], B, Lp)
            w_out = ab_w_out[j]
        else:
            w_in, w_out = cd_w_in[j].astype(BF16), cd_w_out[j].astype(BF16)
            w_in = jnp.concatenate([w_in[:, swa_perm], w_in[:, swa_w:]], axis=1)
            w_out = jnp.concatenate([w_out[swa_perm], w_out[swa_w:]], axis=0)
            proj = _proj(h, g[2], w_in, BF16, tm=640)
            mix_a = _swa(proj, cd_sinks[j], B, Lp)
            mix_b = _stick_breaking(proj, B, Lp)
        ffn2 = (g[4], g[5], ffn_w_gate[i, 1], ffn_w_up[i, 1], ffn_w_down[i, 1])
        mix = (mix_a, mix_b, g[3], w_out)
        if i == depth - 1:
            h = _ffn(h, *ffn2, tm=tm_tok, rows=token_rows, n_tiles=B * S // tm_tok, mix=mix)
        else:
            h = _ffn(h, *ffn2, tm=tm, mix=mix)
    return h.reshape(B, S, D)
```

```python
import functools
import math

import jax
import jax.numpy as jnp
from jax import lax
from jax.experimental import pallas as pl
from jax.experimental.pallas import tpu as pltpu

F32 = jnp.float32
BF16 = jnp.bfloat16

N_META = 16
NORM_EPS = 1e-6
BLK = 128
PAD = BLK - N_META

RET_HEADS, RET_DK, RET_DV = 4, 128, 128
GDN_HEADS, GDN_DK, GDN_DV, GDN_CONV = 4, 128, 128, 4
SWA_HEADS, SWA_KV_HEADS, SWA_DH = 8, 2, 64
SB_HEADS, SB_DH = 8, 64
SOLVE_BLK = 16

RET_W = RET_HEADS * RET_DK
AB_GATE_COL = 4096
AB_COLS = AB_GATE_COL + BLK
CONV_HALO = 8

VMEM_LIMIT = 56 * 1024 * 1024
NEG_BIG = -1e30
SB_SKIP = -87.5
SB_LEAD = 3


def _rms(x, g):
    return x * lax.rsqrt(jnp.mean(x * x, axis=-1, keepdims=True) + NORM_EPS) * g


def _silu(x):
    return x * jax.nn.sigmoid(x)


def _dot(a, b):
    return jnp.dot(a, b, preferred_element_type=F32)


def _dot_nt(a, b):
    return lax.dot_general(a, b, (((1,), (1,)), ((), ())), preferred_element_type=F32)


def _dot_tn(a, b):
    return lax.dot_general(a, b, (((0,), (0,)), ((), ())), preferred_element_type=F32)


def _split(a):
    hi = a.astype(BF16)
    return hi, (a - hi.astype(F32)).astype(BF16)


def _dot3(a, b, dot=_dot):
    ah, al = _split(a)
    bh, bl = _split(b)
    return dot(ah, bh) + (dot(ah, bl) + dot(al, bh))


def _params(n_grid, parallel=True):
    sem = ("parallel",) + ("arbitrary",) * (n_grid - 1) if parallel else ("arbitrary",) * n_grid
    return pltpu.CompilerParams(dimension_semantics=sem, vmem_limit_bytes=VMEM_LIMIT)


def _const_spec(shape):
    nd = len(shape)
    return pl.BlockSpec(shape, lambda *_: (0,) * nd, pipeline_mode=pl.Buffered(1))


def _row_tile(rows, target):
    tile = target
    while rows % tile:
        tile -= BLK
    return tile


def _blocks_per_step(nb, prefs):
    for s in prefs:
        if nb % s == 0:
            return s
    return 1


def _ffn_kernel(h_ref, gpre_ref, gpost_ref, wg_ref, wu_ref, wd_ref, *rest, tf, lead_every, mixed):
    o_ref, xn_ref, act_ref = rest[-3:]
    h = h_ref[...]
    if lead_every:
        h = jnp.where(pl.program_id(0) % lead_every == 0, rest[0][...], h)
    if mixed:
        a_ref, b_ref, gmix_ref, wa_ref, wb_ref = rest[:5]
        h = h + _rms(_dot(a_ref[...], wa_ref[...]) + _dot(b_ref[...], wb_ref[...]), gmix_ref[...])
    xn_ref[...] = _rms(h, gpre_ref[...]).astype(BF16)
    for c in range(0, wg_ref.shape[1], tf):
        xn = xn_ref[...]
        g = _dot(xn, wg_ref[:, c:c + tf])
        u = _dot(xn, wu_ref[:, c:c + tf])
        act_ref[:, c:c + tf] = (_silu(g) * u).astype(BF16)
    y = _dot(act_ref[...], wd_ref[...])
    o_ref[...] = h + 0.5 * _rms(y, gpost_ref[...])


def _ffn(src, g_pre, g_post, w_gate, w_up, w_down, *, tm, rows=None, n_tiles=None, lead=None, lead_every=0,
         mix=None, tf=256):
    D = src.shape[1]
    F = w_gate.shape[1]
    n_tiles = n_tiles or src.shape[0] // tm
    rows = rows or (lambda width: pl.BlockSpec((tm, width), lambda i: (i, 0)))
    operands = [src, g_pre.reshape(1, D), g_post.reshape(1, D),
                w_gate.astype(BF16), w_up.astype(BF16), w_down.astype(BF16)]
    in_specs = [rows(D), _const_spec((1, D)), _const_spec((1, D)),
                _const_spec((D, F)), _const_spec((D, F)), _const_spec((F, D))]
    if lead is not None:
        operands.append(lead)
        in_specs.append(pl.BlockSpec((tm, D), lambda i: (i // lead_every, 0)))
    if mix is not None:
        a, b, g_mix, w_out = mix
        Ka, Kb = a.shape[1], b.shape[1]
        w_out = w_out.astype(BF16)
        operands += [a, b, g_mix.reshape(1, D), w_out[:Ka], w_out[Ka:]]
        in_specs += [rows(Ka), rows(Kb), _const_spec((1, D)), _const_spec((Ka, D)), _const_spec((Kb, D))]
    return pl.pallas_call(
        functools.partial(_ffn_kernel, tf=tf, lead_every=lead_every, mixed=mix is not None),
        grid=(n_tiles,),
        in_specs=in_specs,
        out_specs=pl.BlockSpec((tm, D), lambda i: (i, 0)),
        out_shape=jax.ShapeDtypeStruct((n_tiles * tm, D), F32),
        scratch_shapes=[pltpu.VMEM((tm, D), BF16), pltpu.VMEM((tm, F), BF16)],
        compiler_params=_params(1),
        name="ffn",
    )(*operands)


def _shifted_token_rows(tm, S, Lp):
    per_batch = Lp // tm
    return lambda width: pl.BlockSpec(
        (pl.Element(tm), pl.Element(width)),
        lambda i: (BLK * ((i // per_batch) * (S // BLK) + jnp.maximum((i % per_batch) * (tm // BLK) - 1, 0)), 0))


def _token_rows(tm, S, Lp):
    per_batch = S // tm
    return lambda width: pl.BlockSpec(
        (pl.Element(tm), pl.Element(width)),
        lambda i: (BLK * ((i // per_batch) * (Lp // BLK) + 1 + (i % per_batch) * (tm // BLK)), 0))


def _proj_kernel(h_ref, g_ref, w_ref, o_ref, *, tn):
    xn = _rms(h_ref[...], g_ref[...]).astype(BF16)
    for c in range(0, w_ref.shape[1], tn):
        o_ref[:, c:c + tn] = _dot(xn, w_ref[:, c:c + tn]).astype(o_ref.dtype)


def _proj(h, g, w, out_dtype, *, tm=512, tn=256):
    T, D = h.shape
    N = w.shape[1]
    tm = _row_tile(T, tm)
    return pl.pallas_call(
        functools.partial(_proj_kernel, tn=tn),
        grid=(T // tm,),
        in_specs=[pl.BlockSpec((tm, D), lambda i: (i, 0)), _const_spec((1, D)), _const_spec((D, N))],
        out_specs=pl.BlockSpec((tm, N), lambda i: (i, 0)),
        out_shape=jax.ShapeDtypeStruct((T, N), out_dtype),
        compiler_params=_params(1),
        name="in_proj",
    )(h, g.reshape(1, D), w.astype(BF16))


def _proj_ab_kernel(h_ref, g_ref, w_ref, cos_ref, sin_ref, cw_ref, o16_ref, o32_ref, xn_ref, halo_ref, *, tn):
    rows = h_ref.shape[0]
    W = RET_W

    @pl.when(pl.program_id(1) == 0)
    def _start_of_sequence():
        halo_ref[...] = jnp.zeros_like(halo_ref)

    xn_ref[...] = _rms(h_ref[...], g_ref[...]).astype(BF16)
    cos, sin = cos_ref[...], sin_ref[...]
    rot = lambda t: t * cos + pltpu.roll(t, RET_DK // 2, 1) * sin
    l2n = lambda t: t * lax.rsqrt(jnp.sum(t * t, axis=-1, keepdims=True) + NORM_EPS)
    heavy = list(range(4 * W, 7 * W, tn))
    light = [c for c in range(0, AB_GATE_COL, tn) if c not in heavy]
    order = [c for pair in zip(heavy, light) for c in pair] + light[len(heavy):]
    for c in order:
        pre = _dot(xn_ref[...], w_ref[:, c:c + tn])
        group, off = divmod(c, W)
        if group <= 1:
            scale = 1.0 if group == 0 else RET_DK ** -0.5
            for hs in range(0, tn, RET_DK):
                o16_ref[:, c + hs:c + hs + RET_DK] = (rot(pre[:, hs:hs + RET_DK]) * scale).astype(BF16)
        elif group == 2:
            o16_ref[:, c:c + tn] = pre.astype(BF16)
        elif group == 3:
            o32_ref[:, off:off + tn] = _silu(pre)
        elif group <= 6:
            ch = c - 4 * W
            x_ext = jnp.concatenate([halo_ref[:, ch:ch + tn], pre], axis=0)
            y = cw_ref[GDN_CONV - 1:GDN_CONV, ch:ch + tn] * pre
            for shift in range(1, GDN_CONV):
                tap = GDN_CONV - 1 - shift
                y = y + cw_ref[tap:tap + 1, ch:ch + tn] * pltpu.roll(x_ext, shift, 0)[CONV_HALO:]
            halo_ref[:, ch:ch + tn] = pre[rows - CONV_HALO:rows]
            act = _silu(y)
            if group == 6:
                o32_ref[:, W + ch:W + ch + tn] = act
            else:
                scale = GDN_DK ** -0.5 if group == 4 else 1.0
                for hs in range(0, tn, GDN_DK):
                    o32_ref[:, W + ch + hs:W + ch + hs + GDN_DK] = l2n(act[:, hs:hs + GDN_DK]) * scale
        else:
            o32_ref[:, 4 * W + off:4 * W + off + tn] = _silu(pre)
    o32_ref[:, 5 * W:5 * W + BLK] = _dot(xn_ref[...], w_ref[:, AB_GATE_COL:AB_GATE_COL + BLK])


def _proj_ab(h, g, w_in, conv_w, cos, sin, B, Lp, *, tm=640, tn=256):
    D = h.shape[1]
    tm = _row_tile(Lp, tm)
    W = RET_W
    seq = lambda n: pl.BlockSpec((None, tm, n), lambda b, j: (b, j, 0))
    tab = pl.BlockSpec((tm, RET_DK), lambda b, j: (j, 0))
    return pl.pallas_call(
        functools.partial(_proj_ab_kernel, tn=tn),
        grid=(B, Lp // tm),
        in_specs=[seq(D), _const_spec((1, D)), _const_spec((D, AB_COLS)), tab, tab,
                  _const_spec((GDN_CONV, 3 * W))],
        out_specs=[seq(3 * W), seq(5 * W + BLK)],
        out_shape=[jax.ShapeDtypeStruct((B, Lp, 3 * W), BF16), jax.ShapeDtypeStruct((B, Lp, 5 * W + BLK), F32)],
        scratch_shapes=[pltpu.VMEM((tm, D), BF16), pltpu.VMEM((CONV_HALO, 3 * W), F32)],
        compiler_params=_params(2, parallel=False),
        name="in_proj_ab",
    )(h.reshape(B, Lp, D), g.reshape(1, D), _ab_weight(w_in.astype(BF16)), cos, sin, conv_w.astype(F32))


def _ret_kernel(q_ref, k_ref, v_ref, gate_ref, o_ref, s_ref, dec_ref, zx_ref):
    nbatch, C = q_ref.shape[0], q_ref.shape[1]
    n = pl.program_id(0)
    log_gamma = [math.log1p(-2.0 ** (-5.0 - hd)) for hd in range(RET_HEADS)]

    @pl.when(n == 0)
    def _init():
        s_ref[...] = jnp.zeros_like(s_ref)
        diff = (lax.broadcasted_iota(jnp.int32, (C, C), 0) - lax.broadcasted_iota(jnp.int32, (C, C), 1)).astype(F32)
        idx = lax.broadcasted_iota(jnp.int32, (C, RET_DK), 0).astype(F32)
        for hd, lg in enumerate(log_gamma):
            dec_ref[hd] = jnp.where(diff >= 0, jnp.exp(jnp.maximum(diff, 0.0) * lg), 0.0)
            zx_ref[0, hd] = jnp.exp((C - 1.0 - idx) * lg)
            zx_ref[1, hd] = jnp.exp((idx + 1.0) * lg)

    chains = [(b, hd) for b in range(nbatch) for hd in range(RET_HEADS)]
    sl = lambda hd: slice(hd * RET_DK, (hd + 1) * RET_DK)
    q = [q_ref[b, :, sl(hd)] for b, hd in chains]
    k = [k_ref[b, :, sl(hd)] for b, hd in chains]
    v = [v_ref[b, :, sl(hd)] for b, hd in chains]
    scores = [_dot_nt(qq, kk) * dec_ref[hd] for qq, kk, (_, hd) in zip(q, k, chains)]
    intra = [_dot(sc.astype(BF16), vv) for sc, vv in zip(scores, v)]
    state = [s_ref[i] for i in range(len(chains))]
    cross = [_dot(qq, st.astype(BF16)) * zx_ref[1, hd] for qq, st, (_, hd) in zip(q, state, chains)]
    kv = [_dot_tn(kk, (vv.astype(F32) * zx_ref[0, hd]).astype(BF16)) for kk, vv, (_, hd) in zip(k, v, chains)]
    for i, (b, hd) in enumerate(chains):
        s_ref[i] = state[i] * math.exp(C * log_gamma[hd]) + kv[i]
        o = intra[i] + cross[i]
        mu = jnp.mean(o, axis=-1, keepdims=True)
        var = jnp.mean(jnp.square(o - mu), axis=-1, keepdims=True)
        o_ref[b, :, sl(hd)] = ((o - mu) * lax.rsqrt(var + NORM_EPS) * gate_ref[b, :, sl(hd)]).astype(o_ref.dtype)


def _retention(p16, p32, B, Lp):
    nb = Lp // BLK
    col = lambda j: pl.BlockSpec((B, BLK, RET_W), lambda n: (0, n, j))
    out = pl.pallas_call(
        _ret_kernel,
        grid=(nb,),
        in_specs=[col(0), col(1), col(2), col(0)],
        out_specs=pl.BlockSpec((B, BLK, RET_W), lambda n: (0, n, 0)),
        out_shape=jax.ShapeDtypeStruct((B, Lp, RET_W), BF16),
        scratch_shapes=[pltpu.VMEM((B * RET_HEADS, RET_DK, RET_DV), F32),
                        pltpu.VMEM((RET_HEADS, BLK, BLK), F32),
                        pltpu.VMEM((2, RET_HEADS, BLK, RET_DK), F32)],
        compiler_params=_params(1, parallel=False),
        name="retention",
    )(p16, p16, p16, p32)
    return out.reshape(B * Lp, RET_W)


def _unit_lower_inverses(a_mats, eye, same_blk):
    d = [jnp.where(same_blk, a, 0.0) for a in a_mats]
    e = [(a - dd).astype(BF16) for a, dd in zip(a_mats, d)]
    x = [-dd for dd in d]
    t = _product_of_powers([eye + xx for xx in x], x, 3)
    tb = [tt.astype(BF16) for tt in t]
    f = [-_dot(tt, ee) for tt, ee in zip(tb, e)]
    p = _product_of_powers([eye + ff for ff in f], f, 2)
    return [pp.astype(BF16) for pp in p], tb


def _product_of_powers(prod, x, steps):
    size = x[0].shape[1]
    x = [_dot(xb, xb) for xb in [xx.astype(BF16) for xx in x]]
    for step in range(steps):
        xb = [xx.astype(BF16) for xx in x]
        if step == steps - 1:
            return [pp + _dot(xx, pp.astype(BF16)) for pp, xx in zip(prod, xb)]
        both = [_dot(xx, jnp.concatenate([xx, pp.astype(BF16)], axis=1)) for pp, xx in zip(prod, xb)]
        x = [bb[:, :size] for bb in both]
        prod = [pp + bb[:, size:] for pp, bb in zip(prod, both)]


def _gdn_kernel(q_ref, k_ref, v_ref, z_ref, gate_ref, alog_ref, dtb_ref, onorm_ref, o_ref, s_ref):
    nbatch, C = q_ref.shape[0], q_ref.shape[1]
    n = pl.program_id(0)

    @pl.when(n == 0)
    def _init():
        s_ref[...] = jnp.zeros_like(s_ref)

    row = lax.broadcasted_iota(jnp.int32, (C, C), 0)
    col = lax.broadcasted_iota(jnp.int32, (C, C), 1)
    incl = row >= col
    strict = row > col
    same_blk = (row // SOLVE_BLK) == (col // SOLVE_BLK)
    eye = jnp.where(row == col, 1.0, 0.0).astype(F32)
    tri_incl = jnp.where(incl, 1.0, 0.0).astype(F32)
    tri_upper = jnp.where(row <= col, 1.0, 0.0).astype(F32)

    gates = [gate_ref[b] for b in range(nbatch)]
    beta_all = [jax.nn.sigmoid(gt) for gt in gates]
    g_all = [-jnp.exp(alog_ref[...]) * jax.nn.softplus(gt + dtb_ref[...]) for gt in gates]
    gcum_all = [_dot3(tri_incl, g) for g in g_all]
    gcum_t = [_dot3(g, tri_upper, _dot_tn) for g in g_all]

    chains = [(b, hd) for b in range(nbatch) for hd in range(GDN_HEADS)]
    head = lambda ref, b, hd: ref[b, :, hd * GDN_DK:(hd + 1) * GDN_DK]
    q = [head(q_ref, b, hd) for b, hd in chains]
    k = [head(k_ref, b, hd) for b, hd in chains]
    v = [head(v_ref, b, hd) for b, hd in chains]
    beta = [beta_all[b][:, hd:hd + 1] for b, hd in chains]
    gcum = [gcum_all[b][:, GDN_HEADS + hd:GDN_HEADS + hd + 1] for b, hd in chains]
    gcum_row = [gcum_t[b][GDN_HEADS + hd:GDN_HEADS + hd + 1, :] for b, hd in chains]
    decay = [jnp.where(incl, jnp.exp(jnp.where(incl, gc - gr, 0.0)), 0.0) for gc, gr in zip(gcum, gcum_row)]
    k_beta = [kk * bb for kk, bb in zip(k, beta)]
    scores = [_dot_nt(jnp.concatenate([kbt.astype(BF16), qq.astype(BF16)], axis=0), kk.astype(BF16))
              for kbt, qq, kk in zip(k_beta, q, k)]
    a_mat = [jnp.where(strict, sc[:C] * dc, 0.0) for sc, dc in zip(scores, decay)]
    p_mat, t_inv = _unit_lower_inverses(a_mat, eye, same_blk)
    e_gcum = [jnp.exp(gc) for gc in gcum]
    rhs = [jnp.concatenate([vv * bb, kbt * eg], axis=-1) for vv, bb, kbt, eg in zip(v, beta, k_beta, e_gcum)]
    sol = [_dot(tt, rr.astype(BF16)) for tt, rr in zip(t_inv, rhs)]
    sol = [_dot(pp, ss.astype(BF16)) for pp, ss in zip(p_mat, sol)]
    qk = [jnp.where(incl, sc[C:] * dc, 0.0) for sc, dc in zip(scores, decay)]
    g_last = [gc[C - 1:C, :] for gc in gcum]
    k_tail = [(kk * jnp.exp(gl - gc)).astype(BF16) for kk, gl, gc in zip(k, g_last, gcum)]
    q_dec = [(qq * eg).astype(BF16) for qq, eg in zip(q, e_gcum)]

    state = [s_ref[i] for i in range(len(chains))]
    state_b = [st.astype(BF16) for st in state]
    v_new = [ss[:, :GDN_DV] - _dot(ss[:, GDN_DV:].astype(BF16), sb) for ss, sb in zip(sol, state_b)]
    v_new_b = [vn.astype(BF16) for vn in v_new]
    out = [_dot(qd, sb) + _dot(qkm.astype(BF16), vn) for qd, sb, qkm, vn in zip(q_dec, state_b, qk, v_new_b)]
    for i, (b, hd) in enumerate(chains):
        s_ref[i] = state[i] * jnp.exp(g_last[i]) + _dot_tn(k_tail[i], v_new_b[i])
        sl = slice(hd * GDN_DV, (hd + 1) * GDN_DV)
        o_ref[b, :, sl] = (_rms(out[i], onorm_ref[...]) * z_ref[b, :, sl]).astype(o_ref.dtype)


def _gdn(p32, a_log, dt_bias, out_norm, B, Lp):
    C = BLK
    nc = Lp // C
    col = lambda j: pl.BlockSpec((B, C, RET_W), lambda n: (0, n, j))
    gate_lanes = jnp.zeros((1, BLK), F32)
    alog = gate_lanes.at[0, GDN_HEADS:2 * GDN_HEADS].set(a_log.astype(F32))
    dtb = gate_lanes.at[0, GDN_HEADS:2 * GDN_HEADS].set(dt_bias.astype(F32))
    out = pl.pallas_call(
        _gdn_kernel,
        grid=(nc,),
        in_specs=[col(1), col(2), col(3), col(4),
                  pl.BlockSpec((B, C, BLK), lambda n: (0, n, 5 * RET_W // BLK)),
                  _const_spec((1, BLK)), _const_spec((1, BLK)), _const_spec((1, GDN_DV))],
        out_specs=pl.BlockSpec((B, C, RET_W), lambda n: (0, n, 0)),
        out_shape=jax.ShapeDtypeStruct((B, Lp, RET_W), BF16),
        scratch_shapes=[pltpu.VMEM((B * GDN_HEADS, GDN_DK, GDN_DV), F32)],
        compiler_params=_params(1, parallel=False),
        name="gated_deltanet",
    )(p32, p32, p32, p32, p32, alog, dtb, out_norm.astype(F32).reshape(1, GDN_DV))
    return out.reshape(B * Lp, RET_W)


def _swa_regroup(w, axis):
    G = SWA_HEADS // SWA_KV_HEADS
    width = SWA_HEADS * SWA_DH
    if axis == 0:
        heads = w[:width].reshape(SWA_KV_HEADS, G, SWA_DH, w.shape[1])
        return jnp.concatenate([jnp.swapaxes(heads, 0, 1).reshape(width, w.shape[1]), w[width:]], axis=0)
    heads = w[:, :width].reshape(w.shape[0], SWA_KV_HEADS, G, SWA_DH)
    return jnp.concatenate([jnp.swapaxes(heads, 1, 2).reshape(w.shape[0], width), w[:, width:]], axis=1)


def _swa_kernel(q_ref, kc_ref, vc_ref, kp_ref, vp_ref, km_ref, vm_ref, sink_ref, o_ref, *, S):
    n = pl.program_id(1)
    G = SWA_HEADS // SWA_KV_HEADS
    R = G * BLK
    r = lax.broadcasted_iota(jnp.int32, (R, BLK), 0) & (BLK - 1)
    col = lax.broadcasted_iota(jnp.int32, (R, BLK), 1)
    lower = col <= r
    upper = col > r
    is_meta = col >= PAD
    halves = [col < SWA_DH, col >= SWA_DH]
    k_meta, v_meta = km_ref[...], vm_ref[...]
    for s in range(S):
        blk = n * S + s
        rows = slice(s * BLK, (s + 1) * BLK)
        k_cur, v_cur = kc_ref[rows, :], vc_ref[rows, :]
        if s == 0:
            k_prev, v_prev = kp_ref[...], vp_ref[...]
        else:
            k_prev, v_prev = kc_ref[(s - 1) * BLK:s * BLK, :], vc_ref[(s - 1) * BLK:s * BLK, :]
        cur_ok = lower & (blk >= 1)
        prev_ok = upper & (blk >= 2)
        meta_ok = is_meta & ((blk >= 1) | lower)
        q_st = jnp.concatenate([q_ref[rows, b * BLK:(b + 1) * BLK] for b in range(G)], axis=0) * SWA_DH ** -0.5
        outs = []
        for kv in range(SWA_KV_HEADS):
            q = jnp.where(halves[kv], q_st, 0)
            s_cur = jnp.where(cur_ok, _dot_nt(q, k_cur), NEG_BIG)
            s_prev = jnp.where(prev_ok, _dot_nt(q, k_prev), NEG_BIG)
            s_meta = jnp.where(meta_ok, _dot_nt(q, k_meta), NEG_BIG)
            sink = jnp.concatenate([jnp.full((BLK, 1), sink_ref[kv * G + b], F32) for b in range(G)], axis=0)
            m = jnp.maximum(jnp.max(jnp.maximum(jnp.maximum(s_cur, s_prev), s_meta), axis=-1, keepdims=True), sink)
            p_cur = jnp.exp(s_cur - m)
            p_prev = jnp.exp(s_prev - m)
            p_meta = jnp.exp(s_meta - m)
            denom = jnp.sum(p_cur + p_prev + p_meta, axis=-1, keepdims=True) + jnp.exp(sink - m)
            o = (_dot(p_cur.astype(BF16), v_cur) + _dot(p_prev.astype(BF16), v_prev)
                 + _dot(p_meta.astype(BF16), v_meta))
            outs.append(o / denom)
        o = jnp.where(halves[0], outs[0], outs[1])
        for b in range(G):
            o_ref[rows, b * BLK:(b + 1) * BLK] = o[b * BLK:(b + 1) * BLK, :].astype(o_ref.dtype)


def _swa(proj, sinks, B, Lp):
    T = proj.shape[0]
    nb = Lp // BLK
    S = _blocks_per_step(nb, (13, 5))
    ns = nb // S
    k_col, v_col = RET_W // BLK, RET_W // BLK + 1
    cur = lambda j: pl.BlockSpec((S * BLK, BLK), lambda b, n: (b * ns + n, j))
    prev = lambda j: pl.BlockSpec((BLK, BLK), lambda b, n: (b * nb + jnp.maximum(n * S - 1, 0), j))
    first = lambda j: pl.BlockSpec((BLK, BLK), lambda b, n: (b * nb, j))
    return pl.pallas_call(
        functools.partial(_swa_kernel, S=S),
        grid=(B, ns),
        in_specs=[pl.BlockSpec((S * BLK, RET_W), lambda b, n: (b * ns + n, 0)),
                  cur(k_col), cur(v_col), prev(k_col), prev(v_col), first(k_col), first(v_col),
                  pl.BlockSpec(memory_space=pltpu.SMEM)],
        out_specs=pl.BlockSpec((S * BLK, RET_W), lambda b, n: (b * ns + n, 0)),
        out_shape=jax.ShapeDtypeStruct((T, RET_W), BF16),
        compiler_params=_params(2),
        name="swa_sink",
    )(proj, proj, proj, proj, proj, proj, proj, sinks.astype(F32))


def _sb_kernel(q_ref, k_ref, v_ref, o_ref, acc_ref, run_ref, *, S):
    n = pl.program_id(2)
    heads = BLK // SB_DH
    row = lax.broadcasted_iota(jnp.int32, (BLK, BLK), 0)
    col = lax.broadcasted_iota(jnp.int32, (BLK, BLK), 1)
    suffix = jnp.where(row > col, 1.0, 0.0).astype(BF16)
    ones = jnp.ones((BLK, BLK), BF16)
    sum_rhs = jnp.concatenate([jnp.concatenate([suffix, ones], axis=1)] * 2, axis=0)
    lane = lax.broadcasted_iota(jnp.int32, (S * BLK, BLK), 1)
    q_all = q_ref[...] * SB_DH ** -0.5
    q_head = [jnp.where((lane >= hh * SB_DH) & (lane < (hh + 1) * SB_DH), q_all, 0).reshape(S, BLK, BLK)
              for hh in range(heads)]

    def diagonal(hh, k, v, valid, run_in):
        z = jnp.einsum("sqd,skd->sqk", q_head[hh], k, preferred_element_type=F32)
        softplus_neg = jnp.log(1.0 + jnp.exp(-jnp.abs(z)))
        log_beta = jnp.minimum(z, 0.0) - softplus_neg
        log_1m = log_beta - z
        if valid is not None:
            log_1m = jnp.where(valid, log_1m, 0.0)
        hi, lo = _split(log_1m)
        sums = _dot(jnp.concatenate([hi, lo], axis=-1).reshape(S * BLK, 2 * BLK), sum_rhs)
        sums = sums.reshape(S, BLK, 2 * BLK)
        log_stick = sums[..., :BLK] if run_in is None else sums[..., :BLK] + run_in
        a = jnp.exp(log_beta + log_stick)
        if valid is not None:
            a = jnp.where(valid, a, 0.0)
        av = jnp.einsum("sqk,skd->sqd", a.astype(BF16), v, preferred_element_type=F32)
        run = sums[..., BLK:] if run_in is None else run_in + sums[..., BLK:]
        return av, run

    sub_block = lax.broadcasted_iota(jnp.int32, (S, BLK, BLK), 0)

    def keep_going(runs, next_d):
        alive = n * S + sub_block >= next_d
        top = jnp.max(jnp.where(alive, runs[0], NEG_BIG))
        for run in runs[1:]:
            top = jnp.maximum(top, jnp.max(jnp.where(alive, run, NEG_BIG)))
        return (top >= SB_SKIP).astype(jnp.int32)

    def slab(ref, first_blk):
        at = pl.multiple_of(first_blk * BLK, BLK)
        return ref[pl.ds(at, S * BLK), :].reshape(S, BLK, BLK)

    lead = min(SB_LEAD, S)

    def leading(at_sequence_start):
        k_diag, v_diag = slab(k_ref, n * S), slab(v_ref, n * S)
        k_before, v_before = [], []
        for back in range(lead - 1, 0, -1):
            at = pl.multiple_of(jnp.maximum(n * S - back, 0) * BLK, BLK)
            k_before.append(k_ref[pl.ds(at, BLK), :][None])
            v_before.append(v_ref[pl.ds(at, BLK), :][None])
        accs, runs = [None] * heads, [None] * heads
        for d in range(lead):
            k = jnp.concatenate(k_before[lead - 1 - d:] + [k_diag[:S - d]], axis=0)
            v = jnp.concatenate(v_before[lead - 1 - d:] + [v_diag[:S - d]], axis=0)
            if at_sequence_start:
                ok = jnp.stack([(n * S + s - d) * BLK + col >= PAD for s in range(S)])
                ok = ok & (col < row) if d == 0 else ok
            else:
                ok = jnp.broadcast_to(col < row, (S, BLK, BLK)) if d == 0 else None
            for hh in range(heads):
                av, runs[hh] = diagonal(hh, k, v, ok, runs[hh])
                accs[hh] = av if d == 0 else accs[hh] + av
        for hh in range(heads):
            acc_ref[hh] = accs[hh]
            run_ref[hh] = runs[hh]
        return keep_going(runs, lead)

    go = lax.cond(n * S <= lead - 1, lambda: leading(True), lambda: leading(False))

    def cond(carry):
        d, go = carry
        return (d <= n * S + S - 1) & (go > 0)

    def body(carry):
        d, _ = carry
        first_blk = n * S - d

        def further(k, v, valid):
            runs = []
            for hh in range(heads):
                av, run = diagonal(hh, k, v, valid, run_ref[hh])
                acc_ref[hh] += av
                run_ref[hh] = run
                runs.append(run)
            return keep_going(runs, d + 1)

        def interior():
            return further(slab(k_ref, first_blk), slab(v_ref, first_blk), None)

        def edge():
            ks, vs, valids = [], [], []
            for s in range(S):
                jb = first_blk + s
                at = pl.multiple_of(jnp.maximum(jb, 0) * BLK, BLK)
                ks.append(k_ref[pl.ds(at, BLK), :])
                vs.append(v_ref[pl.ds(at, BLK), :])
                valids.append(jb * BLK + col >= PAD)
            return further(jnp.stack(ks), jnp.stack(vs), jnp.stack(valids))

        return d + 1, lax.cond(first_blk >= 1, interior, edge)

    lax.while_loop(cond, body, (jnp.int32(lead), go))
    o = acc_ref[0]
    for hh in range(1, heads):
        o = jnp.where(lane.reshape(S, BLK, BLK) >= hh * SB_DH, acc_ref[hh], o)
    o_ref[...] = o.reshape(S * BLK, BLK).astype(o_ref.dtype)


def _stick_breaking(proj, B, Lp):
    T = proj.shape[0]
    nb = Lp // BLK
    S = _blocks_per_step(nb, (13, 5))
    ns = nb // S
    pairs = SB_HEADS * SB_DH // BLK
    heads = BLK // SB_DH
    q0 = (SWA_HEADS + 2 * SWA_KV_HEADS) * SWA_DH // BLK
    k0, v0 = q0 + pairs, q0 + 2 * pairs
    proj3 = proj.reshape(B, Lp, proj.shape[1])
    seq = lambda c0: pl.BlockSpec((None, Lp, BLK), lambda b, p, n: (b, 0, c0 + p))
    return pl.pallas_call(
        functools.partial(_sb_kernel, S=S),
        grid=(B, pairs, ns),
        in_specs=[pl.BlockSpec((S * BLK, BLK), lambda b, p, n: (b * ns + n, q0 + p)), seq(k0), seq(v0)],
        out_specs=pl.BlockSpec((S * BLK, BLK), lambda b, p, n: (b * ns + n, p)),
        out_shape=jax.ShapeDtypeStruct((T, RET_W), BF16),
        scratch_shapes=[pltpu.VMEM((heads, S, BLK, BLK), F32), pltpu.VMEM((heads, S, BLK, BLK), F32)],
        compiler_params=_params(3),
        name="stick_breaking",
    )(proj, proj3, proj3)


def _rotation_tables(Lp):
    half = RET_DK // 2
    inv_freq = 1.0 / (10000.0 ** jnp.linspace(0.0, 1.0, half, dtype=F32))
    pos = jnp.arange(Lp, dtype=F32) - float(PAD)
    ang = pos[:, None] * inv_freq[None, :]
    cos, sin = jnp.cos(ang), jnp.sin(ang)
    return jnp.concatenate([cos, cos], axis=1), jnp.concatenate([-sin, sin], axis=1)


def _ab_weight(w_in):
    D = w_in.shape[0]
    qk = w_in[:, :2 * RET_W].reshape(D, 2 * RET_HEADS, RET_DK // 2, 2)
    qk = jnp.swapaxes(qk, 2, 3).reshape(D, 2 * RET_W)
    return jnp.concatenate([qk, w_in[:, 2 * RET_W:], jnp.zeros((D, AB_COLS - w_in.shape[1]), w_in.dtype)], axis=1)


def kernel(x, meta_tokens, norm_gains, ffn_w_gate, ffn_w_up, ffn_w_down, ab_w_in, ab_conv_w, ab_a_log, ab_dt_bias, ab_out_norm, ab_w_out, cd_w_in, cd_sinks, cd_w_out):
    B, S, D = x.shape
    Lp = S + BLK
    depth = norm_gains.shape[0]
    meta = jnp.broadcast_to(meta_tokens[None].astype(x.dtype), (B, N_META, D))
    tm = _row_tile(B * Lp, 512)
    tm_tok = _row_tile(S, 512)
    token_rows = _token_rows(tm_tok, S, Lp)
    tm_seq = _row_tile(Lp, 640)
    lead_tiles = jnp.concatenate([jnp.zeros((B, PAD, D), x.dtype), meta, x[:, :tm_seq - BLK]], axis=1)
    cos, sin = _rotation_tables(Lp)
    for i in range(depth):
        g = norm_gains[i]
        j = i // 2
        ffn1 = (g[0], g[1], ffn_w_gate[i, 0], ffn_w_up[i, 0], ffn_w_down[i, 0])
        if i == 0:
            h = _ffn(x.reshape(B * S, D), *ffn1, tm=tm_seq, rows=_shifted_token_rows(tm_seq, S, Lp),
                     n_tiles=B * Lp // tm_seq, lead=lead_tiles.reshape(B * tm_seq, D), lead_every=Lp // tm_seq)
        else:
            h = _ffn(h, *ffn1, tm=tm)
        if i % 2 == 0:
            p16, p32 = _proj_ab(h, g[2], ab_w_in[j], ab_conv_w[j], cos, sin, B, Lp)
            mix_a = _retention(p16, p32, B, Lp)
            mix_b = _gdn(p32, ab_a_log[j], ab_dt_bias[j], ab_out_norm[j], B, Lp)
            w_out = ab_w_out[j]
        else:
            w_in = _swa_regroup(cd_w_in[j].astype(BF16), axis=1)
            w_out = _swa_regroup(cd_w_out[j].astype(BF16), axis=0)
            proj = _proj(h, g[2], w_in, BF16, tm=640)
            mix_a = _swa(proj, cd_sinks[j], B, Lp)
            mix_b = _stick_breaking(proj, B, Lp)
        ffn2 = (g[4], g[5], ffn_w_gate[i, 1], ffn_w_up[i, 1], ffn_w_down[i, 1])
        mix = (mix_a, mix_b, g[3], w_out)
        if i == depth - 1:
            h = _ffn(h, *ffn2, tm=tm_tok, rows=token_rows, n_tiles=B * S // tm_tok, mix=mix)
        else:
            h = _ffn(h, *ffn2, tm=tm, mix=mix)
    return h.reshape(B, S, D)
```

```python
import functools
import math

import jax
import jax.numpy as jnp
from jax import lax
from jax.experimental import pallas as pl
from jax.experimental.pallas import tpu as pltpu

F32 = jnp.float32
BF16 = jnp.bfloat16

N_META = 16
NORM_EPS = 1e-6
BLK = 128
PAD = BLK - N_META

RET_HEADS, RET_DK, RET_DV = 4, 128, 128
GDN_HEADS, GDN_DK, GDN_DV, GDN_CONV = 4, 128, 128, 4
SWA_HEADS, SWA_KV_HEADS, SWA_DH = 8, 2, 64
SB_HEADS, SB_DH = 8, 64
SOLVE_BLK = 16

RET_W = RET_HEADS * RET_DK
AB_GATE_COL = 4096
AB_COLS = AB_GATE_COL + BLK
CONV_HALO = 8

VMEM_LIMIT = 56 * 1024 * 1024
NEG_BIG = -1e30
SB_SKIP = -87.5
SB_LEAD = 3


def _rms(x, g):
    return x * lax.rsqrt(jnp.mean(x * x, axis=-1, keepdims=True) + NORM_EPS) * g


def _silu(x):
    return x * jax.nn.sigmoid(x)


def _dot(a, b):
    return jnp.dot(a, b, preferred_element_type=F32)


def _dot_nt(a, b):
    return lax.dot_general(a, b, (((1,), (1,)), ((), ())), preferred_element_type=F32)


def _dot_tn(a, b):
    return lax.dot_general(a, b, (((0,), (0,)), ((), ())), preferred_element_type=F32)


def _split(a):
    hi = a.astype(BF16)
    return hi, (a - hi.astype(F32)).astype(BF16)


def _dot3(a, b, dot=_dot):
    ah, al = _split(a)
    bh, bl = _split(b)
    return dot(ah, bh) + (dot(ah, bl) + dot(al, bh))


def _params(n_grid, parallel=True):
    sem = ("parallel",) + ("arbitrary",) * (n_grid - 1) if parallel else ("arbitrary",) * n_grid
    return pltpu.CompilerParams(dimension_semantics=sem, vmem_limit_bytes=VMEM_LIMIT)


def _const_spec(shape):
    nd = len(shape)
    return pl.BlockSpec(shape, lambda *_: (0,) * nd, pipeline_mode=pl.Buffered(1))


def _row_tile(rows, target):
    tile = target
    while rows % tile:
        tile -= BLK
    return tile


def _blocks_per_step(nb, prefs):
    for s in prefs:
        if nb % s == 0:
            return s
    return 1


def _ffn_kernel(h_ref, gpre_ref, gpost_ref, wg_ref, wu_ref, wd_ref, *rest, tf, lead_every, mixed):
    o_ref, xn_ref, act_ref = rest[-3:]
    h = h_ref[...]
    if lead_every:
        h = jnp.where(pl.program_id(0) % lead_every == 0, rest[0][...], h)
    if mixed:
        a_ref, b_ref, gmix_ref, wa_ref, wb_ref = rest[:5]
        h = h + _rms(_dot(a_ref[...], wa_ref[...]) + _dot(b_ref[...], wb_ref[...]), gmix_ref[...])
    xn_ref[...] = _rms(h, gpre_ref[...]).astype(BF16)
    for c in range(0, wg_ref.shape[1], tf):
        xn = xn_ref[...]
        g = _dot(xn, wg_ref[:, c:c + tf])
        u = _dot(xn, wu_ref[:, c:c + tf])
        act_ref[:, c:c + tf] = (_silu(g) * u).astype(BF16)
    y = _dot(act_ref[...], wd_ref[...])
    o_ref[...] = h + 0.5 * _rms(y, gpost_ref[...])


def _ffn(src, g_pre, g_post, w_gate, w_up, w_down, *, tm, rows=None, n_tiles=None, lead=None, lead_every=0,
         mix=None, tf=256):
    D = src.shape[1]
    F = w_gate.shape[1]
    n_tiles = n_tiles or src.shape[0] // tm
    rows = rows or (lambda width: pl.BlockSpec((tm, width), lambda i: (i, 0)))
    operands = [src, g_pre.reshape(1, D), g_post.reshape(1, D),
                w_gate.astype(BF16), w_up.astype(BF16), w_down.astype(BF16)]
    in_specs = [rows(D), _const_spec((1, D)), _const_spec((1, D)),
                _const_spec((D, F)), _const_spec((D, F)), _const_spec((F, D))]
    if lead is not None:
        operands.append(lead)
        in_specs.append(pl.BlockSpec((tm, D), lambda i: (i // lead_every, 0)))
    if mix is not None:
        a, b, g_mix, w_out = mix
        Ka, Kb = a.shape[1], b.shape[1]
        w_out = w_out.astype(BF16)
        operands += [a, b, g_mix.reshape(1, D), w_out[:Ka], w_out[Ka:]]
        in_specs += [rows(Ka), rows(Kb), _const_spec((1, D)), _const_spec((Ka, D)), _const_spec((Kb, D))]
    return pl.pallas_call(
        functools.partial(_ffn_kernel, tf=tf, lead_every=lead_every, mixed=mix is not None),
        grid=(n_tiles,),
        in_specs=in_specs,
        out_specs=pl.BlockSpec((tm, D), lambda i: (i, 0)),
        out_shape=jax.ShapeDtypeStruct((n_tiles * tm, D), F32),
        scratch_shapes=[pltpu.VMEM((tm, D), BF16), pltpu.VMEM((tm, F), BF16)],
        compiler_params=_params(1),
        name="ffn",
    )(*operands)


def _shifted_token_rows(tm, S, Lp):
    per_batch = Lp // tm
    return lambda width: pl.BlockSpec(
        (pl.Element(tm), pl.Element(width)),
        lambda i: (BLK * ((i // per_batch) * (S // BLK) + jnp.maximum((i % per_batch) * (tm // BLK) - 1, 0)), 0))


def _token_rows(tm, S, Lp):
    per_batch = S // tm
    return lambda width: pl.BlockSpec(
        (pl.Element(tm), pl.Element(width)),
        lambda i: (BLK * ((i // per_batch) * (Lp // BLK) + 1 + (i % per_batch) * (tm // BLK)), 0))


def _proj_kernel(h_ref, g_ref, w_ref, o_ref, *, tn):
    xn = _rms(h_ref[...], g_ref[...]).astype(BF16)
    for c in range(0, w_ref.shape[1], tn):
        o_ref[:, c:c + tn] = _dot(xn, w_ref[:, c:c + tn]).astype(o_ref.dtype)


def _proj(h, g, w, out_dtype, *, tm=512, tn=256):
    T, D = h.shape
    N = w.shape[1]
    tm = _row_tile(T, tm)
    return pl.pallas_call(
        functools.partial(_proj_kernel, tn=tn),
        grid=(T // tm,),
        in_specs=[pl.BlockSpec((tm, D), lambda i: (i, 0)), _const_spec((1, D)), _const_spec((D, N))],
        out_specs=pl.BlockSpec((tm, N), lambda i: (i, 0)),
        out_shape=jax.ShapeDtypeStruct((T, N), out_dtype),
        compiler_params=_params(1),
        name="in_proj",
    )(h, g.reshape(1, D), w.astype(BF16))


def _proj_ab_kernel(h_ref, g_ref, w_ref, cos_ref, sin_ref, cw_ref, o16_ref, o32_ref, xn_ref, halo_ref, *, tn):
    rows = h_ref.shape[0]
    W = RET_W

    @pl.when(pl.program_id(1) == 0)
    def _start_of_sequence():
        halo_ref[...] = jnp.zeros_like(halo_ref)

    xn_ref[...] = _rms(h_ref[...], g_ref[...]).astype(BF16)
    cos, sin = cos_ref[...], sin_ref[...]
    rot = lambda t: t * cos + pltpu.roll(t, RET_DK // 2, 1) * sin
    l2n = lambda t: t * lax.rsqrt(jnp.sum(t * t, axis=-1, keepdims=True) + NORM_EPS)
    heavy = list(range(4 * W, 7 * W, tn))
    light = [c for c in range(0, AB_GATE_COL, tn) if c not in heavy]
    order = light[:2] + [c for pair in zip(heavy, light[2:]) for c in pair] + light[2 + len(heavy):]
    for c in order:
        pre = _dot(xn_ref[...], w_ref[:, c:c + tn])
        group, off = divmod(c, W)
        if group <= 1:
            scale = 1.0 if group == 0 else RET_DK ** -0.5
            for hs in range(0, tn, RET_DK):
                o16_ref[:, c + hs:c + hs + RET_DK] = (rot(pre[:, hs:hs + RET_DK]) * scale).astype(BF16)
        elif group == 2:
            o16_ref[:, c:c + tn] = pre.astype(BF16)
        elif group == 3:
            o32_ref[:, off:off + tn] = _silu(pre)
        elif group <= 6:
            ch = c - 4 * W
            x_ext = jnp.concatenate([halo_ref[:, ch:ch + tn], pre], axis=0)
            y = cw_ref[GDN_CONV - 1:GDN_CONV, ch:ch + tn] * pre
            for shift in range(1, GDN_CONV):
                tap = GDN_CONV - 1 - shift
                y = y + cw_ref[tap:tap + 1, ch:ch + tn] * pltpu.roll(x_ext, shift, 0)[CONV_HALO:]
            halo_ref[:, ch:ch + tn] = pre[rows - CONV_HALO:rows]
            act = _silu(y)
            if group == 6:
                o32_ref[:, W + ch:W + ch + tn] = act
            else:
                scale = GDN_DK ** -0.5 if group == 4 else 1.0
                for hs in range(0, tn, GDN_DK):
                    o32_ref[:, W + ch + hs:W + ch + hs + GDN_DK] = l2n(act[:, hs:hs + GDN_DK]) * scale
        else:
            o32_ref[:, 4 * W + off:4 * W + off + tn] = _silu(pre)
    o32_ref[:, 5 * W:5 * W + BLK] = _dot(xn_ref[...], w_ref[:, AB_GATE_COL:AB_GATE_COL + BLK])


def _proj_ab(h, g, w_in, conv_w, cos, sin, B, Lp, *, tm=640, tn=256):
    D = h.shape[1]
    tm = _row_tile(Lp, tm)
    W = RET_W
    seq = lambda n: pl.BlockSpec((None, tm, n), lambda b, j: (b, j, 0))
    tab = pl.BlockSpec((tm, RET_DK), lambda b, j: (j, 0))
    return pl.pallas_call(
        functools.partial(_proj_ab_kernel, tn=tn),
        grid=(B, Lp // tm),
        in_specs=[seq(D), _const_spec((1, D)), _const_spec((D, AB_COLS)), tab, tab,
                  _const_spec((GDN_CONV, 3 * W))],
        out_specs=[seq(3 * W), seq(5 * W + BLK)],
        out_shape=[jax.ShapeDtypeStruct((B, Lp, 3 * W), BF16), jax.ShapeDtypeStruct((B, Lp, 5 * W + BLK), F32)],
        scratch_shapes=[pltpu.VMEM((tm, D), BF16), pltpu.VMEM((CONV_HALO, 3 * W), F32)],
        compiler_params=_params(2, parallel=False),
        name="in_proj_ab",
    )(h.reshape(B, Lp, D), g.reshape(1, D), _ab_weight(w_in.astype(BF16)), cos, sin, conv_w.astype(F32))


RET_LOG_GAMMA = [math.log1p(-2.0 ** (-5.0 - hd)) for hd in range(RET_HEADS)]


def _ret_init(s_ref, dec_ref, zx_ref):
    C = BLK
    s_ref[...] = jnp.zeros_like(s_ref)
    diff = (lax.broadcasted_iota(jnp.int32, (C, C), 0) - lax.broadcasted_iota(jnp.int32, (C, C), 1)).astype(F32)
    idx = lax.broadcasted_iota(jnp.int32, (C, RET_DV), 0).astype(F32)
    for hd, lg in enumerate(RET_LOG_GAMMA):
        dec_ref[hd] = jnp.where(diff >= 0, jnp.exp(jnp.maximum(diff, 0.0) * lg), 0.0)
        zx_ref[0, hd] = jnp.exp((C - 1.0 - idx) * lg)
        zx_ref[1, hd] = jnp.exp((idx + 1.0) * lg)


def _ret_step(q_ref, k_ref, v_ref, gate_ref, o_ref, s_ref, dec_ref, zx_ref):
    nbatch, C = q_ref.shape[0], q_ref.shape[1]
    log_gamma = RET_LOG_GAMMA
    chains = [(b, hd) for b in range(nbatch) for hd in range(RET_HEADS)]
    sl = lambda hd: slice(hd * RET_DK, (hd + 1) * RET_DK)
    q = [q_ref[b, :, sl(hd)] for b, hd in chains]
    k = [k_ref[b, :, sl(hd)] for b, hd in chains]
    v = [v_ref[b, :, sl(hd)] for b, hd in chains]
    scores = [_dot_nt(qq, kk) * dec_ref[hd] for qq, kk, (_, hd) in zip(q, k, chains)]
    intra = [_dot(sc.astype(BF16), vv) for sc, vv in zip(scores, v)]
    state = [s_ref[i] for i in range(len(chains))]
    cross = [_dot(qq, st.astype(BF16)) * zx_ref[1, hd] for qq, st, (_, hd) in zip(q, state, chains)]
    kv = [_dot_tn(kk, (vv.astype(F32) * zx_ref[0, hd]).astype(BF16)) for kk, vv, (_, hd) in zip(k, v, chains)]
    for i, (b, hd) in enumerate(chains):
        s_ref[i] = state[i] * math.exp(C * log_gamma[hd]) + kv[i]
        o = intra[i] + cross[i]
        mu = jnp.mean(o, axis=-1, keepdims=True)
        var = jnp.mean(jnp.square(o - mu), axis=-1, keepdims=True)
        o_ref[b, :, sl(hd)] = ((o - mu) * lax.rsqrt(var + NORM_EPS) * gate_ref[b, :, sl(hd)]).astype(o_ref.dtype)


def _unit_lower_inverses(a_mats, eye, same_blk):
    d = [jnp.where(same_blk, a, 0.0) for a in a_mats]
    e = [(a - dd).astype(BF16) for a, dd in zip(a_mats, d)]
    x = [-dd for dd in d]
    t = _product_of_powers([eye + xx for xx in x], x, 3)
    tb = [tt.astype(BF16) for tt in t]
    f = [-_dot(tt, ee) for tt, ee in zip(tb, e)]
    p = _product_of_powers([eye + ff for ff in f], f, 2)
    return [pp.astype(BF16) for pp in p], tb


def _product_of_powers(prod, x, steps):
    size = x[0].shape[1]
    x = [_dot(xb, xb) for xb in [xx.astype(BF16) for xx in x]]
    for step in range(steps):
        xb = [xx.astype(BF16) for xx in x]
        if step == steps - 1:
            return [pp + _dot(xx, pp.astype(BF16)) for pp, xx in zip(prod, xb)]
        both = [_dot(xx, jnp.concatenate([xx, pp.astype(BF16)], axis=1)) for pp, xx in zip(prod, xb)]
        x = [bb[:, :size] for bb in both]
        prod = [pp + bb[:, size:] for pp, bb in zip(prod, both)]


def _gdn_step(q_ref, k_ref, v_ref, z_ref, gate_ref, alog_ref, dtb_ref, onorm_ref, o_ref, s_ref):
    nbatch, C = q_ref.shape[0], q_ref.shape[1]
    row = lax.broadcasted_iota(jnp.int32, (C, C), 0)
    col = lax.broadcasted_iota(jnp.int32, (C, C), 1)
    incl = row >= col
    strict = row > col
    same_blk = (row // SOLVE_BLK) == (col // SOLVE_BLK)
    eye = jnp.where(row == col, 1.0, 0.0).astype(F32)
    tri_incl = jnp.where(incl, 1.0, 0.0).astype(F32)
    tri_upper = jnp.where(row <= col, 1.0, 0.0).astype(F32)

    gates = [gate_ref[b] for b in range(nbatch)]
    beta_all = [jax.nn.sigmoid(gt) for gt in gates]
    g_all = [-jnp.exp(alog_ref[...]) * jax.nn.softplus(gt + dtb_ref[...]) for gt in gates]
    gcum_all = [_dot3(tri_incl, g) for g in g_all]
    gcum_t = [_dot3(g, tri_upper, _dot_tn) for g in g_all]

    chains = [(b, hd) for b in range(nbatch) for hd in range(GDN_HEADS)]
    head = lambda ref, b, hd: ref[b, :, hd * GDN_DK:(hd + 1) * GDN_DK]
    q = [head(q_ref, b, hd) for b, hd in chains]
    k = [head(k_ref, b, hd) for b, hd in chains]
    v = [head(v_ref, b, hd) for b, hd in chains]
    beta = [beta_all[b][:, hd:hd + 1] for b, hd in chains]
    gcum = [gcum_all[b][:, GDN_HEADS + hd:GDN_HEADS + hd + 1] for b, hd in chains]
    gcum_row = [gcum_t[b][GDN_HEADS + hd:GDN_HEADS + hd + 1, :] for b, hd in chains]
    decay = [jnp.where(incl, jnp.exp(jnp.where(incl, gc - gr, 0.0)), 0.0) for gc, gr in zip(gcum, gcum_row)]
    k_beta = [kk * bb for kk, bb in zip(k, beta)]
    scores = [_dot_nt(jnp.concatenate([kbt.astype(BF16), qq.astype(BF16)], axis=0), kk.astype(BF16))
              for kbt, qq, kk in zip(k_beta, q, k)]
    a_mat = [jnp.where(strict, sc[:C] * dc, 0.0) for sc, dc in zip(scores, decay)]
    p_mat, t_inv = _unit_lower_inverses(a_mat, eye, same_blk)
    e_gcum = [jnp.exp(gc) for gc in gcum]
    rhs = [jnp.concatenate([vv * bb, kbt * eg], axis=-1) for vv, bb, kbt, eg in zip(v, beta, k_beta, e_gcum)]
    sol = [_dot(tt, rr.astype(BF16)) for tt, rr in zip(t_inv, rhs)]
    sol = [_dot(pp, ss.astype(BF16)) for pp, ss in zip(p_mat, sol)]
    qk = [jnp.where(incl, sc[C:] * dc, 0.0) for sc, dc in zip(scores, decay)]
    g_last = [gc[C - 1:C, :] for gc in gcum]
    k_tail = [(kk * jnp.exp(gl - gc)).astype(BF16) for kk, gl, gc in zip(k, g_last, gcum)]
    q_dec = [(qq * eg).astype(BF16) for qq, eg in zip(q, e_gcum)]

    state = [s_ref[i] for i in range(len(chains))]
    state_b = [st.astype(BF16) for st in state]
    v_new = [ss[:, :GDN_DV] - _dot(ss[:, GDN_DV:].astype(BF16), sb) for ss, sb in zip(sol, state_b)]
    v_new_b = [vn.astype(BF16) for vn in v_new]
    out = [_dot(qd, sb) + _dot(qkm.astype(BF16), vn) for qd, sb, qkm, vn in zip(q_dec, state_b, qk, v_new_b)]
    for i, (b, hd) in enumerate(chains):
        s_ref[i] = state[i] * jnp.exp(g_last[i]) + _dot_tn(k_tail[i], v_new_b[i])
        sl = slice(hd * GDN_DV, (hd + 1) * GDN_DV)
        o_ref[b, :, sl] = (_rms(out[i], onorm_ref[...]) * z_ref[b, :, sl]).astype(o_ref.dtype)


def _ab_mixers_kernel(rq_ref, rk_ref, rv_ref, rgate_ref, gq_ref, gk_ref, gv_ref, gz_ref, gate_ref,
                      alog_ref, dtb_ref, onorm_ref, ret_o_ref, gdn_o_ref, ret_s_ref, dec_ref, zx_ref, gdn_s_ref):
    @pl.when(pl.program_id(0) == 0)
    def _init():
        _ret_init(ret_s_ref, dec_ref, zx_ref)
        gdn_s_ref[...] = jnp.zeros_like(gdn_s_ref)

    _ret_step(rq_ref, rk_ref, rv_ref, rgate_ref, ret_o_ref, ret_s_ref, dec_ref, zx_ref)
    _gdn_step(gq_ref, gk_ref, gv_ref, gz_ref, gate_ref, alog_ref, dtb_ref, onorm_ref, gdn_o_ref, gdn_s_ref)


def _ab_mixers(p16, p32, a_log, dt_bias, out_norm, B, Lp):
    C = BLK
    col = lambda j: pl.BlockSpec((B, C, RET_W), lambda n: (0, n, j))
    gate_lanes = jnp.zeros((1, BLK), F32)
    alog = gate_lanes.at[0, GDN_HEADS:2 * GDN_HEADS].set(a_log.astype(F32))
    dtb = gate_lanes.at[0, GDN_HEADS:2 * GDN_HEADS].set(dt_bias.astype(F32))
    out_shape = jax.ShapeDtypeStruct((B, Lp, RET_W), BF16)
    ret, gdn = pl.pallas_call(
        _ab_mixers_kernel,
        grid=(Lp // C,),
        in_specs=[col(0), col(1), col(2), col(0), col(1), col(2), col(3), col(4),
                  pl.BlockSpec((B, C, BLK), lambda n: (0, n, 5 * RET_W // BLK)),
                  _const_spec((1, BLK)), _const_spec((1, BLK)), _const_spec((1, GDN_DV))],
        out_specs=[col(0), col(0)],
        out_shape=[out_shape, out_shape],
        scratch_shapes=[pltpu.VMEM((B * RET_HEADS, RET_DK, RET_DV), F32),
                        pltpu.VMEM((RET_HEADS, BLK, BLK), F32),
                        pltpu.VMEM((2, RET_HEADS, BLK, RET_DV), F32),
                        pltpu.VMEM((B * GDN_HEADS, GDN_DK, GDN_DV), F32)],
        compiler_params=_params(1, parallel=False),
        name="retention_deltanet",
    )(p16, p16, p16, p32, p32, p32, p32, p32, p32, alog, dtb, out_norm.astype(F32).reshape(1, GDN_DV))
    return ret.reshape(B * Lp, RET_W), gdn.reshape(B * Lp, RET_W)


def _swa_regroup(w, axis):
    G = SWA_HEADS // SWA_KV_HEADS
    width = SWA_HEADS * SWA_DH
    if axis == 0:
        heads = w[:width].reshape(SWA_KV_HEADS, G, SWA_DH, w.shape[1])
        return jnp.concatenate([jnp.swapaxes(heads, 0, 1).reshape(width, w.shape[1]), w[width:]], axis=0)
    heads = w[:, :width].reshape(w.shape[0], SWA_KV_HEADS, G, SWA_DH)
    return jnp.concatenate([jnp.swapaxes(heads, 1, 2).reshape(w.shape[0], width), w[:, width:]], axis=1)


def _swa_kernel(q_ref, kc_ref, vc_ref, kp_ref, vp_ref, km_ref, vm_ref, sink_ref, o_ref, *, S):
    n = pl.program_id(1)
    G = SWA_HEADS // SWA_KV_HEADS
    R = G * BLK
    r = lax.broadcasted_iota(jnp.int32, (R, BLK), 0) & (BLK - 1)
    col = lax.broadcasted_iota(jnp.int32, (R, BLK), 1)
    lower = col <= r
    upper = col > r
    is_meta = col >= PAD
    halves = [col < SWA_DH, col >= SWA_DH]
    k_meta, v_meta = km_ref[...], vm_ref[...]
    for s in range(S):
        blk = n * S + s
        rows = slice(s * BLK, (s + 1) * BLK)
        k_cur, v_cur = kc_ref[rows, :], vc_ref[rows, :]
        if s == 0:
            k_prev, v_prev = kp_ref[...], vp_ref[...]
        else:
            k_prev, v_prev = kc_ref[(s - 1) * BLK:s * BLK, :], vc_ref[(s - 1) * BLK:s * BLK, :]
        cur_ok = lower & (blk >= 1)
        prev_ok = upper & (blk >= 2)
        meta_ok = is_meta & ((blk >= 1) | lower)
        q_st = jnp.concatenate([q_ref[rows, b * BLK:(b + 1) * BLK] for b in range(G)], axis=0) * SWA_DH ** -0.5
        outs = []
        for kv in range(SWA_KV_HEADS):
            q = jnp.where(halves[kv], q_st, 0)
            s_cur = jnp.where(cur_ok, _dot_nt(q, k_cur), NEG_BIG)
            s_prev = jnp.where(prev_ok, _dot_nt(q, k_prev), NEG_BIG)
            s_meta = jnp.where(meta_ok, _dot_nt(q, k_meta), NEG_BIG)
            sink = jnp.concatenate([jnp.full((BLK, 1), sink_ref[kv * G + b], F32) for b in range(G)], axis=0)
            m = jnp.maximum(jnp.max(jnp.maximum(jnp.maximum(s_cur, s_prev), s_meta), axis=-1, keepdims=True), sink)
            p_cur = jnp.exp(s_cur - m)
            p_prev = jnp.exp(s_prev - m)
            p_meta = jnp.exp(s_meta - m)
            denom = jnp.sum(p_cur + p_prev + p_meta, axis=-1, keepdims=True) + jnp.exp(sink - m)
            o = (_dot(p_cur.astype(BF16), v_cur) + _dot(p_prev.astype(BF16), v_prev)
                 + _dot(p_meta.astype(BF16), v_meta))
            outs.append(o / denom)
        o = jnp.where(halves[0], outs[0], outs[1])
        for b in range(G):
            o_ref[rows, b * BLK:(b + 1) * BLK] = o[b * BLK:(b + 1) * BLK, :].astype(o_ref.dtype)


def _swa(proj, sinks, B, Lp):
    T = proj.shape[0]
    nb = Lp // BLK
    S = _blocks_per_step(nb, (13, 5))
    ns = nb // S
    k_col, v_col = RET_W // BLK, RET_W // BLK + 1
    cur = lambda j: pl.BlockSpec((S * BLK, BLK), lambda b, n: (b * ns + n, j))
    prev = lambda j: pl.BlockSpec((BLK, BLK), lambda b, n: (b * nb + jnp.maximum(n * S - 1, 0), j))
    first = lambda j: pl.BlockSpec((BLK, BLK), lambda b, n: (b * nb, j))
    return pl.pallas_call(
        functools.partial(_swa_kernel, S=S),
        grid=(B, ns),
        in_specs=[pl.BlockSpec((S * BLK, RET_W), lambda b, n: (b * ns + n, 0)),
                  cur(k_col), cur(v_col), prev(k_col), prev(v_col), first(k_col), first(v_col),
                  pl.BlockSpec(memory_space=pltpu.SMEM)],
        out_specs=pl.BlockSpec((S * BLK, RET_W), lambda b, n: (b * ns + n, 0)),
        out_shape=jax.ShapeDtypeStruct((T, RET_W), BF16),
        compiler_params=_params(2),
        name="swa_sink",
    )(proj, proj, proj, proj, proj, proj, proj, sinks.astype(F32))


def _sb_kernel(q_ref, k_ref, v_ref, o_ref, acc_ref, run_ref, *, S):
    n = pl.program_id(2)
    heads = BLK // SB_DH
    row = lax.broadcasted_iota(jnp.int32, (BLK, BLK), 0)
    col = lax.broadcasted_iota(jnp.int32, (BLK, BLK), 1)
    suffix = jnp.where(row > col, 1.0, 0.0).astype(BF16)
    ones = jnp.ones((BLK, BLK), BF16)
    sum_rhs = jnp.concatenate([jnp.concatenate([suffix, ones], axis=1)] * 2, axis=0)
    lane = lax.broadcasted_iota(jnp.int32, (S * BLK, BLK), 1)
    q_all = q_ref[...] * SB_DH ** -0.5
    q_head = [jnp.where((lane >= hh * SB_DH) & (lane < (hh + 1) * SB_DH), q_all, 0).reshape(S, BLK, BLK)
              for hh in range(heads)]

    def diagonal(hh, k, v, valid, run_in):
        z = jnp.einsum("sqd,skd->sqk", q_head[hh], k, preferred_element_type=F32)
        softplus_neg = jnp.log(1.0 + jnp.exp(-jnp.abs(z)))
        log_beta = jnp.minimum(z, 0.0) - softplus_neg
        log_1m = log_beta - z
        if valid is not None:
            log_1m = jnp.where(valid, log_1m, 0.0)
        hi, lo = _split(log_1m)
        sums = _dot(jnp.concatenate([hi, lo], axis=-1).reshape(S * BLK, 2 * BLK), sum_rhs)
        sums = sums.reshape(S, BLK, 2 * BLK)
        log_stick = sums[..., :BLK] if run_in is None else sums[..., :BLK] + run_in
        a = jnp.exp(log_beta + log_stick)
        if valid is not None:
            a = jnp.where(valid, a, 0.0)
        av = jnp.einsum("sqk,skd->sqd", a.astype(BF16), v, preferred_element_type=F32)
        run = sums[..., BLK:] if run_in is None else run_in + sums[..., BLK:]
        return av, run

    sub_block = lax.broadcasted_iota(jnp.int32, (S, BLK, BLK), 0)

    def keep_going(runs, next_d):
        alive = n * S + sub_block >= next_d
        top = jnp.max(jnp.where(alive, runs[0], NEG_BIG))
        for run in runs[1:]:
            top = jnp.maximum(top, jnp.max(jnp.where(alive, run, NEG_BIG)))
        return (top >= SB_SKIP).astype(jnp.int32)

    def slab(ref, first_blk):
        at = pl.multiple_of(first_blk * BLK, BLK)
        return ref[pl.ds(at, S * BLK), :].reshape(S, BLK, BLK)

    lead = min(SB_LEAD, S)

    def leading(at_sequence_start):
        k_diag, v_diag = slab(k_ref, n * S), slab(v_ref, n * S)
        k_before, v_before = [], []
        for back in range(lead - 1, 0, -1):
            at = pl.multiple_of(jnp.maximum(n * S - back, 0) * BLK, BLK)
            k_before.append(k_ref[pl.ds(at, BLK), :][None])
            v_before.append(v_ref[pl.ds(at, BLK), :][None])
        accs, runs = [None] * heads, [None] * heads
        for d in range(lead):
            k = jnp.concatenate(k_before[lead - 1 - d:] + [k_diag[:S - d]], axis=0)
            v = jnp.concatenate(v_before[lead - 1 - d:] + [v_diag[:S - d]], axis=0)
            if at_sequence_start:
                ok = jnp.stack([(n * S + s - d) * BLK + col >= PAD for s in range(S)])
                ok = ok & (col < row) if d == 0 else ok
            else:
                ok = jnp.broadcast_to(col < row, (S, BLK, BLK)) if d == 0 else None
            for hh in range(heads):
                av, runs[hh] = diagonal(hh, k, v, ok, runs[hh])
                accs[hh] = av if d == 0 else accs[hh] + av
        for hh in range(heads):
            acc_ref[hh] = accs[hh]
            run_ref[hh] = runs[hh]
        return keep_going(runs, lead)

    go = lax.cond(n * S <= lead - 1, lambda: leading(True), lambda: leading(False))

    def cond(carry):
        d, go = carry
        return (d <= n * S + S - 1) & (go > 0)

    def body(carry):
        d, _ = carry
        first_blk = n * S - d

        def further(k, v, valid):
            runs = []
            for hh in range(heads):
                av, run = diagonal(hh, k, v, valid, run_ref[hh])
                acc_ref[hh] += av
                run_ref[hh] = run
                runs.append(run)
            return keep_going(runs, d + 1)

        def interior():
            return further(slab(k_ref, first_blk), slab(v_ref, first_blk), None)

        def edge():
            ks, vs, valids = [], [], []
            for s in range(S):
                jb = first_blk + s
                at = pl.multiple_of(jnp.maximum(jb, 0) * BLK, BLK)
                ks.append(k_ref[pl.ds(at, BLK), :])
                vs.append(v_ref[pl.ds(at, BLK), :])
                valids.append(jb * BLK + col >= PAD)
            return further(jnp.stack(ks), jnp.stack(vs), jnp.stack(valids))

        return d + 1, lax.cond(first_blk >= 1, interior, edge)

    lax.while_loop(cond, body, (jnp.int32(lead), go))
    o = acc_ref[0]
    for hh in range(1, heads):
        o = jnp.where(lane.reshape(S, BLK, BLK) >= hh * SB_DH, acc_ref[hh], o)
    o_ref[...] = o.reshape(S * BLK, BLK).astype(o_ref.dtype)


def _stick_breaking(proj, B, Lp):
    T = proj.shape[0]
    nb = Lp // BLK
    S = _blocks_per_step(nb, (13, 5))
    ns = nb // S
    pairs = SB_HEADS * SB_DH // BLK
    heads = BLK // SB_DH
    q0 = (SWA_HEADS + 2 * SWA_KV_HEADS) * SWA_DH // BLK
    k0, v0 = q0 + pairs, q0 + 2 * pairs
    proj3 = proj.reshape(B, Lp, proj.shape[1])
    seq = lambda c0: pl.BlockSpec((None, Lp, BLK), lambda b, p, n: (b, 0, c0 + p))
    return pl.pallas_call(
        functools.partial(_sb_kernel, S=S),
        grid=(B, pairs, ns),
        in_specs=[pl.BlockSpec((S * BLK, BLK), lambda b, p, n: (b * ns + n, q0 + p)), seq(k0), seq(v0)],
        out_specs=pl.BlockSpec((S * BLK, BLK), lambda b, p, n: (b * ns + n, p)),
        out_shape=jax.ShapeDtypeStruct((T, RET_W), BF16),
        scratch_shapes=[pltpu.VMEM((heads, S, BLK, BLK), F32), pltpu.VMEM((heads, S, BLK, BLK), F32)],
        compiler_params=_params(3),
        name="stick_breaking",
    )(proj, proj3, proj3)


def _rotation_tables(Lp):
    half = RET_DK // 2
    inv_freq = 1.0 / (10000.0 ** jnp.linspace(0.0, 1.0, half, dtype=F32))
    pos = jnp.arange(Lp, dtype=F32) - float(PAD)
    ang = pos[:, None] * inv_freq[None, :]
    cos, sin = jnp.cos(ang), jnp.sin(ang)
    return jnp.concatenate([cos, cos], axis=1), jnp.concatenate([-sin, sin], axis=1)


def _ab_weight(w_in):
    D = w_in.shape[0]
    qk = w_in[:, :2 * RET_W].reshape(D, 2 * RET_HEADS, RET_DK // 2, 2)
    qk = jnp.swapaxes(qk, 2, 3).reshape(D, 2 * RET_W)
    return jnp.concatenate([qk, w_in[:, 2 * RET_W:], jnp.zeros((D, AB_COLS - w_in.shape[1]), w_in.dtype)], axis=1)


def kernel(x, meta_tokens, norm_gains, ffn_w_gate, ffn_w_up, ffn_w_down, ab_w_in, ab_conv_w, ab_a_log, ab_dt_bias, ab_out_norm, ab_w_out, cd_w_in, cd_sinks, cd_w_out):
    B, S, D = x.shape
    Lp = S + BLK
    depth = norm_gains.shape[0]
    meta = jnp.broadcast_to(meta_tokens[None].astype(x.dtype), (B, N_META, D))
    tm = _row_tile(B * Lp, 640)
    tm_tok = _row_tile(S, 512)
    token_rows = _token_rows(tm_tok, S, Lp)
    tm_seq = _row_tile(Lp, 640)
    lead_tiles = jnp.concatenate([jnp.zeros((B, PAD, D), x.dtype), meta, x[:, :tm_seq - BLK]], axis=1)
    cos, sin = _rotation_tables(Lp)
    for i in range(depth):
        g = norm_gains[i]
        j = i // 2
        ffn1 = (g[0], g[1], ffn_w_gate[i, 0], ffn_w_up[i, 0], ffn_w_down[i, 0])
        if i == 0:
            h = _ffn(x.reshape(B * S, D), *ffn1, tm=tm_seq, rows=_shifted_token_rows(tm_seq, S, Lp),
                     n_tiles=B * Lp // tm_seq, lead=lead_tiles.reshape(B * tm_seq, D), lead_every=Lp // tm_seq)
        else:
            h = _ffn(h, *ffn1, tm=tm)
        if i % 2 == 0:
            p16, p32 = _proj_ab(h, g[2], ab_w_in[j], ab_conv_w[j], cos, sin, B, Lp)
            mix_a, mix_b = _ab_mixers(p16, p32, ab_a_log[j], ab_dt_bias[j], ab_out_norm[j], B, Lp)
            w_out = ab_w_out[j]
        else:
            w_in = _swa_regroup(cd_w_in[j].astype(BF16), axis=1)
            w_out = _swa_regroup(cd_w_out[j].astype(BF16), axis=0)
            proj = _proj(h, g[2], w_in, BF16, tm=640)
            mix_a = _swa(proj, cd_sinks[j], B, Lp)
            mix_b = _stick_breaking(proj, B, Lp)
        ffn2 = (g[4], g[5], ffn_w_gate[i, 1], ffn_w_up[i, 1], ffn_w_down[i, 1])
        mix = (mix_a, mix_b, g[3], w_out)
        if i == depth - 1:
            h = _ffn(h, *ffn2, tm=tm_tok, rows=token_rows, n_tiles=B * S // tm_tok, mix=mix)
        else:
            h = _ffn(h, *ffn2, tm=tm, mix=mix)
    return h.reshape(B, S, D)
```

```python
import functools
import math

import jax
import jax.numpy as jnp
from jax import lax
from jax.experimental import pallas as pl
from jax.experimental.pallas import tpu as pltpu

F32 = jnp.float32
BF16 = jnp.bfloat16

N_META = 16
NORM_EPS = 1e-6
BLK = 128
PAD = BLK - N_META

RET_HEADS, RET_DK, RET_DV = 4, 128, 128
GDN_HEADS, GDN_DK, GDN_DV, GDN_CONV = 4, 128, 128, 4
SWA_HEADS, SWA_KV_HEADS, SWA_DH = 8, 2, 64
SB_HEADS, SB_DH = 8, 64
SOLVE_BLK = 16

RET_W = RET_HEADS * RET_DK
AB_GATE_COL = 4096
AB_COLS = AB_GATE_COL + BLK
CONV_HALO = 8

VMEM_LIMIT = 56 * 1024 * 1024
NEG_BIG = -1e30
SB_SKIP = -87.5
SB_LEAD = 3


def _rms(x, g):
    return x * lax.rsqrt(jnp.mean(x * x, axis=-1, keepdims=True) + NORM_EPS) * g


def _silu(x):
    return x * jax.nn.sigmoid(x)


def _dot(a, b):
    return jnp.dot(a, b, preferred_element_type=F32)


def _dot_nt(a, b):
    return lax.dot_general(a, b, (((1,), (1,)), ((), ())), preferred_element_type=F32)


def _dot_tn(a, b):
    return lax.dot_general(a, b, (((0,), (0,)), ((), ())), preferred_element_type=F32)


def _split(a):
    hi = a.astype(BF16)
    return hi, (a - hi.astype(F32)).astype(BF16)


def _dot3(a, b, dot=_dot):
    ah, al = _split(a)
    bh, bl = _split(b)
    return dot(ah, bh) + (dot(ah, bl) + dot(al, bh))


def _params(n_grid, parallel=True):
    sem = ("parallel",) + ("arbitrary",) * (n_grid - 1) if parallel else ("arbitrary",) * n_grid
    return pltpu.CompilerParams(dimension_semantics=sem, vmem_limit_bytes=VMEM_LIMIT)


def _const_spec(shape):
    nd = len(shape)
    return pl.BlockSpec(shape, lambda *_: (0,) * nd, pipeline_mode=pl.Buffered(1))


def _row_tile(rows, target):
    tile = target
    while rows % tile:
        tile -= BLK
    return tile


def _blocks_per_step(nb, prefs):
    for s in prefs:
        if nb % s == 0:
            return s
    return 1


def _ffn_kernel(h_ref, gpre_ref, gpost_ref, wg_ref, wu_ref, wd_ref, *rest, tf, lead_every, mixed):
    o_ref, xn_ref, act_ref = rest[-3:]
    h = h_ref[...]
    if lead_every:
        h = jnp.where(pl.program_id(0) % lead_every == 0, rest[0][...], h)
    if mixed:
        a_ref, b_ref, gmix_ref, wa_ref, wb_ref = rest[:5]
        h = h + _rms(_dot(a_ref[...], wa_ref[...]) + _dot(b_ref[...], wb_ref[...]), gmix_ref[...])
    xn_ref[...] = _rms(h, gpre_ref[...]).astype(BF16)
    for c in range(0, wg_ref.shape[1], tf):
        xn = xn_ref[...]
        g = _dot(xn, wg_ref[:, c:c + tf])
        u = _dot(xn, wu_ref[:, c:c + tf])
        act_ref[:, c:c + tf] = (_silu(g) * u).astype(BF16)
    y = _dot(act_ref[...], wd_ref[...])
    o_ref[...] = h + 0.5 * _rms(y, gpost_ref[...])


def _ffn(src, g_pre, g_post, w_gate, w_up, w_down, *, tm, rows=None, n_tiles=None, lead=None, lead_every=0,
         mix=None, tf=256):
    D = src.shape[1]
    F = w_gate.shape[1]
    n_tiles = n_tiles or src.shape[0] // tm
    rows = rows or (lambda width: pl.BlockSpec((tm, width), lambda i: (i, 0)))
    operands = [src, g_pre.reshape(1, D), g_post.reshape(1, D),
                w_gate.astype(BF16), w_up.astype(BF16), w_down.astype(BF16)]
    in_specs = [rows(D), _const_spec((1, D)), _const_spec((1, D)),
                _const_spec((D, F)), _const_spec((D, F)), _const_spec((F, D))]
    if lead is not None:
        operands.append(lead)
        in_specs.append(pl.BlockSpec((tm, D), lambda i: (i // lead_every, 0)))
    if mix is not None:
        a, b, g_mix, w_out = mix
        Ka, Kb = a.shape[1], b.shape[1]
        w_out = w_out.astype(BF16)
        operands += [a, b, g_mix.reshape(1, D), w_out[:Ka], w_out[Ka:]]
        in_specs += [rows(Ka), rows(Kb), _const_spec((1, D)), _const_spec((Ka, D)), _const_spec((Kb, D))]
    return pl.pallas_call(
        functools.partial(_ffn_kernel, tf=tf, lead_every=lead_every, mixed=mix is not None),
        grid=(n_tiles,),
        in_specs=in_specs,
        out_specs=pl.BlockSpec((tm, D), lambda i: (i, 0)),
        out_shape=jax.ShapeDtypeStruct((n_tiles * tm, D), F32),
        scratch_shapes=[pltpu.VMEM((tm, D), BF16), pltpu.VMEM((tm, F), BF16)],
        compiler_params=_params(1),
        name="ffn",
    )(*operands)


def _shifted_token_rows(tm, S, Lp):
    per_batch = Lp // tm
    return lambda width: pl.BlockSpec(
        (pl.Element(tm), pl.Element(width)),
        lambda i: (BLK * ((i // per_batch) * (S // BLK) + jnp.maximum((i % per_batch) * (tm // BLK) - 1, 0)), 0))


def _token_rows(tm, S, Lp):
    per_batch = S // tm
    return lambda width: pl.BlockSpec(
        (pl.Element(tm), pl.Element(width)),
        lambda i: (BLK * ((i // per_batch) * (Lp // BLK) + 1 + (i % per_batch) * (tm // BLK)), 0))


def _proj_kernel(h_ref, g_ref, w_ref, o_ref, *, tn):
    xn = _rms(h_ref[...], g_ref[...]).astype(BF16)
    for c in range(0, w_ref.shape[1], tn):
        o_ref[:, c:c + tn] = _dot(xn, w_ref[:, c:c + tn]).astype(o_ref.dtype)


def _proj(h, g, w, out_dtype, *, tm=512, tn=256):
    T, D = h.shape
    N = w.shape[1]
    tm = _row_tile(T, tm)
    return pl.pallas_call(
        functools.partial(_proj_kernel, tn=tn),
        grid=(T // tm,),
        in_specs=[pl.BlockSpec((tm, D), lambda i: (i, 0)), _const_spec((1, D)), _const_spec((D, N))],
        out_specs=pl.BlockSpec((tm, N), lambda i: (i, 0)),
        out_shape=jax.ShapeDtypeStruct((T, N), out_dtype),
        compiler_params=_params(1),
        name="in_proj",
    )(h, g.reshape(1, D), w.astype(BF16))


def _proj_ab_kernel(h_ref, g_ref, w_ref, cos_ref, sin_ref, cw_ref, o16_ref, o32_ref, xn_ref, halo_ref, *, tn):
    rows = h_ref.shape[0]
    W = RET_W

    @pl.when(pl.program_id(1) == 0)
    def _start_of_sequence():
        halo_ref[...] = jnp.zeros_like(halo_ref)

    xn_ref[...] = _rms(h_ref[...], g_ref[...]).astype(BF16)
    cos, sin = cos_ref[...], sin_ref[...]
    rot = lambda t: t * cos + pltpu.roll(t, RET_DK // 2, 1) * sin
    l2n = lambda t: t * lax.rsqrt(jnp.sum(t * t, axis=-1, keepdims=True) + NORM_EPS)
    heavy = list(range(4 * W, 7 * W, tn))
    light = [c for c in range(0, AB_GATE_COL, tn) if c not in heavy]
    order = light[:2] + [c for pair in zip(heavy, light[2:]) for c in pair] + light[2 + len(heavy):]
    for c in order:
        pre = _dot(xn_ref[...], w_ref[:, c:c + tn])
        group, off = divmod(c, W)
        if group <= 1:
            scale = 1.0 if group == 0 else RET_DK ** -0.5
            for hs in range(0, tn, RET_DK):
                o16_ref[:, c + hs:c + hs + RET_DK] = (rot(pre[:, hs:hs + RET_DK]) * scale).astype(BF16)
        elif group == 2:
            o16_ref[:, c:c + tn] = pre.astype(BF16)
        elif group == 3:
            o32_ref[:, off:off + tn] = _silu(pre)
        elif group <= 6:
            ch = c - 4 * W
            x_ext = jnp.concatenate([halo_ref[:, ch:ch + tn], pre], axis=0)
            y = cw_ref[GDN_CONV - 1:GDN_CONV, ch:ch + tn] * pre
            for shift in range(1, GDN_CONV):
                tap = GDN_CONV - 1 - shift
                y = y + cw_ref[tap:tap + 1, ch:ch + tn] * pltpu.roll(x_ext, shift, 0)[CONV_HALO:]
            halo_ref[:, ch:ch + tn] = pre[rows - CONV_HALO:rows]
            act = _silu(y)
            if group == 6:
                o32_ref[:, W + ch:W + ch + tn] = act
            else:
                scale = GDN_DK ** -0.5 if group == 4 else 1.0
                for hs in range(0, tn, GDN_DK):
                    o32_ref[:, W + ch + hs:W + ch + hs + GDN_DK] = l2n(act[:, hs:hs + GDN_DK]) * scale
        else:
            o32_ref[:, 4 * W + off:4 * W + off + tn] = _silu(pre)
    o32_ref[:, 5 * W:5 * W + BLK] = _dot(xn_ref[...], w_ref[:, AB_GATE_COL:AB_GATE_COL + BLK])


def _proj_ab(h, g, w_in, conv_w, cos, sin, B, Lp, *, tm=640, tn=256):
    D = h.shape[1]
    tm = _row_tile(Lp, tm)
    W = RET_W
    seq = lambda n: pl.BlockSpec((None, tm, n), lambda b, j: (b, j, 0))
    tab = pl.BlockSpec((tm, RET_DK), lambda b, j: (j, 0))
    return pl.pallas_call(
        functools.partial(_proj_ab_kernel, tn=tn),
        grid=(B, Lp // tm),
        in_specs=[seq(D), _const_spec((1, D)), _const_spec((D, AB_COLS)), tab, tab,
                  _const_spec((GDN_CONV, 3 * W))],
        out_specs=[seq(3 * W), seq(5 * W + BLK)],
        out_shape=[jax.ShapeDtypeStruct((B, Lp, 3 * W), BF16), jax.ShapeDtypeStruct((B, Lp, 5 * W + BLK), F32)],
        scratch_shapes=[pltpu.VMEM((tm, D), BF16), pltpu.VMEM((CONV_HALO, 3 * W), F32)],
        compiler_params=_params(2, parallel=False),
        name="in_proj_ab",
    )(h.reshape(B, Lp, D), g.reshape(1, D), _ab_weight(w_in.astype(BF16)), cos, sin, conv_w.astype(F32))


RET_LOG_GAMMA = [math.log1p(-2.0 ** (-5.0 - hd)) for hd in range(RET_HEADS)]


def _ret_init(s_ref, dec_ref, zx_ref):
    C = BLK
    s_ref[...] = jnp.zeros_like(s_ref)
    diff = (lax.broadcasted_iota(jnp.int32, (C, C), 0) - lax.broadcasted_iota(jnp.int32, (C, C), 1)).astype(F32)
    idx = lax.broadcasted_iota(jnp.int32, (C, RET_DV), 0).astype(F32)
    for hd, lg in enumerate(RET_LOG_GAMMA):
        dec_ref[hd] = jnp.where(diff >= 0, jnp.exp(jnp.maximum(diff, 0.0) * lg), 0.0)
        zx_ref[0, hd] = jnp.exp((C - 1.0 - idx) * lg)
        zx_ref[1, hd] = jnp.exp((idx + 1.0) * lg)


def _ret_step(q_ref, k_ref, v_ref, gate_ref, o_ref, s_ref, dec_ref, zx_ref):
    nbatch, C = q_ref.shape[0], q_ref.shape[1]
    log_gamma = RET_LOG_GAMMA
    chains = [(b, hd) for b in range(nbatch) for hd in range(RET_HEADS)]
    sl = lambda hd: slice(hd * RET_DK, (hd + 1) * RET_DK)
    q = [q_ref[b, :, sl(hd)] for b, hd in chains]
    k = [k_ref[b, :, sl(hd)] for b, hd in chains]
    v = [v_ref[b, :, sl(hd)] for b, hd in chains]
    scores = [_dot_nt(qq, kk) * dec_ref[hd] for qq, kk, (_, hd) in zip(q, k, chains)]
    intra = [_dot(sc.astype(BF16), vv) for sc, vv in zip(scores, v)]
    state = [s_ref[i] for i in range(len(chains))]
    cross = [_dot(qq, st.astype(BF16)) * zx_ref[1, hd] for qq, st, (_, hd) in zip(q, state, chains)]
    kv = [_dot_tn(kk, (vv.astype(F32) * zx_ref[0, hd]).astype(BF16)) for kk, vv, (_, hd) in zip(k, v, chains)]
    for i, (b, hd) in enumerate(chains):
        s_ref[i] = state[i] * math.exp(C * log_gamma[hd]) + kv[i]
        o = intra[i] + cross[i]
        mu = jnp.mean(o, axis=-1, keepdims=True)
        var = jnp.mean(jnp.square(o - mu), axis=-1, keepdims=True)
        o_ref[b, :, sl(hd)] = ((o - mu) * lax.rsqrt(var + NORM_EPS) * gate_ref[b, :, sl(hd)]).astype(o_ref.dtype)


def _unit_lower_inverses(a_mats, eye, same_blk):
    d = [jnp.where(same_blk, a, 0.0) for a in a_mats]
    e = [(a - dd).astype(BF16) for a, dd in zip(a_mats, d)]
    x = [-dd for dd in d]
    t = _product_of_powers([eye + xx for xx in x], x, 3)
    tb = [tt.astype(BF16) for tt in t]
    f = [-_dot(tt, ee) for tt, ee in zip(tb, e)]
    p = _product_of_powers([eye + ff for ff in f], f, 2)
    return [pp.astype(BF16) for pp in p], tb


def _product_of_powers(prod, x, steps):
    size = x[0].shape[1]
    x = [_dot(xb, xb) for xb in [xx.astype(BF16) for xx in x]]
    for step in range(steps):
        xb = [xx.astype(BF16) for xx in x]
        if step == steps - 1:
            return [pp + _dot(xx, pp.astype(BF16)) for pp, xx in zip(prod, xb)]
        both = [_dot(xx, jnp.concatenate([xx, pp.astype(BF16)], axis=1)) for pp, xx in zip(prod, xb)]
        x = [bb[:, :size] for bb in both]
        prod = [pp + bb[:, size:] for pp, bb in zip(prod, both)]


def _gdn_step(q_ref, k_ref, v_ref, z_ref, gate_ref, alog_ref, dtb_ref, onorm_ref, o_ref, s_ref):
    nbatch, C = q_ref.shape[0], q_ref.shape[1]
    row = lax.broadcasted_iota(jnp.int32, (C, C), 0)
    col = lax.broadcasted_iota(jnp.int32, (C, C), 1)
    incl = row >= col
    strict = row > col
    same_blk = (row // SOLVE_BLK) == (col // SOLVE_BLK)
    eye = jnp.where(row == col, 1.0, 0.0).astype(F32)
    tri_incl = jnp.where(incl, 1.0, 0.0).astype(F32)
    tri_upper = jnp.where(row <= col, 1.0, 0.0).astype(F32)

    gates = [gate_ref[b] for b in range(nbatch)]
    beta_all = [jax.nn.sigmoid(gt) for gt in gates]
    g_all = [-jnp.exp(alog_ref[...]) * jax.nn.softplus(gt + dtb_ref[...]) for gt in gates]
    gcum_all = [_dot3(tri_incl, g) for g in g_all]
    gcum_t = [_dot3(g, tri_upper, _dot_tn) for g in g_all]

    chains = [(b, hd) for b in range(nbatch) for hd in range(GDN_HEADS)]
    head = lambda ref, b, hd: ref[b, :, hd * GDN_DK:(hd + 1) * GDN_DK]
    q = [head(q_ref, b, hd) for b, hd in chains]
    k = [head(k_ref, b, hd) for b, hd in chains]
    v = [head(v_ref, b, hd) for b, hd in chains]
    beta = [beta_all[b][:, hd:hd + 1] for b, hd in chains]
    gcum = [gcum_all[b][:, GDN_HEADS + hd:GDN_HEADS + hd + 1] for b, hd in chains]
    gcum_row = [gcum_t[b][GDN_HEADS + hd:GDN_HEADS + hd + 1, :] for b, hd in chains]
    decay = [jnp.where(incl, jnp.exp(jnp.where(incl, gc - gr, 0.0)), 0.0) for gc, gr in zip(gcum, gcum_row)]
    k_beta = [kk * bb for kk, bb in zip(k, beta)]
    scores = [_dot_nt(jnp.concatenate([kbt.astype(BF16), qq.astype(BF16)], axis=0), kk.astype(BF16))
              for kbt, qq, kk in zip(k_beta, q, k)]
    a_mat = [jnp.where(strict, sc[:C] * dc, 0.0) for sc, dc in zip(scores, decay)]
    p_mat, t_inv = _unit_lower_inverses(a_mat, eye, same_blk)
    e_gcum = [jnp.exp(gc) for gc in gcum]
    rhs = [jnp.concatenate([vv * bb, kbt * eg], axis=-1) for vv, bb, kbt, eg in zip(v, beta, k_beta, e_gcum)]
    sol = [_dot(tt, rr.astype(BF16)) for tt, rr in zip(t_inv, rhs)]
    sol = [_dot(pp, ss.astype(BF16)) for pp, ss in zip(p_mat, sol)]
    qk = [jnp.where(incl, sc[C:] * dc, 0.0) for sc, dc in zip(scores, decay)]
    g_last = [gc[C - 1:C, :] for gc in gcum]
    k_tail = [(kk * jnp.exp(gl - gc)).astype(BF16) for kk, gl, gc in zip(k, g_last, gcum)]
    q_dec = [(qq * eg).astype(BF16) for qq, eg in zip(q, e_gcum)]

    state = [s_ref[i] for i in range(len(chains))]
    state_b = [st.astype(BF16) for st in state]
    v_new = [ss[:, :GDN_DV] - _dot(ss[:, GDN_DV:].astype(BF16), sb) for ss, sb in zip(sol, state_b)]
    v_new_b = [vn.astype(BF16) for vn in v_new]
    out = [_dot(qd, sb) + _dot(qkm.astype(BF16), vn) for qd, sb, qkm, vn in zip(q_dec, state_b, qk, v_new_b)]
    for i, (b, hd) in enumerate(chains):
        s_ref[i] = state[i] * jnp.exp(g_last[i]) + _dot_tn(k_tail[i], v_new_b[i])
        sl = slice(hd * GDN_DV, (hd + 1) * GDN_DV)
        o_ref[b, :, sl] = (_rms(out[i], onorm_ref[...]) * z_ref[b, :, sl]).astype(o_ref.dtype)


def _ab_mixers_kernel(rq_ref, rk_ref, rv_ref, rgate_ref, gq_ref, gk_ref, gv_ref, gz_ref, gate_ref,
                      alog_ref, dtb_ref, onorm_ref, ret_o_ref, gdn_o_ref, ret_s_ref, dec_ref, zx_ref, gdn_s_ref):
    @pl.when(pl.program_id(0) == 0)
    def _init():
        _ret_init(ret_s_ref, dec_ref, zx_ref)
        gdn_s_ref[...] = jnp.zeros_like(gdn_s_ref)

    _ret_step(rq_ref, rk_ref, rv_ref, rgate_ref, ret_o_ref, ret_s_ref, dec_ref, zx_ref)
    _gdn_step(gq_ref, gk_ref, gv_ref, gz_ref, gate_ref, alog_ref, dtb_ref, onorm_ref, gdn_o_ref, gdn_s_ref)


def _ab_mixers(p16, p32, a_log, dt_bias, out_norm, B, Lp):
    C = BLK
    col = lambda j: pl.BlockSpec((B, C, RET_W), lambda n: (0, n, j))
    gate_lanes = jnp.zeros((1, BLK), F32)
    alog = gate_lanes.at[0, GDN_HEADS:2 * GDN_HEADS].set(a_log.astype(F32))
    dtb = gate_lanes.at[0, GDN_HEADS:2 * GDN_HEADS].set(dt_bias.astype(F32))
    out_shape = jax.ShapeDtypeStruct((B, Lp, RET_W), BF16)
    ret, gdn = pl.pallas_call(
        _ab_mixers_kernel,
        grid=(Lp // C,),
        in_specs=[col(0), col(1), col(2), col(0), col(1), col(2), col(3), col(4),
                  pl.BlockSpec((B, C, BLK), lambda n: (0, n, 5 * RET_W // BLK)),
                  _const_spec((1, BLK)), _const_spec((1, BLK)), _const_spec((1, GDN_DV))],
        out_specs=[col(0), col(0)],
        out_shape=[out_shape, out_shape],
        scratch_shapes=[pltpu.VMEM((B * RET_HEADS, RET_DK, RET_DV), F32),
                        pltpu.VMEM((RET_HEADS, BLK, BLK), F32),
                        pltpu.VMEM((2, RET_HEADS, BLK, RET_DV), F32),
                        pltpu.VMEM((B * GDN_HEADS, GDN_DK, GDN_DV), F32)],
        compiler_params=_params(1, parallel=False),
        name="retention_deltanet",
    )(p16, p16, p16, p32, p32, p32, p32, p32, p32, alog, dtb, out_norm.astype(F32).reshape(1, GDN_DV))
    return ret.reshape(B * Lp, RET_W), gdn.reshape(B * Lp, RET_W)


def _swa_regroup(w, axis):
    G = SWA_HEADS // SWA_KV_HEADS
    width = SWA_HEADS * SWA_DH
    if axis == 0:
        heads = w[:width].reshape(SWA_KV_HEADS, G, SWA_DH, w.shape[1])
        return jnp.concatenate([jnp.swapaxes(heads, 0, 1).reshape(width, w.shape[1]), w[width:]], axis=0)
    heads = w[:, :width].reshape(w.shape[0], SWA_KV_HEADS, G, SWA_DH)
    return jnp.concatenate([jnp.swapaxes(heads, 1, 2).reshape(w.shape[0], width), w[:, width:]], axis=1)


def _swa_kernel(q_ref, kc_ref, vc_ref, kp_ref, vp_ref, km_ref, vm_ref, sink_ref, o_ref, *, S):
    n = pl.program_id(1)
    G = SWA_HEADS // SWA_KV_HEADS
    R = G * BLK
    r = lax.broadcasted_iota(jnp.int32, (R, BLK), 0) & (BLK - 1)
    col = lax.broadcasted_iota(jnp.int32, (R, BLK), 1)
    lower = col <= r
    upper = col > r
    is_meta = col >= PAD
    halves = [col < SWA_DH, col >= SWA_DH]
    k_meta, v_meta = km_ref[...], vm_ref[...]
    for s in range(S):
        blk = n * S + s
        rows = slice(s * BLK, (s + 1) * BLK)
        k_cur, v_cur = kc_ref[rows, :], vc_ref[rows, :]
        if s == 0:
            k_prev, v_prev = kp_ref[...], vp_ref[...]
        else:
            k_prev, v_prev = kc_ref[(s - 1) * BLK:s * BLK, :], vc_ref[(s - 1) * BLK:s * BLK, :]
        cur_ok = lower & (blk >= 1)
        prev_ok = upper & (blk >= 2)
        meta_ok = is_meta & ((blk >= 1) | lower)
        q_st = jnp.concatenate([q_ref[rows, b * BLK:(b + 1) * BLK] for b in range(G)], axis=0) * SWA_DH ** -0.5
        outs = []
        for kv in range(SWA_KV_HEADS):
            q = jnp.where(halves[kv], q_st, 0)
            s_cur = jnp.where(cur_ok, _dot_nt(q, k_cur), NEG_BIG)
            s_prev = jnp.where(prev_ok, _dot_nt(q, k_prev), NEG_BIG)
            s_meta = jnp.where(meta_ok, _dot_nt(q, k_meta), NEG_BIG)
            sink = jnp.concatenate([jnp.full((BLK, 1), sink_ref[kv * G + b], F32) for b in range(G)], axis=0)
            m = jnp.maximum(jnp.max(jnp.maximum(jnp.maximum(s_cur, s_prev), s_meta), axis=-1, keepdims=True), sink)
            p_cur = jnp.exp(s_cur - m)
            p_prev = jnp.exp(s_prev - m)
            p_meta = jnp.exp(s_meta - m)
            denom = jnp.sum(p_cur + p_prev + p_meta, axis=-1, keepdims=True) + jnp.exp(sink - m)
            o = (_dot(p_cur.astype(BF16), v_cur) + _dot(p_prev.astype(BF16), v_prev)
                 + _dot(p_meta.astype(BF16), v_meta))
            outs.append(o / denom)
        o = jnp.where(halves[0], outs[0], outs[1])
        for b in range(G):
            o_ref[rows, b * BLK:(b + 1) * BLK] = o[b * BLK:(b + 1) * BLK, :].astype(o_ref.dtype)


def _swa(proj, sinks, B, Lp):
    T = proj.shape[0]
    nb = Lp // BLK
    S = _blocks_per_step(nb, (13, 5))
    ns = nb // S
    k_col, v_col = RET_W // BLK, RET_W // BLK + 1
    cur = lambda j: pl.BlockSpec((S * BLK, BLK), lambda b, n: (b * ns + n, j))
    prev = lambda j: pl.BlockSpec((BLK, BLK), lambda b, n: (b * nb + jnp.maximum(n * S - 1, 0), j))
    first = lambda j: pl.BlockSpec((BLK, BLK), lambda b, n: (b * nb, j))
    return pl.pallas_call(
        functools.partial(_swa_kernel, S=S),
        grid=(B, ns),
        in_specs=[pl.BlockSpec((S * BLK, RET_W), lambda b, n: (b * ns + n, 0)),
                  cur(k_col), cur(v_col), prev(k_col), prev(v_col), first(k_col), first(v_col),
                  pl.BlockSpec(memory_space=pltpu.SMEM)],
        out_specs=pl.BlockSpec((S * BLK, RET_W), lambda b, n: (b * ns + n, 0)),
        out_shape=jax.ShapeDtypeStruct((T, RET_W), BF16),
        compiler_params=_params(2),
        name="swa_sink",
    )(proj, proj, proj, proj, proj, proj, proj, sinks.astype(F32))


def _sb_kernel(q_ref, k_ref, v_ref, o_ref, acc_ref, run_ref, *, S):
    n = pl.program_id(2)
    heads = BLK // SB_DH
    row = lax.broadcasted_iota(jnp.int32, (BLK, BLK), 0)
    col = lax.broadcasted_iota(jnp.int32, (BLK, BLK), 1)
    suffix = jnp.where(row > col, 1.0, 0.0).astype(BF16)
    ones = jnp.ones((BLK, BLK), BF16)
    sum_rhs = jnp.concatenate([jnp.concatenate([suffix, ones], axis=1)] * 2, axis=0)
    lane = lax.broadcasted_iota(jnp.int32, (S * BLK, BLK), 1)
    q_all = q_ref[...] * SB_DH ** -0.5
    q_head = [jnp.where((lane >= hh * SB_DH) & (lane < (hh + 1) * SB_DH), q_all, 0).reshape(S, BLK, BLK)
              for hh in range(heads)]

    def diagonal(hh, k, v, valid, run_in):
        z = jnp.einsum("sqd,skd->sqk", q_head[hh], k, preferred_element_type=F32)
        softplus_neg = jnp.log(1.0 + jnp.exp(-jnp.abs(z)))
        log_beta = jnp.minimum(z, 0.0) - softplus_neg
        log_1m = log_beta - z
        if valid is not None:
            log_1m = jnp.where(valid, log_1m, 0.0)
        hi, lo = _split(log_1m)
        sums = _dot(jnp.concatenate([hi, lo], axis=-1).reshape(S * BLK, 2 * BLK), sum_rhs)
        sums = sums.reshape(S, BLK, 2 * BLK)
        log_stick = sums[..., :BLK] if run_in is None else sums[..., :BLK] + run_in
        a = jnp.exp(log_beta + log_stick)
        if valid is not None:
            a = jnp.where(valid, a, 0.0)
        av = jnp.einsum("sqk,skd->sqd", a.astype(BF16), v, preferred_element_type=F32)
        run = sums[..., BLK:] if run_in is None else run_in + sums[..., BLK:]
        return av, run

    sub_block = lax.broadcasted_iota(jnp.int32, (S, BLK, BLK), 0)

    def keep_going(runs, next_d):
        alive = n * S + sub_block >= next_d
        top = jnp.max(jnp.where(alive, runs[0], NEG_BIG))
        for run in runs[1:]:
            top = jnp.maximum(top, jnp.max(jnp.where(alive, run, NEG_BIG)))
        return (top >= SB_SKIP).astype(jnp.int32)

    def slab(ref, first_blk):
        at = pl.multiple_of(first_blk * BLK, BLK)
        return ref[pl.ds(at, S * BLK), :].reshape(S, BLK, BLK)

    lead = min(SB_LEAD, S)

    def leading(at_sequence_start):
        k_diag, v_diag = slab(k_ref, n * S), slab(v_ref, n * S)
        k_before, v_before = [], []
        for back in range(lead - 1, 0, -1):
            at = pl.multiple_of(jnp.maximum(n * S - back, 0) * BLK, BLK)
            k_before.append(k_ref[pl.ds(at, BLK), :][None])
            v_before.append(v_ref[pl.ds(at, BLK), :][None])
        accs, runs = [None] * heads, [None] * heads
        for d in range(lead):
            k = jnp.concatenate(k_before[lead - 1 - d:] + [k_diag[:S - d]], axis=0)
            v = jnp.concatenate(v_before[lead - 1 - d:] + [v_diag[:S - d]], axis=0)
            if at_sequence_start:
                ok = jnp.stack([(n * S + s - d) * BLK + col >= PAD for s in range(S)])
                ok = ok & (col < row) if d == 0 else ok
            else:
                ok = jnp.broadcast_to(col < row, (S, BLK, BLK)) if d == 0 else None
            for hh in range(heads):
                av, runs[hh] = diagonal(hh, k, v, ok, runs[hh])
                accs[hh] = av if d == 0 else accs[hh] + av
        for hh in range(heads):
            acc_ref[hh] = accs[hh]
            run_ref[hh] = runs[hh]
        return keep_going(runs, lead)

    go = lax.cond(n * S <= lead - 1, lambda: leading(True), lambda: leading(False))

    def cond(carry):
        d, go = carry
        return (d <= n * S + S - 1) & (go > 0)

    def body(carry):
        d, _ = carry
        first_blk = n * S - d

        def further(k, v, valid):
            runs = []
            for hh in range(heads):
                av, run = diagonal(hh, k, v, valid, run_ref[hh])
                acc_ref[hh] += av
                run_ref[hh] = run
                runs.append(run)
            return keep_going(runs, d + 1)

        def interior():
            return further(slab(k_ref, first_blk), slab(v_ref, first_blk), None)

        def edge():
            ks, vs, valids = [], [], []
            for s in range(S):
                jb = first_blk + s
                at = pl.multiple_of(jnp.maximum(jb, 0) * BLK, BLK)
                ks.append(k_ref[pl.ds(at, BLK), :])
                vs.append(v_ref[pl.ds(at, BLK), :])
                valids.append(jb * BLK + col >= PAD)
            return further(jnp.stack(ks), jnp.stack(vs), jnp.stack(valids))

        return d + 1, lax.cond(first_blk >= 1, interior, edge)

    lax.while_loop(cond, body, (jnp.int32(lead), go))
    o = acc_ref[0]
    for hh in range(1, heads):
        o = jnp.where(lane.reshape(S, BLK, BLK) >= hh * SB_DH, acc_ref[hh], o)
    o_ref[...] = o.reshape(S * BLK, BLK).astype(o_ref.dtype)


def _stick_breaking(proj, B, Lp):
    T = proj.shape[0]
    nb = Lp // BLK
    S = _blocks_per_step(nb, (13, 5))
    ns = nb // S
    pairs = SB_HEADS * SB_DH // BLK
    heads = BLK // SB_DH
    q0 = (SWA_HEADS + 2 * SWA_KV_HEADS) * SWA_DH // BLK
    k0, v0 = q0 + pairs, q0 + 2 * pairs
    proj3 = proj.reshape(B, Lp, proj.shape[1])
    seq = lambda c0: pl.BlockSpec((None, Lp, BLK), lambda b, p, n: (b, 0, c0 + p))
    return pl.pallas_call(
        functools.partial(_sb_kernel, S=S),
        grid=(B, pairs, ns),
        in_specs=[pl.BlockSpec((S * BLK, BLK), lambda b, p, n: (b * ns + n, q0 + p)), seq(k0), seq(v0)],
        out_specs=pl.BlockSpec((S * BLK, BLK), lambda b, p, n: (b * ns + n, p)),
        out_shape=jax.ShapeDtypeStruct((T, RET_W), BF16),
        scratch_shapes=[pltpu.VMEM((heads, S, BLK, BLK), F32), pltpu.VMEM((heads, S, BLK, BLK), F32)],
        compiler_params=_params(3),
        name="stick_breaking",
    )(proj, proj3, proj3)


def _rotation_tables(Lp):
    half = RET_DK // 2
    inv_freq = 1.0 / (10000.0 ** jnp.linspace(0.0, 1.0, half, dtype=F32))
    pos = jnp.arange(Lp, dtype=F32) - float(PAD)
    ang = pos[:, None] * inv_freq[None, :]
    cos, sin = jnp.cos(ang), jnp.sin(ang)
    return jnp.concatenate([cos, cos], axis=1), jnp.concatenate([-sin, sin], axis=1)


def _ab_weight(w_in):
    D = w_in.shape[0]
    qk = w_in[:, :2 * RET_W].reshape(D, 2 * RET_HEADS, RET_DK // 2, 2)
    qk = jnp.swapaxes(qk, 2, 3).reshape(D, 2 * RET_W)
    return jnp.concatenate([qk, w_in[:, 2 * RET_W:], jnp.zeros((D, AB_COLS - w_in.shape[1]), w_in.dtype)], axis=1)


def kernel(x, meta_tokens, norm_gains, ffn_w_gate, ffn_w_up, ffn_w_down, ab_w_in, ab_conv_w, ab_a_log, ab_dt_bias, ab_out_norm, ab_w_out, cd_w_in, cd_sinks, cd_w_out):
    B, S, D = x.shape
    Lp = S + BLK
    depth = norm_gains.shape[0]
    meta = jnp.broadcast_to(meta_tokens[None].astype(x.dtype), (B, N_META, D))
    tm = _row_tile(B * Lp, 640)
    tm_tok = _row_tile(S, 1024)
    token_rows = _token_rows(tm_tok, S, Lp)
    tm_seq = _row_tile(Lp, 640)
    lead_tiles = jnp.concatenate([jnp.zeros((B, PAD, D), x.dtype), meta, x[:, :tm_seq - BLK]], axis=1)
    cos, sin = _rotation_tables(Lp)
    for i in range(depth):
        g = norm_gains[i]
        j = i // 2
        ffn1 = (g[0], g[1], ffn_w_gate[i, 0], ffn_w_up[i, 0], ffn_w_down[i, 0])
        if i == 0:
            h = _ffn(x.reshape(B * S, D), *ffn1, tm=tm_seq, rows=_shifted_token_rows(tm_seq, S, Lp),
                     n_tiles=B * Lp // tm_seq, lead=lead_tiles.reshape(B * tm_seq, D), lead_every=Lp // tm_seq)
        else:
            h = _ffn(h, *ffn1, tm=tm)
        if i % 2 == 0:
            p16, p32 = _proj_ab(h, g[2], ab_w_in[j], ab_conv_w[j], cos, sin, B, Lp)
            mix_a, mix_b = _ab_mixers(p16, p32, ab_a_log[j], ab_dt_bias[j], ab_out_norm[j], B, Lp)
            w_out = ab_w_out[j]
        else:
            w_in = _swa_regroup(cd_w_in[j].astype(BF16), axis=1)
            w_out = _swa_regroup(cd_w_out[j].astype(BF16), axis=0)
            proj = _proj(h, g[2], w_in, BF16, tm=640)
            mix_a = _swa(proj, cd_sinks[j], B, Lp)
            mix_b = _stick_breaking(proj, B, Lp)
        ffn2 = (g[4], g[5], ffn_w_gate[i, 1], ffn_w_up[i, 1], ffn_w_down[i, 1])
        mix = (mix_a, mix_b, g[3], w_out)
        if i == depth - 1:
            h = _ffn(h, *ffn2, tm=tm_tok, rows=token_rows, n_tiles=B * S // tm_tok, mix=mix)
        else:
            h = _ffn(h, *ffn2, tm=tm, mix=mix)
    return h.reshape(B, S, D)
```

```python
import functools
import math

import jax
import jax.numpy as jnp
from jax import lax
from jax.experimental import pallas as pl
from jax.experimental.pallas import tpu as pltpu

F32 = jnp.float32
BF16 = jnp.bfloat16

N_META = 16
NORM_EPS = 1e-6
BLK = 128
PAD = BLK - N_META

RET_HEADS, RET_DK, RET_DV = 4, 128, 128
GDN_HEADS, GDN_DK, GDN_DV, GDN_CONV = 4, 128, 128, 4
SWA_HEADS, SWA_KV_HEADS, SWA_DH = 8, 2, 64
SB_HEADS, SB_DH = 8, 64
SOLVE_BLK = 16

RET_W = RET_HEADS * RET_DK
AB_GATE_COL = 4096
AB_COLS = AB_GATE_COL + BLK
CONV_HALO = 8

VMEM_LIMIT = 56 * 1024 * 1024
NEG_BIG = -1e30
SB_SKIP = -87.5
SB_LEAD = 3


def _rms(x, g):
    return x * lax.rsqrt(jnp.mean(x * x, axis=-1, keepdims=True) + NORM_EPS) * g


def _silu(x):
    return x * jax.nn.sigmoid(x)


def _dot(a, b):
    return jnp.dot(a, b, preferred_element_type=F32)


def _dot_nt(a, b):
    return lax.dot_general(a, b, (((1,), (1,)), ((), ())), preferred_element_type=F32)


def _dot_tn(a, b):
    return lax.dot_general(a, b, (((0,), (0,)), ((), ())), preferred_element_type=F32)


def _split(a):
    hi = a.astype(BF16)
    return hi, (a - hi.astype(F32)).astype(BF16)


def _dot3(a, b, dot=_dot):
    ah, al = _split(a)
    bh, bl = _split(b)
    return dot(ah, bh) + (dot(ah, bl) + dot(al, bh))


def _params(n_grid, parallel=True):
    sem = ("parallel",) + ("arbitrary",) * (n_grid - 1) if parallel else ("arbitrary",) * n_grid
    return pltpu.CompilerParams(dimension_semantics=sem, vmem_limit_bytes=VMEM_LIMIT)


def _const_spec(shape):
    nd = len(shape)
    return pl.BlockSpec(shape, lambda *_: (0,) * nd, pipeline_mode=pl.Buffered(1))


def _row_tile(rows, target):
    tile = target
    while rows % tile:
        tile -= BLK
    return tile


def _blocks_per_step(nb, prefs):
    for s in prefs:
        if nb % s == 0:
            return s
    return 1


def _ffn_kernel(h_ref, gpre_ref, gpost_ref, wg_ref, wu_ref, wd_ref, *rest, tf, lead_every, mixed):
    o_ref, xn_ref, act_ref = rest[-3:]
    h = h_ref[...]
    if lead_every:
        h = jnp.where(pl.program_id(0) % lead_every == 0, rest[0][...], h)
    if mixed:
        a_ref, b_ref, gmix_ref, wa_ref, wb_ref = rest[:5]
        h = h + _rms(_dot(a_ref[...], wa_ref[...]) + _dot(b_ref[...], wb_ref[...]), gmix_ref[...])
    xn_ref[...] = _rms(h, gpre_ref[...]).astype(BF16)
    for c in range(0, wg_ref.shape[1], tf):
        xn = xn_ref[...]
        g = _dot(xn, wg_ref[:, c:c + tf])
        u = _dot(xn, wu_ref[:, c:c + tf])
        act_ref[:, c:c + tf] = (_silu(g) * u).astype(BF16)
    y = _dot(act_ref[...], wd_ref[...])
    o_ref[...] = h + 0.5 * _rms(y, gpost_ref[...])


def _ffn(src, g_pre, g_post, w_gate, w_up, w_down, *, tm, rows=None, n_tiles=None, lead=None, lead_every=0,
         mix=None, tf=256):
    D = src.shape[1]
    F = w_gate.shape[1]
    n_tiles = n_tiles or src.shape[0] // tm
    rows = rows or (lambda width: pl.BlockSpec((tm, width), lambda i: (i, 0)))
    operands = [src, g_pre.reshape(1, D), g_post.reshape(1, D),
                w_gate.astype(BF16), w_up.astype(BF16), w_down.astype(BF16)]
    in_specs = [rows(D), _const_spec((1, D)), _const_spec((1, D)),
                _const_spec((D, F)), _const_spec((D, F)), _const_spec((F, D))]
    if lead is not None:
        operands.append(lead)
        in_specs.append(pl.BlockSpec((tm, D), lambda i: (i // lead_every, 0)))
    if mix is not None:
        a, b, g_mix, w_out = mix
        Ka, Kb = a.shape[1], b.shape[1]
        w_out = w_out.astype(BF16)
        operands += [a, b, g_mix.reshape(1, D), w_out[:Ka], w_out[Ka:]]
        in_specs += [rows(Ka), rows(Kb), _const_spec((1, D)), _const_spec((Ka, D)), _const_spec((Kb, D))]
    return pl.pallas_call(
        functools.partial(_ffn_kernel, tf=tf, lead_every=lead_every, mixed=mix is not None),
        grid=(n_tiles,),
        in_specs=in_specs,
        out_specs=pl.BlockSpec((tm, D), lambda i: (i, 0)),
        out_shape=jax.ShapeDtypeStruct((n_tiles * tm, D), F32),
        scratch_shapes=[pltpu.VMEM((tm, D), BF16), pltpu.VMEM((tm, F), BF16)],
        compiler_params=_params(1),
        name="ffn",
    )(*operands)


def _shifted_token_rows(tm, S, Lp):
    per_batch = Lp // tm
    return lambda width: pl.BlockSpec(
        (pl.Element(tm), pl.Element(width)),
        lambda i: (BLK * ((i // per_batch) * (S // BLK) + jnp.maximum((i % per_batch) * (tm // BLK) - 1, 0)), 0))


def _token_rows(tm, S, Lp):
    per_batch = S // tm
    return lambda width: pl.BlockSpec(
        (pl.Element(tm), pl.Element(width)),
        lambda i: (BLK * ((i // per_batch) * (Lp // BLK) + 1 + (i % per_batch) * (tm // BLK)), 0))


def _proj_kernel(h_ref, g_ref, w_ref, o_ref, *, tn):
    xn = _rms(h_ref[...], g_ref[...]).astype(BF16)
    for c in range(0, w_ref.shape[1], tn):
        o_ref[:, c:c + tn] = _dot(xn, w_ref[:, c:c + tn]).astype(o_ref.dtype)


def _proj(h, g, w, out_dtype, *, tm=512, tn=256):
    T, D = h.shape
    N = w.shape[1]
    tm = _row_tile(T, tm)
    return pl.pallas_call(
        functools.partial(_proj_kernel, tn=tn),
        grid=(T // tm,),
        in_specs=[pl.BlockSpec((tm, D), lambda i: (i, 0)), _const_spec((1, D)), _const_spec((D, N))],
        out_specs=pl.BlockSpec((tm, N), lambda i: (i, 0)),
        out_shape=jax.ShapeDtypeStruct((T, N), out_dtype),
        compiler_params=_params(1),
        name="in_proj",
    )(h, g.reshape(1, D), w.astype(BF16))


def _proj_ab_kernel(h_ref, g_ref, w_ref, cos_ref, sin_ref, cw_ref, o16_ref, o32_ref, xn_ref, halo_ref, *, tn):
    rows = h_ref.shape[0]
    W = RET_W

    @pl.when(pl.program_id(1) == 0)
    def _start_of_sequence():
        halo_ref[...] = jnp.zeros_like(halo_ref)

    xn_ref[...] = _rms(h_ref[...], g_ref[...]).astype(BF16)
    cos, sin = cos_ref[...], sin_ref[...]
    rot = lambda t: t * cos + pltpu.roll(t, RET_DK // 2, 1) * sin
    l2n = lambda t: t * lax.rsqrt(jnp.sum(t * t, axis=-1, keepdims=True) + NORM_EPS)
    heavy = list(range(4 * W, 7 * W, tn))
    light = [c for c in range(0, AB_GATE_COL, tn) if c not in heavy]
    order = light[:2] + [c for pair in zip(heavy, light[2:]) for c in pair] + light[2 + len(heavy):]
    for c in order:
        pre = _dot(xn_ref[...], w_ref[:, c:c + tn])
        group, off = divmod(c, W)
        if group <= 1:
            scale = 1.0 if group == 0 else RET_DK ** -0.5
            for hs in range(0, tn, RET_DK):
                o16_ref[:, c + hs:c + hs + RET_DK] = (rot(pre[:, hs:hs + RET_DK]) * scale).astype(BF16)
        elif group == 2:
            o16_ref[:, c:c + tn] = pre.astype(BF16)
        elif group == 3:
            o32_ref[:, off:off + tn] = _silu(pre)
        elif group <= 6:
            ch = c - 4 * W
            x_ext = jnp.concatenate([halo_ref[:, ch:ch + tn], pre], axis=0)
            y = cw_ref[GDN_CONV - 1:GDN_CONV, ch:ch + tn] * pre
            for shift in range(1, GDN_CONV):
                tap = GDN_CONV - 1 - shift
                y = y + cw_ref[tap:tap + 1, ch:ch + tn] * pltpu.roll(x_ext, shift, 0)[CONV_HALO:]
            halo_ref[:, ch:ch + tn] = pre[rows - CONV_HALO:rows]
            act = _silu(y)
            if group == 6:
                o32_ref[:, W + ch:W + ch + tn] = act
            else:
                scale = GDN_DK ** -0.5 if group == 4 else 1.0
                for hs in range(0, tn, GDN_DK):
                    o32_ref[:, W + ch + hs:W + ch + hs + GDN_DK] = l2n(act[:, hs:hs + GDN_DK]) * scale
        else:
            o32_ref[:, 4 * W + off:4 * W + off + tn] = _silu(pre)
    o32_ref[:, 5 * W:5 * W + BLK] = _dot(xn_ref[...], w_ref[:, AB_GATE_COL:AB_GATE_COL + BLK])


def _proj_ab(h, g, w_in, conv_w, cos, sin, B, Lp, *, tm=640, tn=256):
    D = h.shape[1]
    tm = _row_tile(Lp, tm)
    W = RET_W
    seq = lambda n: pl.BlockSpec((None, tm, n), lambda b, j: (b, j, 0))
    tab = pl.BlockSpec((tm, RET_DK), lambda b, j: (j, 0))
    return pl.pallas_call(
        functools.partial(_proj_ab_kernel, tn=tn),
        grid=(B, Lp // tm),
        in_specs=[seq(D), _const_spec((1, D)), _const_spec((D, AB_COLS)), tab, tab,
                  _const_spec((GDN_CONV, 3 * W))],
        out_specs=[seq(3 * W), seq(5 * W + BLK)],
        out_shape=[jax.ShapeDtypeStruct((B, Lp, 3 * W), BF16), jax.ShapeDtypeStruct((B, Lp, 5 * W + BLK), F32)],
        scratch_shapes=[pltpu.VMEM((tm, D), BF16), pltpu.VMEM((CONV_HALO, 3 * W), F32)],
        compiler_params=_params(2, parallel=False),
        name="in_proj_ab",
    )(h.reshape(B, Lp, D), g.reshape(1, D), _ab_weight(w_in.astype(BF16)), cos, sin, conv_w.astype(F32))


RET_LOG_GAMMA = [math.log1p(-2.0 ** (-5.0 - hd)) for hd in range(RET_HEADS)]


def _ret_init(s_ref, dec_ref, zx_ref):
    C = BLK
    s_ref[...] = jnp.zeros_like(s_ref)
    diff = (lax.broadcasted_iota(jnp.int32, (C, C), 0) - lax.broadcasted_iota(jnp.int32, (C, C), 1)).astype(F32)
    idx = lax.broadcasted_iota(jnp.int32, (C, RET_DV), 0).astype(F32)
    for hd, lg in enumerate(RET_LOG_GAMMA):
        dec_ref[hd] = jnp.where(diff >= 0, jnp.exp(jnp.maximum(diff, 0.0) * lg), 0.0)
        zx_ref[0, hd] = jnp.exp((C - 1.0 - idx) * lg)
        zx_ref[1, hd] = jnp.exp((idx + 1.0) * lg)


def _ret_step(q_ref, k_ref, v_ref, gate_ref, o_ref, s_ref, dec_ref, zx_ref):
    nbatch, C = q_ref.shape[0], q_ref.shape[1]
    log_gamma = RET_LOG_GAMMA
    chains = [(b, hd) for b in range(nbatch) for hd in range(RET_HEADS)]
    sl = lambda hd: slice(hd * RET_DK, (hd + 1) * RET_DK)
    q = [q_ref[b, :, sl(hd)] for b, hd in chains]
    k = [k_ref[b, :, sl(hd)] for b, hd in chains]
    v = [v_ref[b, :, sl(hd)] for b, hd in chains]
    scores = [_dot_nt(qq, kk) * dec_ref[hd] for qq, kk, (_, hd) in zip(q, k, chains)]
    intra = [_dot(sc.astype(BF16), vv) for sc, vv in zip(scores, v)]
    state = [s_ref[i] for i in range(len(chains))]
    cross = [_dot(qq, st.astype(BF16)) * zx_ref[1, hd] for qq, st, (_, hd) in zip(q, state, chains)]
    kv = [_dot_tn(kk, (vv.astype(F32) * zx_ref[0, hd]).astype(BF16)) for kk, vv, (_, hd) in zip(k, v, chains)]
    for i, (b, hd) in enumerate(chains):
        s_ref[i] = state[i] * math.exp(C * log_gamma[hd]) + kv[i]
        o = intra[i] + cross[i]
        mu = jnp.mean(o, axis=-1, keepdims=True)
        var = jnp.mean(jnp.square(o - mu), axis=-1, keepdims=True)
        o_ref[b, :, sl(hd)] = ((o - mu) * lax.rsqrt(var + NORM_EPS) * gate_ref[b, :, sl(hd)]).astype(o_ref.dtype)


def _unit_lower_inverses(a_mats, eye, same_blk):
    d = [jnp.where(same_blk, a, 0.0) for a in a_mats]
    e = [(a - dd).astype(BF16) for a, dd in zip(a_mats, d)]
    x = [-dd for dd in d]
    t = _product_of_powers([eye + xx for xx in x], x, 3)
    tb = [tt.astype(BF16) for tt in t]
    f = [-_dot(tt, ee) for tt, ee in zip(tb, e)]
    p = _product_of_powers([eye + ff for ff in f], f, 2)
    return [pp.astype(BF16) for pp in p], tb


def _product_of_powers(prod, x, steps):
    size = x[0].shape[1]
    x = [_dot(xb, xb) for xb in [xx.astype(BF16) for xx in x]]
    for step in range(steps):
        xb = [xx.astype(BF16) for xx in x]
        if step == steps - 1:
            return [pp + _dot(xx, pp.astype(BF16)) for pp, xx in zip(prod, xb)]
        both = [_dot(xx, jnp.concatenate([xx, pp.astype(BF16)], axis=1)) for pp, xx in zip(prod, xb)]
        x = [bb[:, :size] for bb in both]
        prod = [pp + bb[:, size:] for pp, bb in zip(prod, both)]


def _gdn_step(q_ref, k_ref, v_ref, z_ref, gate_ref, alog_ref, dtb_ref, onorm_ref, o_ref, s_ref):
    nbatch, C = q_ref.shape[0], q_ref.shape[1]
    row = lax.broadcasted_iota(jnp.int32, (C, C), 0)
    col = lax.broadcasted_iota(jnp.int32, (C, C), 1)
    incl = row >= col
    strict = row > col
    same_blk = (row // SOLVE_BLK) == (col // SOLVE_BLK)
    eye = jnp.where(row == col, 1.0, 0.0).astype(F32)
    tri_incl = jnp.where(incl, 1.0, 0.0).astype(F32)
    tri_upper = jnp.where(row <= col, 1.0, 0.0).astype(F32)

    gates = [gate_ref[b] for b in range(nbatch)]
    beta_all = [jax.nn.sigmoid(gt) for gt in gates]
    g_all = [-jnp.exp(alog_ref[...]) * jax.nn.softplus(gt + dtb_ref[...]) for gt in gates]
    gcum_all = [_dot3(tri_incl, g) for g in g_all]
    gcum_t = [_dot3(g, tri_upper, _dot_tn) for g in g_all]

    chains = [(b, hd) for b in range(nbatch) for hd in range(GDN_HEADS)]
    head = lambda ref, b, hd: ref[b, :, hd * GDN_DK:(hd + 1) * GDN_DK]
    q = [head(q_ref, b, hd) for b, hd in chains]
    k = [head(k_ref, b, hd) for b, hd in chains]
    v = [head(v_ref, b, hd) for b, hd in chains]
    beta = [beta_all[b][:, hd:hd + 1] for b, hd in chains]
    gcum = [gcum_all[b][:, GDN_HEADS + hd:GDN_HEADS + hd + 1] for b, hd in chains]
    gcum_row = [gcum_t[b][GDN_HEADS + hd:GDN_HEADS + hd + 1, :] for b, hd in chains]
    decay = [jnp.where(incl, jnp.exp(jnp.where(incl, gc - gr, 0.0)), 0.0) for gc, gr in zip(gcum, gcum_row)]
    k_beta = [kk * bb for kk, bb in zip(k, beta)]
    scores = [_dot_nt(jnp.concatenate([kbt.astype(BF16), qq.astype(BF16)], axis=0), kk.astype(BF16))
              for kbt, qq, kk in zip(k_beta, q, k)]
    a_mat = [jnp.where(strict, sc[:C] * dc, 0.0) for sc, dc in zip(scores, decay)]
    p_mat, t_inv = _unit_lower_inverses(a_mat, eye, same_blk)
    e_gcum = [jnp.exp(gc) for gc in gcum]
    rhs = [jnp.concatenate([vv * bb, kbt * eg], axis=-1) for vv, bb, kbt, eg in zip(v, beta, k_beta, e_gcum)]
    sol = [_dot(tt, rr.astype(BF16)) for tt, rr in zip(t_inv, rhs)]
    sol = [_dot(pp, ss.astype(BF16)) for pp, ss in zip(p_mat, sol)]
    qk = [jnp.where(incl, sc[C:] * dc, 0.0) for sc, dc in zip(scores, decay)]
    g_last = [gc[C - 1:C, :] for gc in gcum]
    k_tail = [(kk * jnp.exp(gl - gc)).astype(BF16) for kk, gl, gc in zip(k, g_last, gcum)]
    q_dec = [(qq * eg).astype(BF16) for qq, eg in zip(q, e_gcum)]

    state = [s_ref[i] for i in range(len(chains))]
    state_b = [st.astype(BF16) for st in state]
    v_new = [ss[:, :GDN_DV] - _dot(ss[:, GDN_DV:].astype(BF16), sb) for ss, sb in zip(sol, state_b)]
    v_new_b = [vn.astype(BF16) for vn in v_new]
    out = [_dot(qd, sb) + _dot(qkm.astype(BF16), vn) for qd, sb, qkm, vn in zip(q_dec, state_b, qk, v_new_b)]
    for i, (b, hd) in enumerate(chains):
        s_ref[i] = state[i] * jnp.exp(g_last[i]) + _dot_tn(k_tail[i], v_new_b[i])
        sl = slice(hd * GDN_DV, (hd + 1) * GDN_DV)
        o_ref[b, :, sl] = (_rms(out[i], onorm_ref[...]) * z_ref[b, :, sl]).astype(o_ref.dtype)


def _ab_mixers_kernel(rq_ref, rk_ref, rv_ref, rgate_ref, gq_ref, gk_ref, gv_ref, gz_ref, gate_ref,
                      alog_ref, dtb_ref, onorm_ref, ret_o_ref, gdn_o_ref, ret_s_ref, dec_ref, zx_ref, gdn_s_ref):
    @pl.when(pl.program_id(0) == 0)
    def _init():
        _ret_init(ret_s_ref, dec_ref, zx_ref)
        gdn_s_ref[...] = jnp.zeros_like(gdn_s_ref)

    _ret_step(rq_ref, rk_ref, rv_ref, rgate_ref, ret_o_ref, ret_s_ref, dec_ref, zx_ref)
    _gdn_step(gq_ref, gk_ref, gv_ref, gz_ref, gate_ref, alog_ref, dtb_ref, onorm_ref, gdn_o_ref, gdn_s_ref)


def _ab_mixers(p16, p32, a_log, dt_bias, out_norm, B, Lp):
    C = BLK
    col = lambda j: pl.BlockSpec((B, C, RET_W), lambda n: (0, n, j))
    gate_lanes = jnp.zeros((1, BLK), F32)
    alog = gate_lanes.at[0, GDN_HEADS:2 * GDN_HEADS].set(a_log.astype(F32))
    dtb = gate_lanes.at[0, GDN_HEADS:2 * GDN_HEADS].set(dt_bias.astype(F32))
    out_shape = jax.ShapeDtypeStruct((B, Lp, RET_W), BF16)
    ret, gdn = pl.pallas_call(
        _ab_mixers_kernel,
        grid=(Lp // C,),
        in_specs=[col(0), col(1), col(2), col(0), col(1), col(2), col(3), col(4),
                  pl.BlockSpec((B, C, BLK), lambda n: (0, n, 5 * RET_W // BLK)),
                  _const_spec((1, BLK)), _const_spec((1, BLK)), _const_spec((1, GDN_DV))],
        out_specs=[col(0), col(0)],
        out_shape=[out_shape, out_shape],
        scratch_shapes=[pltpu.VMEM((B * RET_HEADS, RET_DK, RET_DV), F32),
                        pltpu.VMEM((RET_HEADS, BLK, BLK), F32),
                        pltpu.VMEM((2, RET_HEADS, BLK, RET_DV), F32),
                        pltpu.VMEM((B * GDN_HEADS, GDN_DK, GDN_DV), F32)],
        compiler_params=_params(1, parallel=False),
        name="retention_deltanet",
    )(p16, p16, p16, p32, p32, p32, p32, p32, p32, alog, dtb, out_norm.astype(F32).reshape(1, GDN_DV))
    return ret.reshape(B * Lp, RET_W), gdn.reshape(B * Lp, RET_W)


def _swa_regroup(w, axis):
    G = SWA_HEADS // SWA_KV_HEADS
    width = SWA_HEADS * SWA_DH
    if axis == 0:
        heads = w[:width].reshape(SWA_KV_HEADS, G, SWA_DH, w.shape[1])
        return jnp.concatenate([jnp.swapaxes(heads, 0, 1).reshape(width, w.shape[1]), w[width:]], axis=0)
    heads = w[:, :width].reshape(w.shape[0], SWA_KV_HEADS, G, SWA_DH)
    return jnp.concatenate([jnp.swapaxes(heads, 1, 2).reshape(w.shape[0], width), w[:, width:]], axis=1)


def _swa_kernel(q_ref, kc_ref, vc_ref, kp_ref, vp_ref, km_ref, vm_ref, sink_ref, o_ref, *, S):
    n = pl.program_id(1)
    G = SWA_HEADS // SWA_KV_HEADS
    R = G * BLK
    r = lax.broadcasted_iota(jnp.int32, (R, BLK), 0) & (BLK - 1)
    col = lax.broadcasted_iota(jnp.int32, (R, BLK), 1)
    lower = col <= r
    upper = col > r
    is_meta = col >= PAD
    halves = [col < SWA_DH, col >= SWA_DH]
    k_meta, v_meta = km_ref[...], vm_ref[...]
    for s in range(S):
        blk = n * S + s
        rows = slice(s * BLK, (s + 1) * BLK)
        k_cur, v_cur = kc_ref[rows, :], vc_ref[rows, :]
        if s == 0:
            k_prev, v_prev = kp_ref[...], vp_ref[...]
        else:
            k_prev, v_prev = kc_ref[(s - 1) * BLK:s * BLK, :], vc_ref[(s - 1) * BLK:s * BLK, :]
        cur_ok = lower & (blk >= 1)
        prev_ok = upper & (blk >= 2)
        meta_ok = is_meta & ((blk >= 1) | lower)
        q_st = jnp.concatenate([q_ref[rows, b * BLK:(b + 1) * BLK] for b in range(G)], axis=0) * SWA_DH ** -0.5
        outs = []
        for kv in range(SWA_KV_HEADS):
            q = jnp.where(halves[kv], q_st, 0)
            s_cur = jnp.where(cur_ok, _dot_nt(q, k_cur), NEG_BIG)
            s_prev = jnp.where(prev_ok, _dot_nt(q, k_prev), NEG_BIG)
            s_meta = jnp.where(meta_ok, _dot_nt(q, k_meta), NEG_BIG)
            sink = jnp.concatenate([jnp.full((BLK, 1), sink_ref[kv * G + b], F32) for b in range(G)], axis=0)
            m = jnp.maximum(jnp.max(jnp.maximum(jnp.maximum(s_cur, s_prev), s_meta), axis=-1, keepdims=True), sink)
            p_cur = jnp.exp(s_cur - m)
            p_prev = jnp.exp(s_prev - m)
            p_meta = jnp.exp(s_meta - m)
            denom = jnp.sum(p_cur + p_prev + p_meta, axis=-1, keepdims=True) + jnp.exp(sink - m)
            o = (_dot(p_cur.astype(BF16), v_cur) + _dot(p_prev.astype(BF16), v_prev)
                 + _dot(p_meta.astype(BF16), v_meta))
            outs.append(o / denom)
        o = jnp.where(halves[0], outs[0], outs[1])
        for b in range(G):
            o_ref[rows, b * BLK:(b + 1) * BLK] = o[b * BLK:(b + 1) * BLK, :].astype(o_ref.dtype)


def _swa(proj, sinks, B, Lp):
    T = proj.shape[0]
    nb = Lp // BLK
    S = _blocks_per_step(nb, (13, 5))
    ns = nb // S
    k_col, v_col = RET_W // BLK, RET_W // BLK + 1
    cur = lambda j: pl.BlockSpec((S * BLK, BLK), lambda b, n: (b * ns + n, j))
    prev = lambda j: pl.BlockSpec((BLK, BLK), lambda b, n: (b * nb + jnp.maximum(n * S - 1, 0), j))
    first = lambda j: pl.BlockSpec((BLK, BLK), lambda b, n: (b * nb, j))
    return pl.pallas_call(
        functools.partial(_swa_kernel, S=S),
        grid=(B, ns),
        in_specs=[pl.BlockSpec((S * BLK, RET_W), lambda b, n: (b * ns + n, 0)),
                  cur(k_col), cur(v_col), prev(k_col), prev(v_col), first(k_col), first(v_col),
                  pl.BlockSpec(memory_space=pltpu.SMEM)],
        out_specs=pl.BlockSpec((S * BLK, RET_W), lambda b, n: (b * ns + n, 0)),
        out_shape=jax.ShapeDtypeStruct((T, RET_W), BF16),
        compiler_params=_params(2),
        name="swa_sink",
    )(proj, proj, proj, proj, proj, proj, proj, sinks.astype(F32))


def _sb_kernel(q_ref, k_ref, v_ref, o_ref, acc_ref, run_ref, *, S):
    n = pl.program_id(2)
    heads = BLK // SB_DH
    row = lax.broadcasted_iota(jnp.int32, (BLK, BLK), 0)
    col = lax.broadcasted_iota(jnp.int32, (BLK, BLK), 1)
    suffix = jnp.where(row > col, 1.0, 0.0).astype(BF16)
    ones = jnp.ones((BLK, BLK), BF16)
    sum_rhs = jnp.concatenate([jnp.concatenate([suffix, ones], axis=1)] * 2, axis=0)
    lane = lax.broadcasted_iota(jnp.int32, (S * BLK, BLK), 1)
    q_all = q_ref[...] * SB_DH ** -0.5
    q_head = [jnp.where((lane >= hh * SB_DH) & (lane < (hh + 1) * SB_DH), q_all, 0).reshape(S, BLK, BLK)
              for hh in range(heads)]

    def diagonal(hh, k, v, valid, run_in):
        z = jnp.einsum("sqd,skd->sqk", q_head[hh], k, preferred_element_type=F32)
        softplus_neg = jnp.log(1.0 + jnp.exp(-jnp.abs(z)))
        log_beta = jnp.minimum(z, 0.0) - softplus_neg
        log_1m = log_beta - z
        if valid is not None:
            log_1m = jnp.where(valid, log_1m, 0.0)
        hi, lo = _split(log_1m)
        sums = _dot(jnp.concatenate([hi, lo], axis=-1).reshape(S * BLK, 2 * BLK), sum_rhs)
        sums = sums.reshape(S, BLK, 2 * BLK)
        log_stick = sums[..., :BLK] if run_in is None else sums[..., :BLK] + run_in
        a = jnp.exp(log_beta + log_stick)
        if valid is not None:
            a = jnp.where(valid, a, 0.0)
        av = jnp.einsum("sqk,skd->sqd", a.astype(BF16), v, preferred_element_type=F32)
        run = sums[..., BLK:] if run_in is None else run_in + sums[..., BLK:]
        return av, run

    sub_block = lax.broadcasted_iota(jnp.int32, (S, BLK, BLK), 0)

    def keep_going(runs, next_d):
        alive = n * S + sub_block >= next_d
        top = jnp.max(jnp.where(alive, runs[0], NEG_BIG))
        for run in runs[1:]:
            top = jnp.maximum(top, jnp.max(jnp.where(alive, run, NEG_BIG)))
        return (top >= SB_SKIP).astype(jnp.int32)

    def slab(ref, first_blk):
        at = pl.multiple_of(first_blk * BLK, BLK)
        return ref[pl.ds(at, S * BLK), :].reshape(S, BLK, BLK)

    lead = min(SB_LEAD, S)

    def leading(at_sequence_start):
        k_diag, v_diag = slab(k_ref, n * S), slab(v_ref, n * S)
        k_before, v_before = [], []
        for back in range(lead - 1, 0, -1):
            at = pl.multiple_of(jnp.maximum(n * S - back, 0) * BLK, BLK)
            k_before.append(k_ref[pl.ds(at, BLK), :][None])
            v_before.append(v_ref[pl.ds(at, BLK), :][None])
        accs, runs = [None] * heads, [None] * heads
        for d in range(lead):
            k = jnp.concatenate(k_before[lead - 1 - d:] + [k_diag[:S - d]], axis=0)
            v = jnp.concatenate(v_before[lead - 1 - d:] + [v_diag[:S - d]], axis=0)
            if at_sequence_start:
                ok = jnp.stack([(n * S + s - d) * BLK + col >= PAD for s in range(S)])
                ok = ok & (col < row) if d == 0 else ok
            else:
                ok = jnp.broadcast_to(col < row, (S, BLK, BLK)) if d == 0 else None
            for hh in range(heads):
                av, runs[hh] = diagonal(hh, k, v, ok, runs[hh])
                accs[hh] = av if d == 0 else accs[hh] + av
        for hh in range(heads):
            acc_ref[hh] = accs[hh]
            run_ref[hh] = runs[hh]
        return keep_going(runs, lead)

    go = lax.cond(n * S <= lead - 1, lambda: leading(True), lambda: leading(False))

    def cond(carry):
        d, go = carry
        return (d <= n * S + S - 1) & (go > 0)

    def body(carry):
        d, _ = carry
        first_blk = n * S - d

        def further(k, v, valid):
            runs = []
            for hh in range(heads):
                av, run = diagonal(hh, k, v, valid, run_ref[hh])
                acc_ref[hh] += av
                run_ref[hh] = run
                runs.append(run)
            return keep_going(runs, d + 1)

        def interior():
            return further(slab(k_ref, first_blk), slab(v_ref, first_blk), None)

        def edge():
            ks, vs, valids = [], [], []
            for s in range(S):
                jb = first_blk + s
                at = pl.multiple_of(jnp.maximum(jb, 0) * BLK, BLK)
                ks.append(k_ref[pl.ds(at, BLK), :])
                vs.append(v_ref[pl.ds(at, BLK), :])
                valids.append(jb * BLK + col >= PAD)
            return further(jnp.stack(ks), jnp.stack(vs), jnp.stack(valids))

        return d + 1, lax.cond(first_blk >= 1, interior, edge)

    lax.while_loop(cond, body, (jnp.int32(lead), go))
    o = acc_ref[0]
    for hh in range(1, heads):
        o = jnp.where(lane.reshape(S, BLK, BLK) >= hh * SB_DH, acc_ref[hh], o)
    o_ref[...] = o.reshape(S * BLK, BLK).astype(o_ref.dtype)


def _stick_breaking(proj, B, Lp):
    T = proj.shape[0]
    nb = Lp // BLK
    S = _blocks_per_step(nb, (13, 5))
    ns = nb // S
    pairs = SB_HEADS * SB_DH // BLK
    heads = BLK // SB_DH
    q0 = (SWA_HEADS + 2 * SWA_KV_HEADS) * SWA_DH // BLK
    k0, v0 = q0 + pairs, q0 + 2 * pairs
    proj3 = proj.reshape(B, Lp, proj.shape[1])
    seq = lambda c0: pl.BlockSpec((None, Lp, BLK), lambda b, p, n: (b, 0, c0 + p))
    return pl.pallas_call(
        functools.partial(_sb_kernel, S=S),
        grid=(B, pairs, ns),
        in_specs=[pl.BlockSpec((S * BLK, BLK), lambda b, p, n: (b * ns + n, q0 + p)), seq(k0), seq(v0)],
        out_specs=pl.BlockSpec((S * BLK, BLK), lambda b, p, n: (b * ns + n, p)),
        out_shape=jax.ShapeDtypeStruct((T, RET_W), BF16),
        scratch_shapes=[pltpu.VMEM((heads, S, BLK, BLK), F32), pltpu.VMEM((heads, S, BLK, BLK), F32)],
        compiler_params=_params(3),
        name="stick_breaking",
    )(proj, proj3, proj3)


def _rotation_tables(Lp):
    half = RET_DK // 2
    inv_freq = 1.0 / (10000.0 ** jnp.linspace(0.0, 1.0, half, dtype=F32))
    pos = jnp.arange(Lp, dtype=F32) - float(PAD)
    ang = pos[:, None] * inv_freq[None, :]
    cos, sin = jnp.cos(ang), jnp.sin(ang)
    return jnp.concatenate([cos, cos], axis=1), jnp.concatenate([-sin, sin], axis=1)


def _ab_weight(w_in):
    D = w_in.shape[0]
    qk = w_in[:, :2 * RET_W].reshape(D, 2 * RET_HEADS, RET_DK // 2, 2)
    qk = jnp.swapaxes(qk, 2, 3).reshape(D, 2 * RET_W)
    return jnp.concatenate([qk, w_in[:, 2 * RET_W:], jnp.zeros((D, AB_COLS - w_in.shape[1]), w_in.dtype)], axis=1)


def kernel(x, meta_tokens, norm_gains, ffn_w_gate, ffn_w_up, ffn_w_down, ab_w_in, ab_conv_w, ab_a_log, ab_dt_bias, ab_out_norm, ab_w_out, cd_w_in, cd_sinks, cd_w_out):
    B, S, D = x.shape
    Lp = S + BLK
    depth = norm_gains.shape[0]
    meta = jnp.broadcast_to(meta_tokens[None].astype(x.dtype), (B, N_META, D))
    tm = _row_tile(B * Lp, 640)
    tm_tok = _row_tile(S, 1024)
    token_rows = _token_rows(tm_tok, S, Lp)
    tm_seq = _row_tile(Lp, 640)
    lead_tiles = jnp.concatenate([jnp.zeros((B, PAD, D), x.dtype), meta, x[:, :tm_seq - BLK]], axis=1)
    cos, sin = _rotation_tables(Lp)
    for i in range(depth):
        g = norm_gains[i]
        j = i // 2
        ffn1 = (g[0], g[1], ffn_w_gate[i, 0], ffn_w_up[i, 0], ffn_w_down[i, 0])
        if i == 0:
            h = _ffn(x.reshape(B * S, D), *ffn1, tm=tm_seq, rows=_shifted_token_rows(tm_seq, S, Lp),
                     n_tiles=B * Lp // tm_seq, lead=lead_tiles.reshape(B * tm_seq, D), lead_every=Lp // tm_seq)
        else:
            h = _ffn(h, *ffn1, tm=tm)
        if i % 2 == 0:
            p16, p32 = _proj_ab(h, g[2], ab_w_in[j], ab_conv_w[j], cos, sin, B, Lp)
            mix_a, mix_b = _ab_mixers(p16, p32, ab_a_log[j], ab_dt_bias[j], ab_out_norm[j], B, Lp)
            w_out = ab_w_out[j]
        else:
            w_in = _swa_regroup(cd_w_in[j].astype(BF16), axis=1)
            w_out = _swa_regroup(cd_w_out[j].astype(BF16), axis=0)
            proj = _proj(h, g[2], w_in, BF16, tm=1664)
            mix_a = _swa(proj, cd_sinks[j], B, Lp)
            mix_b = _stick_breaking(proj, B, Lp)
        ffn2 = (g[4], g[5], ffn_w_gate[i, 1], ffn_w_up[i, 1], ffn_w_down[i, 1])
        mix = (mix_a, mix_b, g[3], w_out)
        if i == depth - 1:
            h = _ffn(h, *ffn2, tm=tm_tok, rows=token_rows, n_tiles=B * S // tm_tok, mix=mix)
        else:
            h = _ffn(h, *ffn2, tm=tm, mix=mix)
    return h.reshape(B, S, D)
```
